```python
import jax, jax.numpy as jnp
from jax import lax
import numpy as np

D_MODEL = 1024
BATCH = 16
SEQ = 256
DEPTH = 2
DEC_BATCH = 8
DEC_SEQ = 2048
PAST_LEN = 512

GRID_W = 64
CHUNK = 64
H_A = 4
DK_A = 128
DV_A = 128
D_A = H_A * DV_A
H_B = 4
DK_B = 128
DV_B = 128
D_B = H_B * DV_B
D_MIX = D_A + D_B
D_FF = 2816
ROPE_BASE = 10000.0
EPS = 1e-6
GATE_FLOOR = 1e-12
IN_SIZES = [H_A * DK_A, H_A * DK_A, H_A * DK_A, D_A, D_A, H_B * DK_B, H_B * DK_B, D_B, D_B]
IN_OFFSETS = [int(v) for v in np.cumsum(IN_SIZES)[:-1]]
D_IN = int(sum(IN_SIZES))

kernel_name = "hybrid_hgrn2_retention_diffusion_step"


def rmsnorm(x, g):
    x = x.astype(jnp.float32)
    return x * lax.rsqrt(jnp.mean(x * x, axis=-1, keepdims=True) + EPS) * g


def heads(x, h):
    b, l, _ = x.shape
    return x.reshape(b, l, h, -1).transpose(0, 2, 1, 3)


def head_norm(o, g, center):
    if center:
        o = o - jnp.mean(o, axis=-1, keepdims=True)
    o = o * lax.rsqrt(jnp.mean(o * o, axis=-1, keepdims=True) + EPS)
    b, h, l, dv = o.shape
    return o.transpose(0, 2, 1, 3).reshape(b, l, h * dv) * g


def grid_positions(n_tokens):
    rows_n = n_tokens // GRID_W
    rr, cc = jnp.meshgrid(jnp.arange(rows_n, dtype=jnp.float32), jnp.arange(GRID_W, dtype=jnp.float32), indexing="ij")
    return rr.reshape(-1), cc.reshape(-1)


def rope_2d(x, rows, cols):
    half = x.shape[-1] // 2
    quarter = half // 2
    inv = ROPE_BASE ** (-jnp.arange(quarter, dtype=jnp.float32) / quarter)

    def rot(xb, pos):
        ang = pos[:, None] * inv
        cs, sn = jnp.cos(ang), jnp.sin(ang)
        x1, x2 = xb[..., :quarter], xb[..., quarter:]
        return jnp.concatenate([x1 * cs - x2 * sn, x2 * cs + x1 * sn], axis=-1)

    return jnp.concatenate([rot(x[..., :half], rows), rot(x[..., half:], cols)], axis=-1)


def chunk_scan(q, k, v, logf, s0):
    b_, h_, l_, _ = q.shape
    n = l_ // CHUNK

    def blocks(t):
        return t.reshape(b_, h_, n, CHUNK, t.shape[-1]).transpose(2, 0, 1, 3, 4)

    causal = jnp.tril(jnp.ones((CHUNK, CHUNK), dtype=bool))[:, :, None]
    scalar_decay = logf.shape[-1] == 1

    def step(S, blk):
        qc, kc, vc, gc = blk
        bcum = jnp.cumsum(gc, axis=-2)
        diff = bcum[..., :, None, :] - bcum[..., None, :, :]
        decay = jnp.where(causal, jnp.exp(jnp.where(causal, diff, 0.0)), 0.0)
        if scalar_decay:
            A = jnp.einsum("bhtd,bhsd->bhts", qc, kc) * decay[..., 0]
        else:
            A = jnp.sum(qc[..., :, None, :] * decay * kc[..., None, :, :], axis=-1)
        o = jnp.einsum("bhts,bhsv->bhtv", A, vc) + jnp.einsum("bhtd,bhdv->bhtv", qc * jnp.exp(bcum), S)
        b_last = bcum[..., -1:, :]
        S = jnp.exp(b_last)[..., 0, :, None] * S + jnp.einsum("bhsd,bhsv->bhdv", kc * jnp.exp(b_last - bcum), vc)
        return S, o

    S, o = lax.scan(step, s0.astype(jnp.float32), (blocks(q), blocks(k), blocks(v), blocks(logf)))
    return o.transpose(1, 2, 0, 3, 4).reshape(b_, h_, l_, -1), S


def bidirectional_scan(q, k_f, k_b, v, g_f, g_b, s0):
    o_f, S_f = chunk_scan(q, k_f, v, g_f, s0[:, 0])
    flip = lambda t: jnp.flip(t, axis=2)
    o_b, S_b = chunk_scan(flip(q), flip(k_b), flip(v), flip(g_b), s0[:, 1])
    return o_f + flip(o_b), jnp.stack([S_f, S_b], axis=1)


def hgrn_gate(z, lb):
    f = lb + (1.0 - lb) * jax.nn.sigmoid(z)
    logf = jnp.log(jnp.maximum(f, GATE_FLOOR))
    k = (1.0 - lb) * jax.nn.sigmoid(-z)
    return heads(logf, H_A), heads(k, H_A)


def token_mixers(h, p, s0_h, s0_r, pos):
    b_, l_, _ = h.shape
    proj = h @ p["w_in"]
    qa, zf, zb, ia, ga, qb, kb, vb, gb = jnp.split(proj, IN_OFFSETS, axis=-1)
    lg_f, k_f = hgrn_gate(zf, p["lb"][0])
    lg_b, k_b = hgrn_gate(zb, p["lb"][1])
    o_a, S_a = bidirectional_scan(heads(jax.nn.silu(qa), H_A), k_f, k_b, heads(ia, H_A), lg_f, lg_b, s0_h)
    o_a = head_norm(o_a, p["hgrn_norm"], center=False) * jax.nn.silu(ga)
    qh = heads(qb, H_B)
    kh = heads(kb, H_B) * (DK_B ** -0.5)
    if pos is not None:
        qh = rope_2d(qh, *pos)
        kh = rope_2d(kh, *pos)
    log_gamma = jax.nn.log_sigmoid(p["ret_logit"].astype(jnp.float32))
    gam_f = jnp.broadcast_to(log_gamma[0][None, :, None, None], (b_, H_B, l_, 1))
    gam_b = jnp.broadcast_to(log_gamma[1][None, :, None, None], (b_, H_B, l_, 1))
    o_b, S_b = bidirectional_scan(qh, kh, kh, heads(vb, H_B), gam_f, gam_b, s0_r)
    o_b = head_norm(o_b, p["ret_norm"], center=True) * jax.nn.silu(gb)
    return jnp.concatenate([o_a, o_b], axis=-1) @ p["w_out"], S_a, S_b


def layer(x, cond, s0_h, s0_r, pos, p):
    mod = jax.nn.silu(cond.astype(jnp.float32)) @ p["ada_w"] + p["ada_b"]
    mod = mod.reshape(cond.shape[:-1] + (6, D_MODEL))
    sh_m, sc_m, g_m, sh_f, sc_f, g_f = [mod[..., i, :][..., None, :] for i in range(6)]
    h = rmsnorm(x, p["norm_mix"]) * (1.0 + sc_m) + sh_m
    y, S_a, S_b = token_mixers(h, p, s0_h, s0_r, pos)
    x = x + g_m * y
    h = rmsnorm(x, p["norm_ffn"]) * (1.0 + sc_f) + sh_f
    gate, up = jnp.split(h @ p["w_up"], 2, axis=-1)
    x = x + g_f * ((jax.nn.silu(gate) * up) @ p["w_down"])
    return x, S_a, S_b


def setup_inputs(seed: int = 0) -> dict:
    key = jax.random.key(seed)
    ks = jax.random.split(key, 20)
    nrm = lambda k, shape, s: jax.random.normal(k, shape, jnp.float32) * s
    base_logit = jnp.log(2.0 ** (5.0 + jnp.arange(H_B, dtype=jnp.float32)) - 1.0)
    return {
        "x_prompt": nrm(ks[0], (BATCH, SEQ, D_MODEL), 1.0),
        "x_sample": nrm(ks[1], (DEC_BATCH, DEC_SEQ, D_MODEL), 1.0),
        "state_hgrn": nrm(ks[2], (DEC_BATCH, DEPTH, 2, H_A, DK_A, DV_A), 0.5),
        "state_ret": nrm(ks[3], (DEC_BATCH, DEPTH, 2, H_B, DK_B, DV_B), 0.5),
        "c": nrm(ks[4], (DEC_BATCH, D_MODEL), 1.0),
        "c_ctx": nrm(ks[5], (D_MODEL,), 1.0),
        "ada_w": nrm(ks[6], (DEPTH, D_MODEL, 6 * D_MODEL), 0.5 * D_MODEL ** -0.5),
        "ada_b": nrm(ks[7], (DEPTH, 6 * D_MODEL), 0.02),
        "norm_mix": 1.0 + nrm(ks[8], (DEPTH, D_MODEL), 0.05),
        "norm_ffn": 1.0 + nrm(ks[9], (DEPTH, D_MODEL), 0.05),
        "w_in": nrm(ks[10], (DEPTH, D_MODEL, D_IN), D_MODEL ** -0.5),
        "hgrn_lb_logits": nrm(ks[11], (DEPTH, 2, H_A * DK_A), 1.0),
        "hgrn_norm": 1.0 + nrm(ks[12], (DEPTH, D_A), 0.05),
        "ret_decay_logit": base_logit[None, None, :] + nrm(ks[13], (DEPTH, 2, H_B), 0.1),
        "ret_norm": 1.0 + nrm(ks[14], (DEPTH, D_B), 0.05),
        "w_out": nrm(ks[15], (DEPTH, D_MIX, D_MODEL), D_MIX ** -0.5),
        "w_up": nrm(ks[16], (DEPTH, D_MODEL, 2 * D_FF), D_MODEL ** -0.5),
        "w_down": nrm(ks[17], (DEPTH, D_FF, D_MODEL), D_FF ** -0.5),
        "norm_final": 1.0 + nrm(ks[18], (D_MODEL,), 0.05),
    }


def reference(x_prompt, x_sample, state_hgrn, state_ret, c, c_ctx, ada_w, ada_b, norm_mix, norm_ffn, w_in,
              hgrn_lb_logits, hgrn_norm, ret_decay_logit, ret_norm, w_out, w_up, w_down, norm_final):
    pl = jax.nn.softmax(hgrn_lb_logits.astype(jnp.float32), axis=0)
    lower_bounds = jnp.clip(jnp.cumsum(pl, axis=0) - pl[0:1], 0.0, 1.0)
    xp = x_prompt.astype(jnp.float32)
    xs = x_sample.astype(jnp.float32)
    bp = xp.shape[0]
    zeros_h = jnp.zeros((bp, 2, H_A, DK_A, DV_A), jnp.float32)
    zeros_r = jnp.zeros((bp, 2, H_B, DK_B, DV_B), jnp.float32)
    cond_ctx = c_ctx[None, :]
    pos = grid_positions(xs.shape[1])
    new_h, new_r = [], []
    for l in range(DEPTH):
        p = {"ada_w": ada_w[l], "ada_b": ada_b[l], "norm_mix": norm_mix[l], "norm_ffn": norm_ffn[l],
             "w_in": w_in[l], "lb": lower_bounds[l], "hgrn_norm": hgrn_norm[l], "ret_logit": ret_decay_logit[l],
             "ret_norm": ret_norm[l], "w_out": w_out[l], "w_up": w_up[l], "w_down": w_down[l]}
        xp, S_a, S_b = layer(xp, cond_ctx, zeros_h, zeros_r, None, p)
        new_h.append(S_a)
        new_r.append(S_b)
        xs, _, _ = layer(xs, c, state_hgrn[:, l], state_ret[:, l], pos, p)
    y_prompt = rmsnorm(xp, norm_final).astype(x_prompt.dtype)
    y_sample = rmsnorm(xs, norm_final).astype(x_sample.dtype)
    new_state_hgrn = jnp.stack(new_h, axis=1).astype(state_hgrn.dtype)
    new_state_ret = jnp.stack(new_r, axis=1).astype(state_ret.dtype)
    return (y_prompt, y_sample, new_state_hgrn, new_state_ret)
```

```python
import functools

import jax
import jax.numpy as jnp
from jax import lax
from jax.experimental import pallas as pl
from jax.experimental.pallas import tpu as pltpu

F32 = jnp.float32
BF16 = jnp.bfloat16

H_A = 4
H_B = 4
HEAD_DIM = 128
N_SECTIONS = 9
SECTION = 512
D_IN = N_SECTIONS * SECTION
GRID_W = 64
ROPE_BASE = 10000.0
EPS = 1e-6
GATE_FLOOR = 1e-12

N_MOD = 6
COND_ROWS = 16

HGRN_CHUNK = 64
RET_CHUNK = 128
SAFE_EXP_RANGE = 60.0
SAFE_Q_MAX = 1e8
FFN_CHUNK = 256
MOD_TILE_N = 1536
VMEM_LIMIT = 56 * 1024 * 1024


def _sigmoid(x):
    return 1.0 / (1.0 + jnp.exp(-x))


def _silu(x):
    return x * _sigmoid(x)


def _dot(a, b):
    return jnp.dot(a, b, preferred_element_type=F32)


def _dot_nt(a, b):
    return lax.dot_general(a, b, (((1,), (1,)), ((), ())), preferred_element_type=F32)


def _modnorm(x, gain, scale, shift):
    ms = jnp.mean(x * x, axis=-1, keepdims=True)
    return x * lax.rsqrt(ms + EPS) * gain * (1.0 + scale) + shift


def _mod_kernel(cond_ref, w_ref, b_ref, out_ref):
    s = _silu(cond_ref[...])
    out_ref[...] = jnp.dot(s, w_ref[...], preferred_element_type=F32, precision=lax.Precision.HIGHEST) + b_ref[...]


def _modulation(cond, ada_w, ada_b):
    depth, d_model, n_out = ada_w.shape
    out = pl.pallas_call(
        _mod_kernel,
        grid=(depth, n_out // MOD_TILE_N),
        in_specs=[
            pl.BlockSpec((COND_ROWS, d_model), lambda l, j: (0, 0)),
            pl.BlockSpec((None, d_model, MOD_TILE_N), lambda l, j: (l, 0, j)),
            pl.BlockSpec((None, 1, MOD_TILE_N), lambda l, j: (l, 0, j)),
        ],
        out_specs=pl.BlockSpec((None, COND_ROWS, MOD_TILE_N), lambda l, j: (l, 0, j)),
        out_shape=jax.ShapeDtypeStruct((depth, COND_ROWS, n_out), F32),
        compiler_params=pltpu.CompilerParams(vmem_limit_bytes=VMEM_LIMIT),
        name="adaln_mod",
    )(cond, ada_w, ada_b.reshape(depth, 1, n_out))
    return out.reshape(depth, COND_ROWS, N_MOD, d_model)


def _in_proj_kernel(x_ref, mod_ref, gain_ref, w_ref, out_ref):
    mod = mod_ref[...]
    h = _modnorm(x_ref[...], gain_ref[...], mod[1:2], mod[0:1]).astype(BF16)
    for j in range(N_SECTIONS):
        cols = slice(j * SECTION, (j + 1) * SECTION)
        out_ref[:, cols] = _dot(h, w_ref[:, cols])


def _in_proj(x, mod, gain, w, layer, tile_m, cond_index):
    n_tok, d_model = x.shape
    return pl.pallas_call(
        _in_proj_kernel,
        grid=(n_tok // tile_m,),
        in_specs=[
            pl.BlockSpec((tile_m, d_model), lambda i: (i, 0)),
            pl.BlockSpec((None, None, N_MOD, d_model), lambda i: (layer, cond_index(i), 0, 0)),
            pl.BlockSpec((None, 1, d_model), lambda i: (layer, 0, 0)),
            pl.BlockSpec((None, d_model, D_IN), lambda i: (layer, 0, 0), pipeline_mode=pl.Buffered(1)),
        ],
        out_specs=pl.BlockSpec((tile_m, D_IN), lambda i: (i, 0)),
        out_shape=jax.ShapeDtypeStruct((n_tok, D_IN), F32),
        compiler_params=pltpu.CompilerParams(vmem_limit_bytes=VMEM_LIMIT),
        name="in_proj",
    )(x, mod, gain, w)


def _cumsum_rows(g, reverse):
    n = g.shape[0]
    row = lax.broadcasted_iota(jnp.int32, g.shape, 0)
    sh = 1
    while sh < n:
        if reverse:
            g = g + jnp.where(row < n - sh, pltpu.roll(g, n - sh, 0), 0.0)
        else:
            g = g + jnp.where(row >= sh, pltpu.roll(g, sh, 0), 0.0)
        sh *= 2
    return g


def _state_update_terms(v, k_cat):
    return _dot(v.T.astype(BF16), k_cat)


def _hgrn_gates(z, lb, reverse):
    s = _sigmoid(z)
    f = lb + (1.0 - lb) * s
    g = jnp.log(jnp.maximum(f, GATE_FLOOR))
    k = (1.0 - lb) * (1.0 - s)
    return k, _cumsum_rows(g, reverse)


def _hgrn_intra(q, q_safe, k, b, k_scr, b_scr, lower):
    n = q.shape[0]
    d = b - b[n // 2 - 1 : n // 2, :]
    safe = jnp.logical_and(q_safe, jnp.max(jnp.abs(d)) <= SAFE_EXP_RANGE)
    k_scr[...] = k
    b_scr[...] = b

    def factored():
        return _dot_nt((q * jnp.exp(d)).astype(BF16), (k * jnp.exp(-d)).astype(BF16))

    def direct():
        col = lax.broadcasted_iota(jnp.int32, (n, n), 1)

        def body(s, acc):
            ks = k_scr[pl.ds(s, 1), :]
            bs = b_scr[pl.ds(s, 1), :]
            z = q * ks * jnp.exp(jnp.minimum(b - bs, 0.0))
            return jnp.where(col == s, jnp.sum(z, axis=1, keepdims=True), acc)

        return lax.fori_loop(0, n, body, jnp.zeros((n, n), F32))

    a = lax.cond(safe, factored, direct)
    t_idx = lax.broadcasted_iota(jnp.int32, (n, n), 0)
    s_idx = lax.broadcasted_iota(jnp.int32, (n, n), 1)
    return jnp.where(t_idx >= s_idx if lower else t_idx <= s_idx, a, 0.0)


def _hgrn_kernel(*refs, layer, seq_len, has_s0, emit_state):
    qa_ref, zf_ref, zb_ref, ia_ref, ga_ref, lbl_ref, gn_ref = refs[:7]
    refs = refs[7:]
    if has_s0:
        s0_ref, refs = refs[0], refs[1:]
    mix_ref, refs = refs[0], refs[1:]
    if emit_state:
        st_ref, refs = refs[0], refs[1:]
    o_scr, qb_scr, ub_scr, db_scr, sf_scr, sb_scr, k_scr, b_scr = refs

    c_len = HGRN_CHUNK
    n_chunks = seq_len // c_len

    logits = lbl_ref[...]
    e = jnp.exp(logits - jnp.max(logits, axis=0, keepdims=True))
    p = e / jnp.sum(e, axis=0, keepdims=True)
    cum = p[0]
    for i in range(1, layer + 1):
        cum = cum + p[i]
    lb = jnp.clip(cum - p[0], 0.0, 1.0)
    lb_f, lb_b = lb[0:1], lb[1:2]

    if has_s0:
        sf_scr[...] = s0_ref[0].T
        sb_scr[...] = s0_ref[1].T
    else:
        sf_scr[...] = jnp.zeros_like(sf_scr)
        sb_scr[...] = jnp.zeros_like(sb_scr)

    def forward_pass(c, carry):
        rows = pl.ds(pl.multiple_of(c * c_len, c_len), c_len)
        q = _silu(qa_ref[rows, :])
        v = ia_ref[rows, :]
        v16 = v.astype(BF16)
        k_f, b_f = _hgrn_gates(zf_ref[rows, :], lb_f, reverse=False)
        k_b, b_b = _hgrn_gates(zb_ref[rows, :], lb_b, reverse=True)
        q_safe = jnp.max(jnp.abs(q)) <= SAFE_Q_MAX
        a = _hgrn_intra(q, q_safe, k_f, b_f, k_scr, b_scr, lower=True)
        a = a + _hgrn_intra(q, q_safe, k_b, b_b, k_scr, b_scr, lower=False)
        o = _dot(a.astype(BF16), v16)
        o = o + _dot_nt((q * jnp.exp(b_f)).astype(BF16), sf_scr[...].astype(BF16))
        o_scr[rows, :] = o
        tot_f = b_f[c_len - 1 : c_len, :]
        tot_b = b_b[0:1, :]
        k_cat = jnp.concatenate([k_f * jnp.exp(tot_f - b_f), k_b * jnp.exp(tot_b - b_b)], axis=1)
        u = _state_update_terms(v, k_cat.astype(BF16))
        sf_scr[...] = jnp.exp(tot_f) * sf_scr[...] + u[:, :HEAD_DIM]
        ub_scr[c] = u[:, HEAD_DIM:]
        db_scr[c] = jnp.exp(tot_b)
        qb_scr[rows, :] = (q * jnp.exp(b_b)).astype(BF16)
        return carry

    lax.fori_loop(0, n_chunks, forward_pass, 0)

    gn = gn_ref[...]

    def backward_pass(i, carry):
        c = n_chunks - 1 - i
        rows = pl.ds(pl.multiple_of(c * c_len, c_len), c_len)
        o = o_scr[rows, :] + _dot_nt(qb_scr[rows, :], sb_scr[...].astype(BF16))
        sb_scr[...] = db_scr[c] * sb_scr[...] + ub_scr[c]
        o = o * lax.rsqrt(jnp.mean(o * o, axis=-1, keepdims=True) + EPS) * gn
        mix_ref[rows, :] = (o * _silu(ga_ref[rows, :])).astype(BF16)
        return carry

    lax.fori_loop(0, n_chunks, backward_pass, 0)

    if emit_state:
        st_ref[0] = sf_scr[...].T
        st_ref[1] = sb_scr[...].T


def _proj_spec(seq_len, section):
    return pl.BlockSpec((None, seq_len, HEAD_DIM), lambda b, h: (b, 0, section * 4 + h))


def _hgrn(proj, lb_logits, norm_gain, state, layer, emit_state):
    batch, seq_len, _ = proj.shape
    depth = lb_logits.shape[0]
    n_chunks = seq_len // HGRN_CHUNK
    has_s0 = state is not None
    in_specs = [_proj_spec(seq_len, s) for s in range(5)] + [
        pl.BlockSpec((depth, 2, HEAD_DIM), lambda b, h: (0, 0, h)),
        pl.BlockSpec((None, 1, HEAD_DIM), lambda b, h: (layer, 0, h)),
    ]
    args = [proj] * 5 + [lb_logits, norm_gain]
    if has_s0:
        in_specs.append(pl.BlockSpec((None, None, 2, None, HEAD_DIM, HEAD_DIM), lambda b, h: (b, layer, 0, h, 0, 0)))
        args.append(state)
    out_specs = [pl.BlockSpec((None, seq_len, HEAD_DIM), lambda b, h: (b, 0, h))]
    out_shape = [jax.ShapeDtypeStruct((batch, seq_len, H_A * HEAD_DIM), BF16)]
    if emit_state:
        out_specs.append(pl.BlockSpec((None, 2, None, HEAD_DIM, HEAD_DIM), lambda b, h: (b, 0, h, 0, 0)))
        out_shape.append(jax.ShapeDtypeStruct((batch, 2, H_A, HEAD_DIM, HEAD_DIM), F32))
    outs = pl.pallas_call(
        functools.partial(_hgrn_kernel, layer=layer, seq_len=seq_len, has_s0=has_s0, emit_state=emit_state),
        grid=(batch, H_A),
        in_specs=in_specs,
        out_specs=out_specs,
        out_shape=out_shape,
        scratch_shapes=[
            pltpu.VMEM((seq_len, HEAD_DIM), F32),
            pltpu.VMEM((seq_len, HEAD_DIM), BF16),
            pltpu.VMEM((n_chunks, HEAD_DIM, HEAD_DIM), F32),
            pltpu.VMEM((n_chunks, 1, HEAD_DIM), F32),
            pltpu.VMEM((HEAD_DIM, HEAD_DIM), F32),
            pltpu.VMEM((HEAD_DIM, HEAD_DIM), F32),
            pltpu.VMEM((HGRN_CHUNK, HEAD_DIM), F32),
            pltpu.VMEM((HGRN_CHUNK, HEAD_DIM), F32),
        ],
        compiler_params=pltpu.CompilerParams(vmem_limit_bytes=VMEM_LIMIT),
        name="hgrn_scan",
    )(*args)
    return (outs[0], outs[1]) if emit_state else (outs[0], None)


def _log_sigmoid(x):
    return jnp.minimum(x, 0.0) - jnp.log1p(jnp.exp(-jnp.abs(x)))


def _rope(x, cos, sin_signed):
    lane = lax.broadcasted_iota(jnp.int32, x.shape, 1)
    quarter = HEAD_DIM // 4
    partner = jnp.where(lane % (2 * quarter) < quarter, pltpu.roll(x, HEAD_DIM - quarter, 1), pltpu.roll(x, quarter, 1))
    return x * cos + partner * sin_signed


def _ret_kernel(*refs, layer, seq_len, has_s0, emit_state, use_rope):
    logit_ref, qb_ref, kb_ref, vb_ref, gb_ref, gn_ref = refs[:6]
    refs = refs[6:]
    if use_rope:
        cos_ref, sin_ref, refs = refs[0], refs[1], refs[2:]
    if has_s0:
        s0_ref, refs = refs[0], refs[1:]
    mix_ref, refs = refs[0], refs[1:]
    if emit_state:
        st_ref, refs = refs[0], refs[1:]
    o_scr, qs_scr, ub_scr, sf_scr, sb_scr = refs

    c_len = RET_CHUNK
    n_chunks = seq_len // c_len
    head = pl.program_id(1)
    lg_f = _log_sigmoid(jnp.full((1, HEAD_DIM), logit_ref[layer, 0, head], F32))
    lg_b = _log_sigmoid(jnp.full((1, HEAD_DIM), logit_ref[layer, 1, head], F32))

    t = lax.broadcasted_iota(jnp.int32, (c_len, HEAD_DIM), 0).astype(F32)
    q_dec_f = jnp.exp((t + 1.0) * lg_f)
    q_dec_b = jnp.exp((c_len - t) * lg_b)
    k_dec_f = jnp.exp((c_len - 1.0 - t) * lg_f)
    k_dec_b = jnp.exp(t * lg_b)
    chunk_dec_f = jnp.exp(c_len * lg_f)
    chunk_dec_b = jnp.exp(c_len * lg_b)
    t_idx = lax.broadcasted_iota(jnp.int32, (c_len, c_len), 0)
    s_idx = lax.broadcasted_iota(jnp.int32, (c_len, c_len), 1)
    dist = (t_idx - s_idx).astype(F32)
    decay = jnp.where(t_idx >= s_idx, jnp.exp(jnp.maximum(dist, 0.0) * lg_f), 0.0)
    decay = decay + jnp.where(t_idx <= s_idx, jnp.exp(jnp.maximum(-dist, 0.0) * lg_b), 0.0)

    if has_s0:
        sf_scr[...] = s0_ref[0].T
        sb_scr[...] = s0_ref[1].T
    else:
        sf_scr[...] = jnp.zeros_like(sf_scr)
        sb_scr[...] = jnp.zeros_like(sb_scr)

    def forward_pass(c, carry):
        rows = pl.ds(pl.multiple_of(c * c_len, c_len), c_len)
        q = qb_ref[rows, :]
        k = kb_ref[rows, :] * (HEAD_DIM ** -0.5)
        v = vb_ref[rows, :]
        if use_rope:
            cos, sin = cos_ref[rows, :], sin_ref[rows, :]
            q = _rope(q, cos, sin)
            k = _rope(k, cos, sin)
        v16 = v.astype(BF16)
        a = _dot_nt(q.astype(BF16), k.astype(BF16)) * decay
        o = _dot(a.astype(BF16), v16)
        o = o + _dot_nt((q * q_dec_f).astype(BF16), sf_scr[...].astype(BF16))
        o_scr[rows, :] = o
        k_cat = jnp.concatenate([k * k_dec_f, k * k_dec_b], axis=1)
        u = _state_update_terms(v, k_cat.astype(BF16))
        sf_scr[...] = chunk_dec_f * sf_scr[...] + u[:, :HEAD_DIM]
        ub_scr[c] = u[:, HEAD_DIM:]
        qs_scr[rows, :] = (q * q_dec_b).astype(BF16)
        return carry

    lax.fori_loop(0, n_chunks, forward_pass, 0)

    gn = gn_ref[...]

    def backward_pass(i, carry):
        c = n_chunks - 1 - i
        rows = pl.ds(pl.multiple_of(c * c_len, c_len), c_len)
        o = o_scr[rows, :] + _dot_nt(qs_scr[rows, :], sb_scr[...].astype(BF16))
        sb_scr[...] = chunk_dec_b * sb_scr[...] + ub_scr[c]
        o = o - jnp.mean(o, axis=-1, keepdims=True)
        o = o * lax.rsqrt(jnp.mean(o * o, axis=-1, keepdims=True) + EPS) * gn
        mix_ref[rows, :] = (o * _silu(gb_ref[rows, :])).astype(BF16)
        return carry

    lax.fori_loop(0, n_chunks, backward_pass, 0)

    if emit_state:
        st_ref[0] = sf_scr[...].T
        st_ref[1] = sb_scr[...].T


def _ret(proj, decay_logit, norm_gain, rope_tables, state, layer, emit_state):
    batch, seq_len, _ = proj.shape
    n_chunks = seq_len // RET_CHUNK
    has_s0 = state is not None
    use_rope = rope_tables is not None
    in_specs = [pl.BlockSpec(memory_space=pltpu.SMEM)] + [_proj_spec(seq_len, s) for s in range(5, 9)] + [
        pl.BlockSpec((None, 1, HEAD_DIM), lambda b, h: (layer, 0, h)),
    ]
    args = [decay_logit] + [proj] * 4 + [norm_gain]
    if use_rope:
        in_specs += [pl.BlockSpec((seq_len, HEAD_DIM), lambda b, h: (0, 0))] * 2
        args += list(rope_tables)
    if has_s0:
        in_specs.append(pl.BlockSpec((None, None, 2, None, HEAD_DIM, HEAD_DIM), lambda b, h: (b, layer, 0, h, 0, 0)))
        args.append(state)
    out_specs = [pl.BlockSpec((None, seq_len, HEAD_DIM), lambda b, h: (b, 0, h))]
    out_shape = [jax.ShapeDtypeStruct((batch, seq_len, H_B * HEAD_DIM), BF16)]
    if emit_state:
        out_specs.append(pl.BlockSpec((None, 2, None, HEAD_DIM, HEAD_DIM), lambda b, h: (b, 0, h, 0, 0)))
        out_shape.append(jax.ShapeDtypeStruct((batch, 2, H_B, HEAD_DIM, HEAD_DIM), F32))
    outs = pl.pallas_call(
        functools.partial(
            _ret_kernel, layer=layer, seq_len=seq_len, has_s0=has_s0, emit_state=emit_state, use_rope=use_rope
        ),
        grid=(batch, H_B),
        in_specs=in_specs,
        out_specs=out_specs,
        out_shape=out_shape,
        scratch_shapes=[
            pltpu.VMEM((seq_len, HEAD_DIM), F32),
            pltpu.VMEM((seq_len, HEAD_DIM), BF16),
            pltpu.VMEM((n_chunks, HEAD_DIM, HEAD_DIM), F32),
            pltpu.VMEM((HEAD_DIM, HEAD_DIM), F32),
            pltpu.VMEM((HEAD_DIM, HEAD_DIM), F32),
        ],
        compiler_params=pltpu.CompilerParams(vmem_limit_bytes=VMEM_LIMIT),
        name="ret_scan",
    )(*args)
    return (outs[0], outs[1]) if emit_state else (outs[0], None)


def _rope_tables(seq_len):
    quarter = HEAD_DIM // 4
    inv = ROPE_BASE ** (-jnp.arange(quarter, dtype=F32) / quarter)
    pos = jnp.arange(seq_len, dtype=jnp.int32)
    ang_r = (pos // GRID_W).astype(F32)[:, None] * inv
    ang_c = (pos % GRID_W).astype(F32)[:, None] * inv
    cos = jnp.concatenate([jnp.cos(ang_r), jnp.cos(ang_r), jnp.cos(ang_c), jnp.cos(ang_c)], axis=1)
    sin = jnp.concatenate([-jnp.sin(ang_r), jnp.sin(ang_r), -jnp.sin(ang_c), jnp.sin(ang_c)], axis=1)
    return cos, sin


def _out_ffn_kernel(*refs, d_ff, final):
    ma_ref, mb_ref, x_ref, mod_ref, gain_ref, wo_ref, wu_ref, wd_ref = refs[:8]
    refs = refs[8:]
    if final:
        gfin_ref, refs = refs[0], refs[1:]
    out_ref, act_scr = refs

    mod = mod_ref[...]
    d_a = ma_ref.shape[1]
    y = _dot(ma_ref[...], wo_ref[:d_a, :]) + _dot(mb_ref[...], wo_ref[d_a:, :])
    x1 = x_ref[...] + mod[2:3] * y
    h = _modnorm(x1, gain_ref[...], mod[4:5], mod[3:4]).astype(BF16)
    for c in range(d_ff // FFN_CHUNK):
        lo = c * FFN_CHUNK
        gate = _dot(h, wu_ref[:, lo : lo + FFN_CHUNK])
        up = _dot(h, wu_ref[:, d_ff + lo : d_ff + lo + FFN_CHUNK])
        act_scr[:, lo : lo + FFN_CHUNK] = (_silu(gate) * up).astype(BF16)
    x2 = x1 + mod[5:6] * _dot(act_scr[...], wd_ref[...])
    if final:
        x2 = x2 * lax.rsqrt(jnp.mean(x2 * x2, axis=-1, keepdims=True) + EPS) * gfin_ref[...]
    out_ref[...] = x2


def _out_ffn(mix_a, mix_b, x, mod, gain, w_out, w_up, w_down, final_gain, layer, tile_m, cond_index):
    n_tok, d_model = x.shape
    d_a, d_b = mix_a.shape[1], mix_b.shape[1]
    d_ff = w_down.shape[1]
    final = final_gain is not None
    const = lambda i: (layer, 0, 0)
    in_specs = [
        pl.BlockSpec((tile_m, d_a), lambda i: (i, 0)),
        pl.BlockSpec((tile_m, d_b), lambda i: (i, 0)),
        pl.BlockSpec((tile_m, d_model), lambda i: (i, 0)),
        pl.BlockSpec((None, None, N_MOD, d_model), lambda i: (layer, cond_index(i), 0, 0)),
        pl.BlockSpec((None, 1, d_model), const),
        pl.BlockSpec((None, d_a + d_b, d_model), const, pipeline_mode=pl.Buffered(1)),
        pl.BlockSpec((None, d_model, 2 * d_ff), const, pipeline_mode=pl.Buffered(1)),
        pl.BlockSpec((None, d_ff, d_model), const, pipeline_mode=pl.Buffered(1)),
    ]
    args = [mix_a, mix_b, x, mod, gain, w_out, w_up, w_down]
    if final:
        in_specs.append(pl.BlockSpec((1, d_model), lambda i: (0, 0)))
        args.append(final_gain)
    return pl.pallas_call(
        functools.partial(_out_ffn_kernel, d_ff=d_ff, final=final),
        grid=(n_tok // tile_m,),
        in_specs=in_specs,
        out_specs=pl.BlockSpec((tile_m, d_model), lambda i: (i, 0)),
        out_shape=jax.ShapeDtypeStruct((n_tok, d_model), F32),
        scratch_shapes=[pltpu.VMEM((tile_m, d_ff), BF16)],
        compiler_params=pltpu.CompilerParams(vmem_limit_bytes=VMEM_LIMIT),
        name="out_ffn",
    )(*args)


def kernel(x_prompt, x_sample, state_hgrn, state_ret, c, c_ctx, ada_w, ada_b, norm_mix, norm_ffn, w_in,
           hgrn_lb_logits, hgrn_norm, ret_decay_logit, ret_norm, w_out, w_up, w_down, norm_final):
    batch, seq, d_model = x_prompt.shape
    dec_batch, dec_seq, _ = x_sample.shape
    depth = ada_w.shape[0]
    tile_m = 512
    assert dec_batch < COND_ROWS and dec_seq % tile_m == 0 and (batch * seq) % tile_m == 0
    assert dec_seq % GRID_W == 0 and seq % RET_CHUNK == 0 and dec_seq % RET_CHUNK == 0

    cond = jnp.concatenate(
        [c.astype(F32), c_ctx.astype(F32)[None, :], jnp.zeros((COND_ROWS - dec_batch - 1, d_model), F32)], axis=0
    )
    mod = _modulation(cond, ada_w, ada_b)

    w_in16, w_out16, w_up16, w_down16 = (w.astype(BF16) for w in (w_in, w_out, w_up, w_down))
    gain_mix = norm_mix.reshape(depth, 1, d_model)
    gain_ffn = norm_ffn.reshape(depth, 1, d_model)
    gain_hgrn = hgrn_norm.reshape(depth, 1, -1)
    gain_ret = ret_norm.reshape(depth, 1, -1)
    gain_final = norm_final.reshape(1, d_model)
    rope_tables = _rope_tables(dec_seq)

    tiles_per_seq = dec_seq // tile_m
    groups = [
        [x_prompt.astype(F32).reshape(batch * seq, d_model), batch, seq, lambda i: dec_batch, None, None, None, True],
        [x_sample.astype(F32).reshape(dec_batch * dec_seq, d_model), dec_batch, dec_seq,
         lambda i: i // tiles_per_seq, state_hgrn, state_ret, rope_tables, False],
    ]
    new_h, new_r = [], []
    for layer in range(depth):
        final_gain = gain_final if layer == depth - 1 else None
        for grp in groups:
            x, n_b, n_l, cond_index, s0_h, s0_r, tables, emit = grp
            proj = _in_proj(x, mod, gain_mix, w_in16, layer, tile_m, cond_index).reshape(n_b, n_l, D_IN)
            mix_a, st_a = _hgrn(proj, hgrn_lb_logits, gain_hgrn, s0_h, layer, emit)
            mix_b, st_b = _ret(proj, ret_decay_logit, gain_ret, tables, s0_r, layer, emit)
            grp[0] = _out_ffn(
                mix_a.reshape(n_b * n_l, -1), mix_b.reshape(n_b * n_l, -1), x, mod, gain_ffn,
                w_out16, w_up16, w_down16, final_gain, layer, tile_m, cond_index,
            )
            if emit:
                new_h.append(st_a)
                new_r.append(st_b)
    y_prompt = groups[0][0].reshape(batch, seq, d_model).astype(x_prompt.dtype)
    y_sample = groups[1][0].reshape(dec_batch, dec_seq, d_model).astype(x_sample.dtype)
    new_state_hgrn = jnp.stack(new_h, axis=1).astype(state_hgrn.dtype)
    new_state_ret = jnp.stack(new_r, axis=1).astype(state_ret.dtype)
    return (y_prompt, y_sample, new_state_hgrn, new_state_ret)
```

```python
import functools

import jax
import jax.numpy as jnp
from jax import lax
from jax.experimental import pallas as pl
from jax.experimental.pallas import tpu as pltpu

F32 = jnp.float32
BF16 = jnp.bfloat16

H_A = 4
H_B = 4
HEAD_DIM = 128
N_SECTIONS = 9
SECTION = 512
D_IN = N_SECTIONS * SECTION
GRID_W = 64
ROPE_BASE = 10000.0
EPS = 1e-6
GATE_FLOOR = 1e-12

N_MOD = 6
COND_ROWS = 16

HGRN_CHUNK = 64
HGRN_BLOCK = 256
RET_CHUNK = 128
SAFE_EXP_RANGE = 60.0
SAFE_Q_MAX = 1e8
FFN_CHUNK = 256
MOD_TILE_N = 1536
VMEM_LIMIT = 56 * 1024 * 1024


def _sigmoid(x):
    return 1.0 / (1.0 + jnp.exp(-x))


def _silu(x):
    return x * _sigmoid(x)


def _dot(a, b):
    return jnp.dot(a, b, preferred_element_type=F32)


def _dot_nt(a, b):
    return lax.dot_general(a, b, (((1,), (1,)), ((), ())), preferred_element_type=F32)


def _modnorm(x, gain, scale, shift):
    ms = jnp.mean(x * x, axis=-1, keepdims=True)
    return x * lax.rsqrt(ms + EPS) * gain * (1.0 + scale) + shift


def _mod_kernel(cond_ref, w_ref, b_ref, out_ref):
    s = _silu(cond_ref[...])
    out_ref[...] = jnp.dot(s, w_ref[...], preferred_element_type=F32, precision=lax.Precision.HIGHEST) + b_ref[...]


def _modulation(cond, ada_w, ada_b):
    depth, d_model, n_out = ada_w.shape
    out = pl.pallas_call(
        _mod_kernel,
        grid=(depth, n_out // MOD_TILE_N),
        in_specs=[
            pl.BlockSpec((COND_ROWS, d_model), lambda l, j: (0, 0)),
            pl.BlockSpec((None, d_model, MOD_TILE_N), lambda l, j: (l, 0, j)),
            pl.BlockSpec((None, 1, MOD_TILE_N), lambda l, j: (l, 0, j)),
        ],
        out_specs=pl.BlockSpec((None, COND_ROWS, MOD_TILE_N), lambda l, j: (l, 0, j)),
        out_shape=jax.ShapeDtypeStruct((depth, COND_ROWS, n_out), F32),
        compiler_params=pltpu.CompilerParams(vmem_limit_bytes=VMEM_LIMIT),
        name="adaln_mod",
    )(cond, ada_w, ada_b.reshape(depth, 1, n_out))
    return out.reshape(depth, COND_ROWS, N_MOD, d_model)


def _in_proj_kernel(x_ref, mod_ref, gain_ref, w_ref, out_ref):
    mod = mod_ref[...]
    h = _modnorm(x_ref[...], gain_ref[...], mod[1:2], mod[0:1]).astype(BF16)
    for j in range(N_SECTIONS):
        cols = slice(j * SECTION, (j + 1) * SECTION)
        out_ref[:, cols] = _dot(h, w_ref[:, cols])


def _in_proj(x, mod, gain, w, layer, tile_m, cond_index):
    n_tok, d_model = x.shape
    return pl.pallas_call(
        _in_proj_kernel,
        grid=(n_tok // tile_m,),
        in_specs=[
            pl.BlockSpec((tile_m, d_model), lambda i: (i, 0)),
            pl.BlockSpec((None, None, N_MOD, d_model), lambda i: (layer, cond_index(i), 0, 0)),
            pl.BlockSpec((None, 1, d_model), lambda i: (layer, 0, 0)),
            pl.BlockSpec((None, d_model, D_IN), lambda i: (layer, 0, 0), pipeline_mode=pl.Buffered(1)),
        ],
        out_specs=pl.BlockSpec((tile_m, D_IN), lambda i: (i, 0)),
        out_shape=jax.ShapeDtypeStruct((n_tok, D_IN), F32),
        compiler_params=pltpu.CompilerParams(vmem_limit_bytes=VMEM_LIMIT),
        name="in_proj",
    )(x, mod, gain, w)


def _cumsum_chunks(g, chunk, reverse):
    n = g.shape[0]
    pos = lax.broadcasted_iota(jnp.int32, g.shape, 0) % chunk
    sh = 1
    while sh < chunk:
        if reverse:
            g = g + jnp.where(pos < chunk - sh, pltpu.roll(g, n - sh, 0), 0.0)
        else:
            g = g + jnp.where(pos >= sh, pltpu.roll(g, sh, 0), 0.0)
        sh *= 2
    return g


def _state_increment(v, k_cat):
    return _dot(v.T.astype(BF16), k_cat)


def _run_states(n_chunks, dec_of, u_scr, st_scr, s_f, s_b):
    def step(i, carry):
        s_f, s_b = carry
        cf, cb = i, n_chunks - 1 - i
        st_scr[cf, :, :HEAD_DIM] = s_f.astype(BF16)
        st_scr[cb, :, HEAD_DIM:] = s_b.astype(BF16)
        s_f = dec_of(cf)[:, :HEAD_DIM] * s_f + u_scr[cf, :, :HEAD_DIM]
        s_b = dec_of(cb)[:, HEAD_DIM:] * s_b + u_scr[cb, :, HEAD_DIM:]
        return s_f, s_b

    return lax.fori_loop(0, n_chunks, step, (s_f, s_b), unroll=2)


def _initial_states(s0_ref):
    if s0_ref is None:
        zero = jnp.zeros((HEAD_DIM, HEAD_DIM), F32)
        return zero, zero
    return s0_ref[0].T, s0_ref[1].T


def _hgrn_gates(z, lb, reverse):
    s = _sigmoid(z)
    f = lb + (1.0 - lb) * s
    g = jnp.log(jnp.maximum(f, GATE_FLOOR))
    k = (1.0 - lb) * (1.0 - s)
    return k, _cumsum_chunks(g, HGRN_CHUNK, reverse)


def _causal_mask(n, lower):
    t_idx = lax.broadcasted_iota(jnp.int32, (n, n), 0)
    s_idx = lax.broadcasted_iota(jnp.int32, (n, n), 1)
    return t_idx >= s_idx if lower else t_idx <= s_idx


def _hgrn_direct_scores(q, k, b, k_scr, b_scr):
    n = q.shape[0]
    k_scr[...] = k
    b_scr[...] = b
    col = lax.broadcasted_iota(jnp.int32, (n, n), 1)

    def body(s, acc):
        z = q * k_scr[pl.ds(s, 1), :] * jnp.exp(jnp.minimum(b - b_scr[pl.ds(s, 1), :], 0.0))
        return jnp.where(col == s, jnp.sum(z, axis=1, keepdims=True), acc)

    return lax.fori_loop(0, n, body, jnp.zeros((n, n), F32))


def _hgrn_kernel(*refs, layer, seq_len, has_s0, emit_state):
    qa_ref, zf_ref, zb_ref, ia_ref, ga_ref, lbl_ref, gn_ref = refs[:7]
    refs = refs[7:]
    s0_ref = None
    if has_s0:
        s0_ref, refs = refs[0], refs[1:]
    mix_ref, refs = refs[0], refs[1:]
    if emit_state:
        st_out_ref, refs = refs[0], refs[1:]
    qt_scr, kt_scr, qe_scr, v16_scr, u_scr, dec_scr, st_scr, safe_smem, k_scr, b_scr = refs

    c_len = HGRN_CHUNK
    blk_len = min(HGRN_BLOCK, seq_len)
    n_chunks = seq_len // c_len
    chunks_per_blk = blk_len // c_len

    logits = lbl_ref[...]
    e = jnp.exp(logits - jnp.max(logits, axis=0, keepdims=True))
    p = e / jnp.sum(e, axis=0, keepdims=True)
    cum = p[0]
    for i in range(1, layer + 1):
        cum = cum + p[i]
    lb = jnp.clip(cum - p[0], 0.0, 1.0)
    directions = ((zf_ref, lb[0:1], False), (zb_ref, lb[1:2], True))

    def prepare(blk, carry):
        r0 = pl.multiple_of(blk * blk_len, blk_len)
        rows = pl.ds(r0, blk_len)
        q = _silu(qa_ref[rows, :])
        v = ia_ref[rows, :]
        v16_scr[rows, :] = v.astype(BF16)
        q_ok = jnp.max(jnp.abs(q)) <= SAFE_Q_MAX
        gates = [_hgrn_gates(z_ref[rows, :], lb_d, rev) for z_ref, lb_d, rev in directions]
        for j in range(chunks_per_blk):
            sl = slice(j * c_len, (j + 1) * c_len)
            rows_j = pl.ds(r0 + j * c_len, c_len)
            c = blk * chunks_per_blk + j
            q_e, k_e, dec, span = [], [], [], None
            for d, (k, b) in enumerate(gates):
                k_j, b_j = k[sl], b[sl]
                tot = b_j[0:1] if directions[d][2] else b_j[c_len - 1 : c_len]
                dist = b_j - b_j[c_len // 2 - 1 : c_len // 2]
                qt_scr[d, rows_j, :] = (q[sl] * jnp.exp(dist)).astype(BF16)
                kt_scr[d, rows_j, :] = (k_j * jnp.exp(-dist)).astype(BF16)
                q_e.append(q[sl] * jnp.exp(b_j))
                k_e.append(k_j * jnp.exp(tot - b_j))
                dec.append(jnp.exp(tot))
                ends = jnp.maximum(jnp.abs(dist[0:1]), jnp.abs(dist[c_len - 1 : c_len]))
                span = ends if span is None else jnp.maximum(span, ends)
            qe_scr[rows_j, :] = jnp.concatenate(q_e, axis=1).astype(BF16)
            u_scr[c] = _state_increment(v[sl], jnp.concatenate(k_e, axis=1).astype(BF16))
            dec_scr[c] = jnp.concatenate(dec, axis=1)
            safe_smem[c] = jnp.logical_and(q_ok, jnp.max(span) <= SAFE_EXP_RANGE).astype(jnp.int32)
        return carry

    lax.fori_loop(0, seq_len // blk_len, prepare, 0)

    s_f, s_b = _run_states(n_chunks, lambda c: dec_scr[c], u_scr, st_scr, *_initial_states(s0_ref))
    if emit_state:
        st_out_ref[0] = s_f.T
        st_out_ref[1] = s_b.T

    gn = gn_ref[...]
    lower, upper = _causal_mask(c_len, True), _causal_mask(c_len, False)

    def finish(c, rows, scores):
        o = _dot(scores.astype(BF16), v16_scr[rows, :]) + _dot_nt(qe_scr[rows, :], st_scr[c])
        o = o * lax.rsqrt(jnp.mean(o * o, axis=-1, keepdims=True) + EPS) * gn
        mix_ref[rows, :] = (o * _silu(ga_ref[rows, :])).astype(BF16)

    def outputs(c, carry):
        rows = pl.ds(pl.multiple_of(c * c_len, c_len), c_len)
        a_f = _dot_nt(qt_scr[0, rows, :], kt_scr[0, rows, :])
        a_b = _dot_nt(qt_scr[1, rows, :], kt_scr[1, rows, :])
        finish(c, rows, jnp.where(lower, a_f, 0.0) + jnp.where(upper, a_b, 0.0))
        return carry

    lax.fori_loop(0, n_chunks, outputs, 0, unroll=4)

    def redo(c, carry):
        @pl.when(safe_smem[c] == 0)
        def _():
            rows = pl.ds(pl.multiple_of(c * c_len, c_len), c_len)
            q = _silu(qa_ref[rows, :])
            scores = jnp.zeros((c_len, c_len), F32)
            for z_ref, lb_d, rev in directions:
                k, b = _hgrn_gates(z_ref[rows, :], lb_d, rev)
                a = _hgrn_direct_scores(q, k, b, k_scr, b_scr)
                scores = scores + jnp.where(upper if rev else lower, a, 0.0)
            finish(c, rows, scores)

        return carry

    lax.fori_loop(0, n_chunks, redo, 0)


def _proj_spec(seq_len, section):
    return pl.BlockSpec((None, seq_len, HEAD_DIM), lambda b, h: (b, 0, section * 4 + h))


def _state_in_spec(layer):
    return pl.BlockSpec((None, None, 2, None, HEAD_DIM, HEAD_DIM), lambda b, h: (b, layer, 0, h, 0, 0))


def _state_out_spec():
    return pl.BlockSpec((None, 2, None, HEAD_DIM, HEAD_DIM), lambda b, h: (b, 0, h, 0, 0))


def _hgrn(proj, lb_logits, norm_gain, state, layer, emit_state):
    batch, seq_len, _ = proj.shape
    depth = lb_logits.shape[0]
    n_chunks = seq_len // HGRN_CHUNK
    has_s0 = state is not None
    in_specs = [_proj_spec(seq_len, s) for s in range(5)] + [
        pl.BlockSpec((depth, 2, HEAD_DIM), lambda b, h: (0, 0, h)),
        pl.BlockSpec((None, 1, HEAD_DIM), lambda b, h: (layer, 0, h)),
    ]
    args = [proj] * 5 + [lb_logits, norm_gain]
    if has_s0:
        in_specs.append(_state_in_spec(layer))
        args.append(state)
    out_specs = [pl.BlockSpec((None, seq_len, HEAD_DIM), lambda b, h: (b, 0, h))]
    out_shape = [jax.ShapeDtypeStruct((batch, seq_len, H_A * HEAD_DIM), BF16)]
    if emit_state:
        out_specs.append(_state_out_spec())
        out_shape.append(jax.ShapeDtypeStruct((batch, 2, H_A, HEAD_DIM, HEAD_DIM), F32))
    outs = pl.pallas_call(
        functools.partial(_hgrn_kernel, layer=layer, seq_len=seq_len, has_s0=has_s0, emit_state=emit_state),
        grid=(batch, H_A),
        in_specs=in_specs,
        out_specs=out_specs,
        out_shape=out_shape,
        scratch_shapes=[
            pltpu.VMEM((2, seq_len, HEAD_DIM), BF16),
            pltpu.VMEM((2, seq_len, HEAD_DIM), BF16),
            pltpu.VMEM((seq_len, 2 * HEAD_DIM), BF16),
            pltpu.VMEM((seq_len, HEAD_DIM), BF16),
            pltpu.VMEM((n_chunks, HEAD_DIM, 2 * HEAD_DIM), F32),
            pltpu.VMEM((n_chunks, 1, 2 * HEAD_DIM), F32),
            pltpu.VMEM((n_chunks, HEAD_DIM, 2 * HEAD_DIM), BF16),
            pltpu.SMEM((n_chunks,), jnp.int32),
            pltpu.VMEM((HGRN_CHUNK, HEAD_DIM), F32),
            pltpu.VMEM((HGRN_CHUNK, HEAD_DIM), F32),
        ],
        compiler_params=pltpu.CompilerParams(vmem_limit_bytes=VMEM_LIMIT),
        name="hgrn_scan",
    )(*args)
    return (outs[0], outs[1]) if emit_state else (outs[0], None)


def _log_sigmoid(x):
    return jnp.minimum(x, 0.0) - jnp.log1p(jnp.exp(-jnp.abs(x)))


def _rope(x, cos, sin_signed):
    lane = lax.broadcasted_iota(jnp.int32, x.shape, 1)
    quarter = HEAD_DIM // 4
    partner = jnp.where(lane % (2 * quarter) < quarter, pltpu.roll(x, HEAD_DIM - quarter, 1), pltpu.roll(x, quarter, 1))
    return x * cos + partner * sin_signed


def _ret_kernel(*refs, layer, seq_len, has_s0, emit_state, use_rope):
    logit_ref, qb_ref, kb_ref, vb_ref, gb_ref, gn_ref = refs[:6]
    refs = refs[6:]
    if use_rope:
        cos_ref, sin_ref, refs = refs[0], refs[1], refs[2:]
    s0_ref = None
    if has_s0:
        s0_ref, refs = refs[0], refs[1:]
    mix_ref, refs = refs[0], refs[1:]
    if emit_state:
        st_out_ref, refs = refs[0], refs[1:]
    q16_scr, k16_scr, v16_scr, qe_scr, u_scr, st_scr = refs

    c_len = RET_CHUNK
    n_chunks = seq_len // c_len
    head = pl.program_id(1)
    lg_f = _log_sigmoid(jnp.full((1, HEAD_DIM), logit_ref[layer, 0, head], F32))
    lg_b = _log_sigmoid(jnp.full((1, HEAD_DIM), logit_ref[layer, 1, head], F32))

    t = lax.broadcasted_iota(jnp.int32, (c_len, HEAD_DIM), 0).astype(F32)
    q_dec = jnp.concatenate([jnp.exp((t + 1.0) * lg_f), jnp.exp((c_len - t) * lg_b)], axis=1)
    k_dec = jnp.concatenate([jnp.exp((c_len - 1.0 - t) * lg_f), jnp.exp(t * lg_b)], axis=1)
    chunk_dec = jnp.concatenate([jnp.exp(c_len * lg_f), jnp.exp(c_len * lg_b)], axis=1)
    t_idx = lax.broadcasted_iota(jnp.int32, (c_len, c_len), 0)
    s_idx = lax.broadcasted_iota(jnp.int32, (c_len, c_len), 1)
    dist = (t_idx - s_idx).astype(F32)
    decay = jnp.where(t_idx >= s_idx, jnp.exp(jnp.maximum(dist, 0.0) * lg_f), 0.0)
    decay = decay + jnp.where(t_idx <= s_idx, jnp.exp(jnp.maximum(-dist, 0.0) * lg_b), 0.0)

    def prepare(c, carry):
        rows = pl.ds(pl.multiple_of(c * c_len, c_len), c_len)
        q = qb_ref[rows, :]
        k = kb_ref[rows, :] * (HEAD_DIM ** -0.5)
        v = vb_ref[rows, :]
        if use_rope:
            cos, sin = cos_ref[rows, :], sin_ref[rows, :]
            q = _rope(q, cos, sin)
            k = _rope(k, cos, sin)
        q16_scr[rows, :] = q.astype(BF16)
        k16_scr[rows, :] = k.astype(BF16)
        v16_scr[rows, :] = v.astype(BF16)
        qe_scr[rows, :] = (jnp.concatenate([q, q], axis=1) * q_dec).astype(BF16)
        u_scr[c] = _state_increment(v, (jnp.concatenate([k, k], axis=1) * k_dec).astype(BF16))
        return carry

    lax.fori_loop(0, n_chunks, prepare, 0, unroll=2)

    s_f, s_b = _run_states(n_chunks, lambda c: chunk_dec, u_scr, st_scr, *_initial_states(s0_ref))
    if emit_state:
        st_out_ref[0] = s_f.T
        st_out_ref[1] = s_b.T

    gn = gn_ref[...]

    def outputs(c, carry):
        rows = pl.ds(pl.multiple_of(c * c_len, c_len), c_len)
        a = _dot_nt(q16_scr[rows, :], k16_scr[rows, :]) * decay
        o = _dot(a.astype(BF16), v16_scr[rows, :]) + _dot_nt(qe_scr[rows, :], st_scr[c])
        o = o - jnp.mean(o, axis=-1, keepdims=True)
        o = o * lax.rsqrt(jnp.mean(o * o, axis=-1, keepdims=True) + EPS) * gn
        mix_ref[rows, :] = (o * _silu(gb_ref[rows, :])).astype(BF16)
        return carry

    lax.fori_loop(0, n_chunks, outputs, 0, unroll=min(4, n_chunks))


def _ret(proj, decay_logit, norm_gain, rope_tables, state, layer, emit_state):
    batch, seq_len, _ = proj.shape
    n_chunks = seq_len // RET_CHUNK
    has_s0 = state is not None
    use_rope = rope_tables is not None
    in_specs = [pl.BlockSpec(memory_space=pltpu.SMEM)] + [_proj_spec(seq_len, s) for s in range(5, 9)] + [
        pl.BlockSpec((None, 1, HEAD_DIM), lambda b, h: (layer, 0, h)),
    ]
    args = [decay_logit] + [proj] * 4 + [norm_gain]
    if use_rope:
        in_specs += [pl.BlockSpec((seq_len, HEAD_DIM), lambda b, h: (0, 0))] * 2
        args += list(rope_tables)
    if has_s0:
        in_specs.append(_state_in_spec(layer))
        args.append(state)
    out_specs = [pl.BlockSpec((None, seq_len, HEAD_DIM), lambda b, h: (b, 0, h))]
    out_shape = [jax.ShapeDtypeStruct((batch, seq_len, H_B * HEAD_DIM), BF16)]
    if emit_state:
        out_specs.append(_state_out_spec())
        out_shape.append(jax.ShapeDtypeStruct((batch, 2, H_B, HEAD_DIM, HEAD_DIM), F32))
    outs = pl.pallas_call(
        functools.partial(
            _ret_kernel, layer=layer, seq_len=seq_len, has_s0=has_s0, emit_state=emit_state, use_rope=use_rope
        ),
        grid=(batch, H_B),
        in_specs=in_specs,
        out_specs=out_specs,
        out_shape=out_shape,
        scratch_shapes=[
            pltpu.VMEM((seq_len, HEAD_DIM), BF16),
            pltpu.VMEM((seq_len, HEAD_DIM), BF16),
            pltpu.VMEM((seq_len, HEAD_DIM), BF16),
            pltpu.VMEM((seq_len, 2 * HEAD_DIM), BF16),
            pltpu.VMEM((n_chunks, HEAD_DIM, 2 * HEAD_DIM), F32),
            pltpu.VMEM((n_chunks, HEAD_DIM, 2 * HEAD_DIM), BF16),
        ],
        compiler_params=pltpu.CompilerParams(vmem_limit_bytes=VMEM_LIMIT),
        name="ret_scan",
    )(*args)
    return (outs[0], outs[1]) if emit_state else (outs[0], None)


def _rope_tables(seq_len):
    quarter = HEAD_DIM // 4
    inv = ROPE_BASE ** (-jnp.arange(quarter, dtype=F32) / quarter)
    pos = jnp.arange(seq_len, dtype=jnp.int32)
    ang_r = (pos // GRID_W).astype(F32)[:, None] * inv
    ang_c = (pos % GRID_W).astype(F32)[:, None] * inv
    cos = jnp.concatenate([jnp.cos(ang_r), jnp.cos(ang_r), jnp.cos(ang_c), jnp.cos(ang_c)], axis=1)
    sin = jnp.concatenate([-jnp.sin(ang_r), jnp.sin(ang_r), -jnp.sin(ang_c), jnp.sin(ang_c)], axis=1)
    return cos, sin


def _out_ffn_kernel(*refs, d_ff, final):
    ma_ref, mb_ref, x_ref, mod_ref, gain_ref, wo_ref, wu_ref, wd_ref = refs[:8]
    refs = refs[8:]
    if final:
        gfin_ref, refs = refs[0], refs[1:]
    out_ref, act_scr = refs

    mod = mod_ref[...]
    d_a = ma_ref.shape[1]
    y = _dot(ma_ref[...], wo_ref[:d_a, :]) + _dot(mb_ref[...], wo_ref[d_a:, :])
    x1 = x_ref[...] + mod[2:3] * y
    h = _modnorm(x1, gain_ref[...], mod[4:5], mod[3:4]).astype(BF16)
    for c in range(d_ff // FFN_CHUNK):
        lo = c * FFN_CHUNK
        gate = _dot(h, wu_ref[:, lo : lo + FFN_CHUNK])
        up = _dot(h, wu_ref[:, d_ff + lo : d_ff + lo + FFN_CHUNK])
        act_scr[:, lo : lo + FFN_CHUNK] = (_silu(gate) * up).astype(BF16)
    x2 = x1 + mod[5:6] * _dot(act_scr[...], wd_ref[...])
    if final:
        x2 = x2 * lax.rsqrt(jnp.mean(x2 * x2, axis=-1, keepdims=True) + EPS) * gfin_ref[...]
    out_ref[...] = x2


def _out_ffn(mix_a, mix_b, x, mod, gain, w_out, w_up, w_down, final_gain, layer, tile_m, cond_index):
    n_tok, d_model = x.shape
    d_a, d_b = mix_a.shape[1], mix_b.shape[1]
    d_ff = w_down.shape[1]
    final = final_gain is not None
    const = lambda i: (layer, 0, 0)
    in_specs = [
        pl.BlockSpec((tile_m, d_a), lambda i: (i, 0)),
        pl.BlockSpec((tile_m, d_b), lambda i: (i, 0)),
        pl.BlockSpec((tile_m, d_model), lambda i: (i, 0)),
        pl.BlockSpec((None, None, N_MOD, d_model), lambda i: (layer, cond_index(i), 0, 0)),
        pl.BlockSpec((None, 1, d_model), const),
        pl.BlockSpec((None, d_a + d_b, d_model), const, pipeline_mode=pl.Buffered(1)),
        pl.BlockSpec((None, d_model, 2 * d_ff), const, pipeline_mode=pl.Buffered(1)),
        pl.BlockSpec((None, d_ff, d_model), const, pipeline_mode=pl.Buffered(1)),
    ]
    args = [mix_a, mix_b, x, mod, gain, w_out, w_up, w_down]
    if final:
        in_specs.append(pl.BlockSpec((1, d_model), lambda i: (0, 0)))
        args.append(final_gain)
    return pl.pallas_call(
        functools.partial(_out_ffn_kernel, d_ff=d_ff, final=final),
        grid=(n_tok // tile_m,),
        in_specs=in_specs,
        out_specs=pl.BlockSpec((tile_m, d_model), lambda i: (i, 0)),
        out_shape=jax.ShapeDtypeStruct((n_tok, d_model), F32),
        scratch_shapes=[pltpu.VMEM((tile_m, d_ff), BF16)],
        compiler_params=pltpu.CompilerParams(vmem_limit_bytes=VMEM_LIMIT),
        name="out_ffn",
    )(*args)


def kernel(x_prompt, x_sample, state_hgrn, state_ret, c, c_ctx, ada_w, ada_b, norm_mix, norm_ffn, w_in,
           hgrn_lb_logits, hgrn_norm, ret_decay_logit, ret_norm, w_out, w_up, w_down, norm_final):
    batch, seq, d_model = x_prompt.shape
    dec_batch, dec_seq, _ = x_sample.shape
    depth = ada_w.shape[0]
    tile_m = 512
    assert dec_batch < COND_ROWS and dec_seq % tile_m == 0 and (batch * seq) % tile_m == 0
    assert dec_seq % GRID_W == 0 and seq % RET_CHUNK == 0 and dec_seq % max(RET_CHUNK, HGRN_BLOCK) == 0

    cond = jnp.concatenate(
        [c.astype(F32), c_ctx.astype(F32)[None, :], jnp.zeros((COND_ROWS - dec_batch - 1, d_model), F32)], axis=0
    )
    mod = _modulation(cond, ada_w, ada_b)

    w_in16, w_out16, w_up16, w_down16 = (w.astype(BF16) for w in (w_in, w_out, w_up, w_down))
    gain_mix = norm_mix.reshape(depth, 1, d_model)
    gain_ffn = norm_ffn.reshape(depth, 1, d_model)
    gain_hgrn = hgrn_norm.reshape(depth, 1, -1)
    gain_ret = ret_norm.reshape(depth, 1, -1)
    gain_final = norm_final.reshape(1, d_model)
    rope_tables = _rope_tables(dec_seq)

    tiles_per_seq = dec_seq // tile_m
    groups = [
        [x_prompt.astype(F32).reshape(batch * seq, d_model), batch, seq, lambda i: dec_batch, None, None, None, True],
        [x_sample.astype(F32).reshape(dec_batch * dec_seq, d_model), dec_batch, dec_seq,
         lambda i: i // tiles_per_seq, state_hgrn, state_ret, rope_tables, False],
    ]
    new_h, new_r = [], []
    for layer in range(depth):
        final_gain = gain_final if layer == depth - 1 else None
        for grp in groups:
            x, n_b, n_l, cond_index, s0_h, s0_r, tables, emit = grp
            proj = _in_proj(x, mod, gain_mix, w_in16, layer, tile_m, cond_index).reshape(n_b, n_l, D_IN)
            mix_a, st_a = _hgrn(proj, hgrn_lb_logits, gain_hgrn, s0_h, layer, emit)
            mix_b, st_b = _ret(proj, ret_decay_logit, gain_ret, tables, s0_r, layer, emit)
            grp[0] = _out_ffn(
                mix_a.reshape(n_b * n_l, -1), mix_b.reshape(n_b * n_l, -1), x, mod, gain_ffn,
                w_out16, w_up16, w_down16, final_gain, layer, tile_m, cond_index,
            )
            if emit:
                new_h.append(st_a)
                new_r.append(st_b)
    y_prompt = groups[0][0].reshape(batch, seq, d_model).astype(x_prompt.dtype)
    y_sample = groups[1][0].reshape(dec_batch, dec_seq, d_model).astype(x_sample.dtype)
    new_state_hgrn = jnp.stack(new_h, axis=1).astype(state_hgrn.dtype)
    new_state_ret = jnp.stack(new_r, axis=1).astype(state_ret.dtype)
    return (y_prompt, y_sample, new_state_hgrn, new_state_ret)
```

```python
import functools

import jax
import jax.numpy as jnp
from jax import lax
from jax.experimental import pallas as pl
from jax.experimental.pallas import tpu as pltpu

F32 = jnp.float32
BF16 = jnp.bfloat16

H_A = 4
H_B = 4
HEAD_DIM = 128
N_SECTIONS = 9
SECTION = 512
D_IN = N_SECTIONS * SECTION
GRID_W = 64
ROPE_BASE = 10000.0
EPS = 1e-6
GATE_FLOOR = 1e-12

N_MOD = 6
COND_ROWS = 16

HGRN_CHUNK = 64
HGRN_BLOCK = 256
RET_CHUNK = 128
RET_BLOCK = 256
SAFE_EXP_RANGE = 60.0
SAFE_Q_MAX = 1e8
FFN_CHUNK = 256
MOD_TILE_N = 1536
VMEM_LIMIT = 56 * 1024 * 1024


def _sigmoid(x):
    return 1.0 / (1.0 + jnp.exp(-x))


def _silu(x):
    return x * _sigmoid(x)


def _dot(a, b):
    return jnp.dot(a, b, preferred_element_type=F32)


def _dot_nt(a, b):
    return lax.dot_general(a, b, (((1,), (1,)), ((), ())), preferred_element_type=F32)


def _modnorm(x, gain, scale, shift):
    ms = jnp.mean(x * x, axis=-1, keepdims=True)
    return x * lax.rsqrt(ms + EPS) * gain * (1.0 + scale) + shift


def _split_bf16(x):
    hi = x.astype(BF16)
    return hi, (x - hi.astype(F32)).astype(BF16)


def _mod_kernel(cond_ref, w_ref, b_ref, out_ref):
    s_hi, s_lo = _split_bf16(_silu(cond_ref[...]))
    w_hi, w_lo = _split_bf16(w_ref[...])
    s_both = jnp.concatenate([s_hi, s_lo], axis=0)
    p = _dot(s_both, w_hi) + _dot(s_both, w_lo)
    out_ref[...] = p[:COND_ROWS] + p[COND_ROWS:] + b_ref[...]


def _modulation(cond, ada_w, ada_b):
    depth, d_model, n_out = ada_w.shape
    out = pl.pallas_call(
        _mod_kernel,
        grid=(depth, n_out // MOD_TILE_N),
        in_specs=[
            pl.BlockSpec((COND_ROWS, d_model), lambda l, j: (0, 0)),
            pl.BlockSpec((None, d_model, MOD_TILE_N), lambda l, j: (l, 0, j)),
            pl.BlockSpec((None, 1, MOD_TILE_N), lambda l, j: (l, 0, j)),
        ],
        out_specs=pl.BlockSpec((None, COND_ROWS, MOD_TILE_N), lambda l, j: (l, 0, j)),
        out_shape=jax.ShapeDtypeStruct((depth, COND_ROWS, n_out), F32),
        compiler_params=pltpu.CompilerParams(vmem_limit_bytes=VMEM_LIMIT),
        name="adaln_mod",
    )(cond, ada_w, ada_b.reshape(depth, 1, n_out))
    return out.reshape(depth, COND_ROWS, N_MOD, d_model)


def _in_proj_kernel(x_ref, mod_ref, gain_ref, w_ref, out_ref):
    mod = mod_ref[...]
    h = _modnorm(x_ref[...], gain_ref[...], mod[1:2], mod[0:1]).astype(BF16)
    for j in range(N_SECTIONS):
        cols = slice(j * SECTION, (j + 1) * SECTION)
        out_ref[:, cols] = _dot(h, w_ref[:, cols])


def _in_proj(x, mod, gain, w, layer, tile_m, cond_index):
    n_tok, d_model = x.shape
    return pl.pallas_call(
        _in_proj_kernel,
        grid=(n_tok // tile_m,),
        in_specs=[
            pl.BlockSpec((tile_m, d_model), lambda i: (i, 0)),
            pl.BlockSpec((None, None, N_MOD, d_model), lambda i: (layer, cond_index(i), 0, 0)),
            pl.BlockSpec((None, 1, d_model), lambda i: (layer, 0, 0)),
            pl.BlockSpec((None, d_model, D_IN), lambda i: (layer, 0, 0), pipeline_mode=pl.Buffered(1)),
        ],
        out_specs=pl.BlockSpec((tile_m, D_IN), lambda i: (i, 0)),
        out_shape=jax.ShapeDtypeStruct((n_tok, D_IN), F32),
        compiler_params=pltpu.CompilerParams(vmem_limit_bytes=VMEM_LIMIT),
        name="in_proj",
    )(x, mod, gain, w)


def _cumsum_chunks(g, chunk, reverse):
    n = g.shape[0]
    pos = lax.broadcasted_iota(jnp.int32, g.shape, 0) % chunk
    sh = 1
    while sh < chunk:
        if reverse:
            g = g + jnp.where(pos < chunk - sh, pltpu.roll(g, n - sh, 0), 0.0)
        else:
            g = g + jnp.where(pos >= sh, pltpu.roll(g, sh, 0), 0.0)
        sh *= 2
    return g


def _state_increment(v, k_cat):
    return _dot(v.T.astype(BF16), k_cat)


def _run_states(n_chunks, dec_of, u_scr, st_scr, s_f, s_b):
    def step(i, carry):
        s_f, s_b = carry
        cf, cb = i, n_chunks - 1 - i
        st_scr[cf, :, :HEAD_DIM] = s_f.astype(BF16)
        st_scr[cb, :, HEAD_DIM:] = s_b.astype(BF16)
        s_f = dec_of(cf)[:, :HEAD_DIM] * s_f + u_scr[cf, :, :HEAD_DIM]
        s_b = dec_of(cb)[:, HEAD_DIM:] * s_b + u_scr[cb, :, HEAD_DIM:]
        return s_f, s_b

    return lax.fori_loop(0, n_chunks, step, (s_f, s_b), unroll=2)


def _initial_states(s0_ref):
    if s0_ref is None:
        zero = jnp.zeros((HEAD_DIM, HEAD_DIM), F32)
        return zero, zero
    return s0_ref[0].T, s0_ref[1].T


def _hgrn_gates(z, lb, reverse):
    s = _sigmoid(z)
    f = lb + (1.0 - lb) * s
    g = jnp.log(jnp.maximum(f, GATE_FLOOR))
    k = (1.0 - lb) * (1.0 - s)
    return k, _cumsum_chunks(g, HGRN_CHUNK, reverse)


def _causal_mask(n, lower):
    t_idx = lax.broadcasted_iota(jnp.int32, (n, n), 0)
    s_idx = lax.broadcasted_iota(jnp.int32, (n, n), 1)
    return t_idx >= s_idx if lower else t_idx <= s_idx


def _hgrn_direct_scores(q, k, b, k_scr, b_scr):
    n = q.shape[0]
    k_scr[...] = k
    b_scr[...] = b
    col = lax.broadcasted_iota(jnp.int32, (n, n), 1)

    def body(s, acc):
        z = q * k_scr[pl.ds(s, 1), :] * jnp.exp(jnp.minimum(b - b_scr[pl.ds(s, 1), :], 0.0))
        return jnp.where(col == s, jnp.sum(z, axis=1, keepdims=True), acc)

    return lax.fori_loop(0, n, body, jnp.zeros((n, n), F32))


def _hgrn_kernel(*refs, layer, seq_len, has_s0, emit_state):
    qa_ref, zf_ref, zb_ref, ia_ref, ga_ref, lbl_ref, gn_ref = refs[:7]
    refs = refs[7:]
    s0_ref = None
    if has_s0:
        s0_ref, refs = refs[0], refs[1:]
    mix_ref, refs = refs[0], refs[1:]
    if emit_state:
        st_out_ref, refs = refs[0], refs[1:]
    a16_scr, qe_scr, v16_scr, o_scr, u_scr, dec_scr, st_scr, safe_smem, k_scr, b_scr = refs

    c_len = HGRN_CHUNK
    blk_len = min(HGRN_BLOCK, seq_len)
    n_chunks = seq_len // c_len
    chunks_per_blk = blk_len // c_len

    logits = lbl_ref[...]
    e = jnp.exp(logits - jnp.max(logits, axis=0, keepdims=True))
    p = e / jnp.sum(e, axis=0, keepdims=True)
    cum = p[0]
    for i in range(1, layer + 1):
        cum = cum + p[i]
    lb = jnp.clip(cum - p[0], 0.0, 1.0)
    directions = ((zf_ref, lb[0:1], False), (zb_ref, lb[1:2], True))
    causal = (_causal_mask(c_len, True), _causal_mask(c_len, False))

    def prepare(blk, carry):
        r0 = pl.multiple_of(blk * blk_len, blk_len)
        rows = pl.ds(r0, blk_len)
        q = _silu(qa_ref[rows, :])
        v = ia_ref[rows, :]
        v16_scr[rows, :] = v.astype(BF16)
        q_ok = jnp.max(jnp.abs(q)) <= SAFE_Q_MAX
        gates = [_hgrn_gates(z_ref[rows, :], lb_d, rev) for z_ref, lb_d, rev in directions]
        for j in range(chunks_per_blk):
            sl = slice(j * c_len, (j + 1) * c_len)
            rows_j = pl.ds(r0 + j * c_len, c_len)
            c = blk * chunks_per_blk + j
            q_e, k_e, dec, span, scores = [], [], [], None, None
            for d, (k, b) in enumerate(gates):
                k_j, b_j = k[sl], b[sl]
                tot = b_j[0:1] if directions[d][2] else b_j[c_len - 1 : c_len]
                dist = b_j - b_j[c_len // 2 - 1 : c_len // 2]
                a = _dot_nt((q[sl] * jnp.exp(dist)).astype(BF16), (k_j * jnp.exp(-dist)).astype(BF16))
                a = jnp.where(causal[d], a, 0.0)
                scores = a if scores is None else scores + a
                q_e.append(q[sl] * jnp.exp(b_j))
                k_e.append(k_j * jnp.exp(tot - b_j))
                dec.append(jnp.exp(tot))
                ends = jnp.maximum(jnp.abs(dist[0:1]), jnp.abs(dist[c_len - 1 : c_len]))
                span = ends if span is None else jnp.maximum(span, ends)
            a16_scr[rows_j, :] = scores.astype(BF16)
            qe_scr[rows_j, :] = jnp.concatenate(q_e, axis=1).astype(BF16)
            u_scr[c] = _state_increment(v[sl], jnp.concatenate(k_e, axis=1).astype(BF16))
            dec_scr[c] = jnp.concatenate(dec, axis=1)
            safe_smem[c] = jnp.logical_and(q_ok, jnp.max(span) <= SAFE_EXP_RANGE).astype(jnp.int32)
        return carry

    lax.fori_loop(0, seq_len // blk_len, prepare, 0)

    s_f, s_b = _run_states(n_chunks, lambda c: dec_scr[c], u_scr, st_scr, *_initial_states(s0_ref))
    if emit_state:
        st_out_ref[0] = s_f.T
        st_out_ref[1] = s_b.T

    def redo_scores(c, carry):
        @pl.when(safe_smem[c] == 0)
        def _():
            rows = pl.ds(pl.multiple_of(c * c_len, c_len), c_len)
            q = _silu(qa_ref[rows, :])
            scores = jnp.zeros((c_len, c_len), F32)
            for d, (z_ref, lb_d, rev) in enumerate(directions):
                k, b = _hgrn_gates(z_ref[rows, :], lb_d, rev)
                scores = scores + jnp.where(causal[d], _hgrn_direct_scores(q, k, b, k_scr, b_scr), 0.0)
            a16_scr[rows, :] = scores.astype(BF16)

        return carry

    lax.fori_loop(0, n_chunks, redo_scores, 0)

    def combine(c, carry):
        rows = pl.ds(pl.multiple_of(c * c_len, c_len), c_len)
        o_scr[rows, :] = _dot(a16_scr[rows, :], v16_scr[rows, :]) + _dot_nt(qe_scr[rows, :], st_scr[c])
        return carry

    lax.fori_loop(0, n_chunks, combine, 0, unroll=min(8, n_chunks))

    gn = gn_ref[...]

    def head_norm_gate(blk, carry):
        rows = pl.ds(pl.multiple_of(blk * blk_len, blk_len), blk_len)
        o = o_scr[rows, :]
        o = o * lax.rsqrt(jnp.mean(o * o, axis=-1, keepdims=True) + EPS) * gn
        mix_ref[rows, :] = (o * _silu(ga_ref[rows, :])).astype(BF16)
        return carry

    lax.fori_loop(0, seq_len // blk_len, head_norm_gate, 0)


def _proj_spec(seq_len, section):
    return pl.BlockSpec((None, seq_len, HEAD_DIM), lambda b, h: (b, 0, section * 4 + h))


def _state_in_spec(layer):
    return pl.BlockSpec((None, None, 2, None, HEAD_DIM, HEAD_DIM), lambda b, h: (b, layer, 0, h, 0, 0))


def _state_out_spec():
    return pl.BlockSpec((None, 2, None, HEAD_DIM, HEAD_DIM), lambda b, h: (b, 0, h, 0, 0))


def _hgrn(proj, lb_logits, norm_gain, state, layer, emit_state):
    batch, seq_len, _ = proj.shape
    depth = lb_logits.shape[0]
    n_chunks = seq_len // HGRN_CHUNK
    has_s0 = state is not None
    in_specs = [_proj_spec(seq_len, s) for s in range(5)] + [
        pl.BlockSpec((depth, 2, HEAD_DIM), lambda b, h: (0, 0, h)),
        pl.BlockSpec((None, 1, HEAD_DIM), lambda b, h: (layer, 0, h)),
    ]
    args = [proj] * 5 + [lb_logits, norm_gain]
    if has_s0:
        in_specs.append(_state_in_spec(layer))
        args.append(state)
    out_specs = [pl.BlockSpec((None, seq_len, HEAD_DIM), lambda b, h: (b, 0, h))]
    out_shape = [jax.ShapeDtypeStruct((batch, seq_len, H_A * HEAD_DIM), BF16)]
    if emit_state:
        out_specs.append(_state_out_spec())
        out_shape.append(jax.ShapeDtypeStruct((batch, 2, H_A, HEAD_DIM, HEAD_DIM), F32))
    outs = pl.pallas_call(
        functools.partial(_hgrn_kernel, layer=layer, seq_len=seq_len, has_s0=has_s0, emit_state=emit_state),
        grid=(batch, H_A),
        in_specs=in_specs,
        out_specs=out_specs,
        out_shape=out_shape,
        scratch_shapes=[
            pltpu.VMEM((seq_len, HGRN_CHUNK), BF16),
            pltpu.VMEM((seq_len, 2 * HEAD_DIM), BF16),
            pltpu.VMEM((seq_len, HEAD_DIM), BF16),
            pltpu.VMEM((seq_len, HEAD_DIM), F32),
            pltpu.VMEM((n_chunks, HEAD_DIM, 2 * HEAD_DIM), F32),
            pltpu.VMEM((n_chunks, 1, 2 * HEAD_DIM), F32),
            pltpu.VMEM((n_chunks, HEAD_DIM, 2 * HEAD_DIM), BF16),
            pltpu.SMEM((n_chunks,), jnp.int32),
            pltpu.VMEM((HGRN_CHUNK, HEAD_DIM), F32),
            pltpu.VMEM((HGRN_CHUNK, HEAD_DIM), F32),
        ],
        compiler_params=pltpu.CompilerParams(vmem_limit_bytes=VMEM_LIMIT),
        name="hgrn_scan",
    )(*args)
    return (outs[0], outs[1]) if emit_state else (outs[0], None)


def _log_sigmoid(x):
    return jnp.minimum(x, 0.0) - jnp.log1p(jnp.exp(-jnp.abs(x)))


def _rope(x, cos, sin_signed):
    lane = lax.broadcasted_iota(jnp.int32, x.shape, 1)
    quarter = HEAD_DIM // 4
    partner = jnp.where(lane % (2 * quarter) < quarter, pltpu.roll(x, HEAD_DIM - quarter, 1), pltpu.roll(x, quarter, 1))
    return x * cos + partner * sin_signed


def _ret_kernel(*refs, layer, seq_len, has_s0, emit_state, use_rope):
    logit_ref, qb_ref, kb_ref, vb_ref, gb_ref, gn_ref = refs[:6]
    refs = refs[6:]
    if use_rope:
        cos_ref, sin_ref, refs = refs[0], refs[1], refs[2:]
    s0_ref = None
    if has_s0:
        s0_ref, refs = refs[0], refs[1:]
    mix_ref, refs = refs[0], refs[1:]
    if emit_state:
        st_out_ref, refs = refs[0], refs[1:]
    a16_scr, v16_scr, qe_scr, o_scr, u_scr, st_scr = refs

    c_len = RET_CHUNK
    n_chunks = seq_len // c_len
    head = pl.program_id(1)
    lg_f = _log_sigmoid(jnp.full((1, HEAD_DIM), logit_ref[layer, 0, head], F32))
    lg_b = _log_sigmoid(jnp.full((1, HEAD_DIM), logit_ref[layer, 1, head], F32))

    t = lax.broadcasted_iota(jnp.int32, (c_len, HEAD_DIM), 0).astype(F32)
    q_dec = jnp.concatenate([jnp.exp((t + 1.0) * lg_f), jnp.exp((c_len - t) * lg_b)], axis=1)
    k_dec = jnp.concatenate([jnp.exp((c_len - 1.0 - t) * lg_f), jnp.exp(t * lg_b)], axis=1)
    chunk_dec = jnp.concatenate([jnp.exp(c_len * lg_f), jnp.exp(c_len * lg_b)], axis=1)
    t_idx = lax.broadcasted_iota(jnp.int32, (c_len, c_len), 0)
    s_idx = lax.broadcasted_iota(jnp.int32, (c_len, c_len), 1)
    dist = (t_idx - s_idx).astype(F32)
    decay = jnp.where(t_idx >= s_idx, jnp.exp(jnp.maximum(dist, 0.0) * lg_f), 0.0)
    decay = decay + jnp.where(t_idx <= s_idx, jnp.exp(jnp.maximum(-dist, 0.0) * lg_b), 0.0)

    def prepare(c, carry):
        rows = pl.ds(pl.multiple_of(c * c_len, c_len), c_len)
        q = qb_ref[rows, :]
        k = kb_ref[rows, :] * (HEAD_DIM ** -0.5)
        v = vb_ref[rows, :]
        if use_rope:
            cos, sin = cos_ref[rows, :], sin_ref[rows, :]
            q = _rope(q, cos, sin)
            k = _rope(k, cos, sin)
        a16_scr[rows, :] = (_dot_nt(q.astype(BF16), k.astype(BF16)) * decay).astype(BF16)
        v16_scr[rows, :] = v.astype(BF16)
        qe_scr[rows, :] = (jnp.concatenate([q, q], axis=1) * q_dec).astype(BF16)
        u_scr[c] = _state_increment(v, (jnp.concatenate([k, k], axis=1) * k_dec).astype(BF16))
        return carry

    lax.fori_loop(0, n_chunks, prepare, 0, unroll=min(4, n_chunks))

    s_f, s_b = _run_states(n_chunks, lambda c: chunk_dec, u_scr, st_scr, *_initial_states(s0_ref))
    if emit_state:
        st_out_ref[0] = s_f.T
        st_out_ref[1] = s_b.T

    def combine(c, carry):
        rows = pl.ds(pl.multiple_of(c * c_len, c_len), c_len)
        o_scr[rows, :] = _dot(a16_scr[rows, :], v16_scr[rows, :]) + _dot_nt(qe_scr[rows, :], st_scr[c])
        return carry

    lax.fori_loop(0, n_chunks, combine, 0, unroll=min(8, n_chunks))

    gn = gn_ref[...]
    blk_len = min(RET_BLOCK, seq_len)

    def head_norm_gate(blk, carry):
        rows = pl.ds(pl.multiple_of(blk * blk_len, blk_len), blk_len)
        o = o_scr[rows, :]
        o = o - jnp.mean(o, axis=-1, keepdims=True)
        o = o * lax.rsqrt(jnp.mean(o * o, axis=-1, keepdims=True) + EPS) * gn
        mix_ref[rows, :] = (o * _silu(gb_ref[rows, :])).astype(BF16)
        return carry

    lax.fori_loop(0, seq_len // blk_len, head_norm_gate, 0)


def _ret(proj, decay_logit, norm_gain, rope_tables, state, layer, emit_state):
    batch, seq_len, _ = proj.shape
    n_chunks = seq_len // RET_CHUNK
    has_s0 = state is not None
    use_rope = rope_tables is not None
    in_specs = [pl.BlockSpec(memory_space=pltpu.SMEM)] + [_proj_spec(seq_len, s) for s in range(5, 9)] + [
        pl.BlockSpec((None, 1, HEAD_DIM), lambda b, h: (layer, 0, h)),
    ]
    args = [decay_logit] + [proj] * 4 + [norm_gain]
    if use_rope:
        in_specs += [pl.BlockSpec((seq_len, HEAD_DIM), lambda b, h: (0, 0))] * 2
        args += list(rope_tables)
    if has_s0:
        in_specs.append(_state_in_spec(layer))
        args.append(state)
    out_specs = [pl.BlockSpec((None, seq_len, HEAD_DIM), lambda b, h: (b, 0, h))]
    out_shape = [jax.ShapeDtypeStruct((batch, seq_len, H_B * HEAD_DIM), BF16)]
    if emit_state:
        out_specs.append(_state_out_spec())
        out_shape.append(jax.ShapeDtypeStruct((batch, 2, H_B, HEAD_DIM, HEAD_DIM), F32))
    outs = pl.pallas_call(
        functools.partial(
            _ret_kernel, layer=layer, seq_len=seq_len, has_s0=has_s0, emit_state=emit_state, use_rope=use_rope
        ),
        grid=(batch, H_B),
        in_specs=in_specs,
        out_specs=out_specs,
        out_shape=out_shape,
        scratch_shapes=[
            pltpu.VMEM((seq_len, RET_CHUNK), BF16),
            pltpu.VMEM((seq_len, HEAD_DIM), BF16),
            pltpu.VMEM((seq_len, 2 * HEAD_DIM), BF16),
            pltpu.VMEM((seq_len, HEAD_DIM), F32),
            pltpu.VMEM((n_chunks, HEAD_DIM, 2 * HEAD_DIM), F32),
            pltpu.VMEM((n_chunks, HEAD_DIM, 2 * HEAD_DIM), BF16),
        ],
        compiler_params=pltpu.CompilerParams(vmem_limit_bytes=VMEM_LIMIT),
        name="ret_scan",
    )(*args)
    return (outs[0], outs[1]) if emit_state else (outs[0], None)


def _rope_tables(seq_len):
    quarter = HEAD_DIM // 4
    inv = ROPE_BASE ** (-jnp.arange(quarter, dtype=F32) / quarter)
    pos = jnp.arange(seq_len, dtype=jnp.int32)
    ang_r = (pos // GRID_W).astype(F32)[:, None] * inv
    ang_c = (pos % GRID_W).astype(F32)[:, None] * inv
    cos = jnp.concatenate([jnp.cos(ang_r), jnp.cos(ang_r), jnp.cos(ang_c), jnp.cos(ang_c)], axis=1)
    sin = jnp.concatenate([-jnp.sin(ang_r), jnp.sin(ang_r), -jnp.sin(ang_c), jnp.sin(ang_c)], axis=1)
    return cos, sin


def _out_ffn_kernel(*refs, d_ff, final):
    ma_ref, mb_ref, x_ref, mod_ref, gain_ref, wo_ref, wu_ref, wd_ref = refs[:8]
    refs = refs[8:]
    if final:
        gfin_ref, refs = refs[0], refs[1:]
    out_ref, act_scr = refs

    mod = mod_ref[...]
    d_a = ma_ref.shape[1]
    y = _dot(ma_ref[...], wo_ref[:d_a, :]) + _dot(mb_ref[...], wo_ref[d_a:, :])
    x1 = x_ref[...] + mod[2:3] * y
    h = _modnorm(x1, gain_ref[...], mod[4:5], mod[3:4]).astype(BF16)
    for c in range(d_ff // FFN_CHUNK):
        lo = c * FFN_CHUNK
        gate = _dot(h, wu_ref[:, lo : lo + FFN_CHUNK])
        up = _dot(h, wu_ref[:, d_ff + lo : d_ff + lo + FFN_CHUNK])
        act_scr[:, lo : lo + FFN_CHUNK] = (_silu(gate) * up).astype(BF16)
    x2 = x1 + mod[5:6] * _dot(act_scr[...], wd_ref[...])
    if final:
        x2 = x2 * lax.rsqrt(jnp.mean(x2 * x2, axis=-1, keepdims=True) + EPS) * gfin_ref[...]
    out_ref[...] = x2


def _out_ffn(mix_a, mix_b, x, mod, gain, w_out, w_up, w_down, final_gain, layer, tile_m, cond_index):
    n_tok, d_model = x.shape
    d_a, d_b = mix_a.shape[1], mix_b.shape[1]
    d_ff = w_down.shape[1]
    final = final_gain is not None
    const = lambda i: (layer, 0, 0)
    in_specs = [
        pl.BlockSpec((tile_m, d_a), lambda i: (i, 0)),
        pl.BlockSpec((tile_m, d_b), lambda i: (i, 0)),
        pl.BlockSpec((tile_m, d_model), lambda i: (i, 0)),
        pl.BlockSpec((None, None, N_MOD, d_model), lambda i: (layer, cond_index(i), 0, 0)),
        pl.BlockSpec((None, 1, d_model), const),
        pl.BlockSpec((None, d_a + d_b, d_model), const, pipeline_mode=pl.Buffered(1)),
        pl.BlockSpec((None, d_model, 2 * d_ff), const, pipeline_mode=pl.Buffered(1)),
        pl.BlockSpec((None, d_ff, d_model), const, pipeline_mode=pl.Buffered(1)),
    ]
    args = [mix_a, mix_b, x, mod, gain, w_out, w_up, w_down]
    if final:
        in_specs.append(pl.BlockSpec((1, d_model), lambda i: (0, 0)))
        args.append(final_gain)
    return pl.pallas_call(
        functools.partial(_out_ffn_kernel, d_ff=d_ff, final=final),
        grid=(n_tok // tile_m,),
        in_specs=in_specs,
        out_specs=pl.BlockSpec((tile_m, d_model), lambda i: (i, 0)),
        out_shape=jax.ShapeDtypeStruct((n_tok, d_model), F32),
        scratch_shapes=[pltpu.VMEM((tile_m, d_ff), BF16)],
        compiler_params=pltpu.CompilerParams(vmem_limit_bytes=VMEM_LIMIT),
        name="out_ffn",
    )(*args)


def kernel(x_prompt, x_sample, state_hgrn, state_ret, c, c_ctx, ada_w, ada_b, norm_mix, norm_ffn, w_in,
           hgrn_lb_logits, hgrn_norm, ret_decay_logit, ret_norm, w_out, w_up, w_down, norm_final):
    batch, seq, d_model = x_prompt.shape
    dec_batch, dec_seq, _ = x_sample.shape
    depth = ada_w.shape[0]
    tile_m = 512
    assert dec_batch < COND_ROWS and dec_seq % tile_m == 0 and (batch * seq) % tile_m == 0
    assert dec_seq % GRID_W == 0
    for n_l in (seq, dec_seq):
        assert n_l % min(HGRN_BLOCK, n_l) == 0 and n_l % HGRN_CHUNK == 0
        assert n_l % min(RET_BLOCK, n_l) == 0 and n_l % RET_CHUNK == 0

    cond = jnp.concatenate(
        [c.astype(F32), c_ctx.astype(F32)[None, :], jnp.zeros((COND_ROWS - dec_batch - 1, d_model), F32)], axis=0
    )
    mod = _modulation(cond, ada_w, ada_b)

    w_in16, w_out16, w_up16, w_down16 = (w.astype(BF16) for w in (w_in, w_out, w_up, w_down))
    gain_mix = norm_mix.reshape(depth, 1, d_model)
    gain_ffn = norm_ffn.reshape(depth, 1, d_model)
    gain_hgrn = hgrn_norm.reshape(depth, 1, -1)
    gain_ret = ret_norm.reshape(depth, 1, -1)
    gain_final = norm_final.reshape(1, d_model)
    rope_tables = _rope_tables(dec_seq)

    tiles_per_seq = dec_seq // tile_m
    groups = [
        [x_prompt.astype(F32).reshape(batch * seq, d_model), batch, seq, lambda i: dec_batch, None, None, None, True],
        [x_sample.astype(F32).reshape(dec_batch * dec_seq, d_model), dec_batch, dec_seq,
         lambda i: i // tiles_per_seq, state_hgrn, state_ret, rope_tables, False],
    ]
    new_h, new_r = [], []
    for layer in range(depth):
        final_gain = gain_final if layer == depth - 1 else None
        for grp in groups:
            x, n_b, n_l, cond_index, s0_h, s0_r, tables, emit = grp
            proj = _in_proj(x, mod, gain_mix, w_in16, layer, tile_m, cond_index).reshape(n_b, n_l, D_IN)
            mix_a, st_a = _hgrn(proj, hgrn_lb_logits, gain_hgrn, s0_h, layer, emit)
            mix_b, st_b = _ret(proj, ret_decay_logit, gain_ret, tables, s0_r, layer, emit)
            grp[0] = _out_ffn(
                mix_a.reshape(n_b * n_l, -1), mix_b.reshape(n_b * n_l, -1), x, mod, gain_ffn,
                w_out16, w_up16, w_down16, final_gain, layer, tile_m, cond_index,
            )
            if emit:
                new_h.append(st_a)
                new_r.append(st_b)
    y_prompt = groups[0][0].reshape(batch, seq, d_model).astype(x_prompt.dtype)
    y_sample = groups[1][0].reshape(dec_batch, dec_seq, d_model).astype(x_sample.dtype)
    new_state_hgrn = jnp.stack(new_h, axis=1).astype(state_hgrn.dtype)
    new_state_ret = jnp.stack(new_r, axis=1).astype(state_ret.dtype)
    return (y_prompt, y_sample, new_state_hgrn, new_state_ret)
```

```python
import functools

import jax
import jax.numpy as jnp
from jax import lax
from jax.experimental import pallas as pl
from jax.experimental.pallas import tpu as pltpu

F32 = jnp.float32
BF16 = jnp.bfloat16

N_HEADS = 4
HEAD_DIM = 128
N_SECTIONS = 9
SECTION = N_HEADS * HEAD_DIM
D_IN = N_SECTIONS * SECTION
GRID_W = 64
ROPE_BASE = 10000.0
EPS = 1e-6
GATE_FLOOR = 1e-12

P32_QA, P32_KF, P32_KB, P32_GATE_A, P32_GATE_R = range(5)
P16_GF_HI, P16_GF_LO, P16_GB_HI, P16_GB_LO, P16_VA, P16_QR, P16_KR, P16_VR = range(8)
N_P32 = 5 * N_HEADS
N_P16 = 8 * N_HEADS

N_MOD = 6
COND_ROWS = 16

HGRN_CHUNK = 64
HGRN_BLOCK = 256
RET_CHUNK = 128
RET_BLOCK = 256
SAFE_EXP_RANGE = 60.0
SAFE_Q_MAX = 1e8
FFN_CHUNK = 256
MOD_TILE_N = 1536
VMEM_LIMIT = 56 * 1024 * 1024


def _silu(x):
    h = 0.5 * x
    return h + h * jnp.tanh(h)


def _dot(a, b):
    return jnp.dot(a, b, preferred_element_type=F32)


def _dot_nt(a, b):
    return lax.dot_general(a, b, (((1,), (1,)), ((), ())), preferred_element_type=F32)


def _split_bf16(x):
    hi = x.astype(BF16)
    return hi, (x - hi.astype(F32)).astype(BF16)


def _modnorm(x, gain, scale, shift):
    ms = jnp.mean(x * x, axis=-1, keepdims=True)
    return x * lax.rsqrt(ms + EPS) * gain * (1.0 + scale) + shift


def _mod_kernel(cond_ref, w_ref, b_ref, out_ref):
    s_hi, s_lo = _split_bf16(_silu(cond_ref[...]))
    w_hi, w_lo = _split_bf16(w_ref[...])
    s_both = jnp.concatenate([s_hi, s_lo], axis=0)
    p = _dot(s_both, w_hi) + _dot(s_both, w_lo)
    out_ref[...] = p[:COND_ROWS] + p[COND_ROWS:] + b_ref[...]


def _modulation(cond, ada_w, ada_b):
    depth, d_model, n_out = ada_w.shape
    out = pl.pallas_call(
        _mod_kernel,
        grid=(depth, n_out // MOD_TILE_N),
        in_specs=[
            pl.BlockSpec((COND_ROWS, d_model), lambda l, j: (0, 0)),
            pl.BlockSpec((None, d_model, MOD_TILE_N), lambda l, j: (l, 0, j)),
            pl.BlockSpec((None, 1, MOD_TILE_N), lambda l, j: (l, 0, j)),
        ],
        out_specs=pl.BlockSpec((None, COND_ROWS, MOD_TILE_N), lambda l, j: (l, 0, j)),
        out_shape=jax.ShapeDtypeStruct((depth, COND_ROWS, n_out), F32),
        compiler_params=pltpu.CompilerParams(vmem_limit_bytes=VMEM_LIMIT),
        name="adaln_mod",
    )(cond, ada_w, ada_b.reshape(depth, 1, n_out))
    return out.reshape(depth, COND_ROWS, N_MOD, d_model)


def _rope(x, cos, sin_signed):
    lane = lax.broadcasted_iota(jnp.int32, x.shape, 1)
    quarter = HEAD_DIM // 4
    partner = jnp.where(lane % (2 * quarter) < quarter, pltpu.roll(x, HEAD_DIM - quarter, 1), pltpu.roll(x, quarter, 1))
    return x * cos + partner * sin_signed


def _hgrn_lower_bounds(logits, layer):
    e = jnp.exp(logits - jnp.max(logits, axis=0, keepdims=True))
    p = e / jnp.sum(e, axis=0, keepdims=True)
    cum = p[0]
    for i in range(1, layer + 1):
        cum = cum + p[i]
    return jnp.clip(cum - p[0], 0.0, 1.0)


def _in_proj_kernel(*refs, layer, use_rope):
    x_ref, mod_ref, gain_ref, w_ref, lbl_ref = refs[:5]
    refs = refs[5:]
    if use_rope:
        cos_ref, sin_ref, refs = refs[0], refs[1], refs[2:]
    p32_ref, p16_ref = refs

    mod = mod_ref[...]
    h = _modnorm(x_ref[...], gain_ref[...], mod[1:2], mod[0:1]).astype(BF16)
    lb = _hgrn_lower_bounds(lbl_ref[...], layer)

    def proj(j):
        return _dot(h, w_ref[:, j * SECTION : (j + 1) * SECTION])

    def put(ref, section, val):
        for hd in range(N_HEADS):
            ref[section * N_HEADS + hd] = val[:, hd * HEAD_DIM : (hd + 1) * HEAD_DIM].astype(ref.dtype)

    put(p32_ref, P32_QA, _silu(proj(0)))
    for d, (sec_hi, sec_lo, sec_k) in enumerate(
        ((P16_GF_HI, P16_GF_LO, P32_KF), (P16_GB_HI, P16_GB_LO, P32_KB))
    ):
        lb_d = lb[d : d + 1]
        c0, c1 = 0.5 + 0.5 * lb_d, 0.5 - 0.5 * lb_d
        ct = c1 * jnp.tanh(0.5 * proj(1 + d))
        g_hi, g_lo = _split_bf16(jnp.log(jnp.maximum(c0 + ct, GATE_FLOOR)))
        put(p16_ref, sec_hi, g_hi)
        put(p16_ref, sec_lo, g_lo)
        put(p32_ref, sec_k, c1 - ct)
    put(p16_ref, P16_VA, proj(3))
    put(p32_ref, P32_GATE_A, _silu(proj(4)))
    q = proj(5)
    k = proj(6) * (HEAD_DIM ** -0.5)
    if use_rope:
        cos, sin = cos_ref[...], sin_ref[...]
        heads = [slice(hd * HEAD_DIM, (hd + 1) * HEAD_DIM) for hd in range(N_HEADS)]
        q = jnp.concatenate([_rope(q[:, sl], cos, sin) for sl in heads], axis=1)
        k = jnp.concatenate([_rope(k[:, sl], cos, sin) for sl in heads], axis=1)
    put(p16_ref, P16_QR, q)
    put(p16_ref, P16_KR, k)
    put(p16_ref, P16_VR, proj(7))
    put(p32_ref, P32_GATE_R, _silu(proj(8)))


def _in_proj(x, mod, gain, w, lb_logits, rope_tables, layer, tile_m, cond_index, tiles_per_seq):
    n_tok, d_model = x.shape
    depth = lb_logits.shape[0]
    use_rope = rope_tables is not None
    in_specs = [
        pl.BlockSpec((tile_m, d_model), lambda i: (i, 0)),
        pl.BlockSpec((None, None, N_MOD, d_model), lambda i: (layer, cond_index(i), 0, 0)),
        pl.BlockSpec((None, 1, d_model), lambda i: (layer, 0, 0)),
        pl.BlockSpec((None, d_model, D_IN), lambda i: (layer, 0, 0), pipeline_mode=pl.Buffered(1)),
        pl.BlockSpec((depth, 2, SECTION), lambda i: (0, 0, 0)),
    ]
    args = [x, mod, gain, w, lb_logits]
    if use_rope:
        in_specs += [pl.BlockSpec((tile_m, HEAD_DIM), lambda i: (i % tiles_per_seq, 0))] * 2
        args += list(rope_tables)
    return pl.pallas_call(
        functools.partial(_in_proj_kernel, layer=layer, use_rope=use_rope),
        grid=(n_tok // tile_m,),
        in_specs=in_specs,
        out_specs=[
            pl.BlockSpec((N_P32, tile_m, HEAD_DIM), lambda i: (0, i, 0)),
            pl.BlockSpec((N_P16, tile_m, HEAD_DIM), lambda i: (0, i, 0)),
        ],
        out_shape=[
            jax.ShapeDtypeStruct((N_P32, n_tok, HEAD_DIM), F32),
            jax.ShapeDtypeStruct((N_P16, n_tok, HEAD_DIM), BF16),
        ],
        compiler_params=pltpu.CompilerParams(vmem_limit_bytes=VMEM_LIMIT),
        name="in_proj",
    )(*args)


def _rope_tables(seq_len):
    quarter = HEAD_DIM // 4
    inv = ROPE_BASE ** (-jnp.arange(quarter, dtype=F32) / quarter)
    pos = jnp.arange(seq_len, dtype=jnp.int32)
    ang_r = (pos // GRID_W).astype(F32)[:, None] * inv
    ang_c = (pos % GRID_W).astype(F32)[:, None] * inv
    cos = jnp.concatenate([jnp.cos(ang_r), jnp.cos(ang_r), jnp.cos(ang_c), jnp.cos(ang_c)], axis=1)
    sin = jnp.concatenate([-jnp.sin(ang_r), jnp.sin(ang_r), -jnp.sin(ang_c), jnp.sin(ang_c)], axis=1)
    return cos, sin


def _state_increment(v16, k_cat):
    return _dot(v16.astype(F32).T.astype(BF16), k_cat)


def _run_states(n_chunks, dec_of, u_scr, st_scr, s_f, s_b):
    def step(i, carry):
        s_f, s_b = carry
        cf, cb = i, n_chunks - 1 - i
        st_scr[cf, :, :HEAD_DIM] = s_f.astype(BF16)
        st_scr[cb, :, HEAD_DIM:] = s_b.astype(BF16)
        s_f = dec_of(cf)[:, :HEAD_DIM] * s_f + u_scr[cf, :, :HEAD_DIM]
        s_b = dec_of(cb)[:, HEAD_DIM:] * s_b + u_scr[cb, :, HEAD_DIM:]
        return s_f, s_b

    return lax.fori_loop(0, n_chunks, step, (s_f, s_b), unroll=2)


def _initial_states(s0_ref):
    if s0_ref is None:
        zero = jnp.zeros((HEAD_DIM, HEAD_DIM), F32)
        return zero, zero
    return s0_ref[0].T, s0_ref[1].T


def _slab_spec(seq_len, section):
    return pl.BlockSpec((None, None, seq_len, HEAD_DIM), lambda b, h: (section * N_HEADS + h, b, 0, 0))


def _state_in_spec(layer):
    return pl.BlockSpec((None, None, 2, None, HEAD_DIM, HEAD_DIM), lambda b, h: (b, layer, 0, h, 0, 0))


def _state_out_spec():
    return pl.BlockSpec((None, 2, None, HEAD_DIM, HEAD_DIM), lambda b, h: (b, 0, h, 0, 0))


def _mix_out(batch, seq_len):
    spec = pl.BlockSpec((None, None, seq_len, HEAD_DIM), lambda b, h: (h, b, 0, 0))
    return spec, jax.ShapeDtypeStruct((N_HEADS, batch, seq_len, HEAD_DIM), BF16)


def _cumsum_rows(g, reverse):
    n = g.shape[0]
    pos = lax.broadcasted_iota(jnp.int32, g.shape, 0)
    sh = 1
    while sh < n:
        if reverse:
            g = g + jnp.where(pos < n - sh, pltpu.roll(g, n - sh, 0), 0.0)
        else:
            g = g + jnp.where(pos >= sh, pltpu.roll(g, sh, 0), 0.0)
        sh *= 2
    return g


def _causal_mask(n, lower):
    t_idx = lax.broadcasted_iota(jnp.int32, (n, n), 0)
    s_idx = lax.broadcasted_iota(jnp.int32, (n, n), 1)
    return t_idx >= s_idx if lower else t_idx <= s_idx


def _hgrn_direct_scores(q, k, b, k_scr, b_scr):
    n = q.shape[0]
    k_scr[...] = k
    b_scr[...] = b
    col = lax.broadcasted_iota(jnp.int32, (n, n), 1)

    def body(s, acc):
        z = q * k_scr[pl.ds(s, 1), :] * jnp.exp(jnp.minimum(b - b_scr[pl.ds(s, 1), :], 0.0))
        return jnp.where(col == s, jnp.sum(z, axis=1, keepdims=True), acc)

    return lax.fori_loop(0, n, body, jnp.zeros((n, n), F32))


def _hgrn_kernel(*refs, seq_len, has_s0, emit_state):
    qa_ref, kf_ref, kb_ref, gate_ref, gf_hi_ref, gf_lo_ref, gb_hi_ref, gb_lo_ref, va_ref, gn_ref = refs[:10]
    refs = refs[10:]
    s0_ref = None
    if has_s0:
        s0_ref, refs = refs[0], refs[1:]
    mix_ref, refs = refs[0], refs[1:]
    if emit_state:
        st_out_ref, refs = refs[0], refs[1:]
    tri_scr, cum_scr, a16_scr, qe_scr, o_scr, u_scr, dec_scr, st_scr, safe_smem, k_scr, b_scr = refs

    c_len = HGRN_CHUNK
    blk_len = min(HGRN_BLOCK, seq_len)
    n_chunks = seq_len // c_len
    chunks_per_blk = blk_len // c_len
    directions = ((kf_ref, gf_hi_ref, gf_lo_ref, False), (kb_ref, gb_hi_ref, gb_lo_ref, True))
    causal = (_causal_mask(c_len, True), _causal_mask(c_len, False))

    t_idx = lax.broadcasted_iota(jnp.int32, (blk_len, blk_len), 0)
    s_idx = lax.broadcasted_iota(jnp.int32, (blk_len, blk_len), 1)
    same_chunk = (t_idx // c_len) == (s_idx // c_len)
    tri_scr[0] = jnp.where(jnp.logical_and(same_chunk, s_idx <= t_idx), 1.0, 0.0).astype(BF16)
    tri_scr[1] = jnp.where(jnp.logical_and(same_chunk, s_idx >= t_idx), 1.0, 0.0).astype(BF16)

    def cumulate(blk, carry):
        rows = pl.ds(pl.multiple_of(blk * blk_len, blk_len), blk_len)
        for d, (_, g_hi_ref, g_lo_ref, _) in enumerate(directions):
            cum = _dot(tri_scr[d], jnp.concatenate([g_hi_ref[rows, :], g_lo_ref[rows, :]], axis=1))
            cum_scr[d, rows, :] = cum[:, :HEAD_DIM] + cum[:, HEAD_DIM:]
        return carry

    lax.fori_loop(0, seq_len // blk_len, cumulate, 0, unroll=min(2, seq_len // blk_len))

    def prepare(blk, carry):
        r0 = pl.multiple_of(blk * blk_len, blk_len)
        rows = pl.ds(r0, blk_len)
        q = qa_ref[rows, :]
        v16 = va_ref[rows, :]
        q_ok = jnp.max(jnp.abs(q)) <= SAFE_Q_MAX
        gates = [(k_ref[rows, :], cum_scr[d, rows, :]) for d, (k_ref, _, _, _) in enumerate(directions)]
        for j in range(chunks_per_blk):
            sl = slice(j * c_len, (j + 1) * c_len)
            rows_j = pl.ds(r0 + j * c_len, c_len)
            c = blk * chunks_per_blk + j
            q_e, k_e, dec, span, scores = [], [], [], None, None
            for d, (k, b) in enumerate(gates):
                k_j, b_j = k[sl], b[sl]
                tot = b_j[0:1] if directions[d][3] else b_j[c_len - 1 : c_len]
                dist = b_j - b_j[c_len // 2 - 1 : c_len // 2]
                a = _dot_nt((q[sl] * jnp.exp(dist)).astype(BF16), (k_j * jnp.exp(-dist)).astype(BF16))
                a = jnp.where(causal[d], a, 0.0)
                scores = a if scores is None else scores + a
                q_e.append(q[sl] * jnp.exp(b_j))
                k_e.append(k_j * jnp.exp(tot - b_j))
                dec.append(jnp.exp(tot))
                ends = jnp.maximum(jnp.abs(dist[0:1]), jnp.abs(dist[c_len - 1 : c_len]))
                span = ends if span is None else jnp.maximum(span, ends)
            a16_scr[rows_j, :] = scores.astype(BF16)
            qe_scr[rows_j, :] = jnp.concatenate(q_e, axis=1).astype(BF16)
            u_scr[c] = _state_increment(v16[sl], jnp.concatenate(k_e, axis=1).astype(BF16))
            dec_scr[c] = jnp.concatenate(dec, axis=1)
            safe_smem[c] = jnp.logical_and(q_ok, jnp.max(span) <= SAFE_EXP_RANGE).astype(jnp.int32)
        return carry

    lax.fori_loop(0, seq_len // blk_len, prepare, 0)

    s_f, s_b = _run_states(n_chunks, lambda c: dec_scr[c], u_scr, st_scr, *_initial_states(s0_ref))
    if emit_state:
        st_out_ref[0] = s_f.T
        st_out_ref[1] = s_b.T

    def redo_scores(c, carry):
        @pl.when(safe_smem[c] == 0)
        def _():
            rows = pl.ds(pl.multiple_of(c * c_len, c_len), c_len)
            q = qa_ref[rows, :]
            scores = jnp.zeros((c_len, c_len), F32)
            for d, (k_ref, g_hi_ref, g_lo_ref, rev) in enumerate(directions):
                b = _cumsum_rows(g_hi_ref[rows, :].astype(F32) + g_lo_ref[rows, :].astype(F32), rev)
                scores = scores + jnp.where(causal[d], _hgrn_direct_scores(q, k_ref[rows, :], b, k_scr, b_scr), 0.0)
            a16_scr[rows, :] = scores.astype(BF16)

        return carry

    lax.fori_loop(0, n_chunks, redo_scores, 0)

    def combine(c, carry):
        rows = pl.ds(pl.multiple_of(c * c_len, c_len), c_len)
        o_scr[rows, :] = _dot(a16_scr[rows, :], va_ref[rows, :]) + _dot_nt(qe_scr[rows, :], st_scr[c])
        return carry

    lax.fori_loop(0, n_chunks, combine, 0, unroll=min(8, n_chunks))

    gn = gn_ref[...]

    def head_norm_gate(blk, carry):
        rows = pl.ds(pl.multiple_of(blk * blk_len, blk_len), blk_len)
        o = o_scr[rows, :]
        o = o * lax.rsqrt(jnp.mean(o * o, axis=-1, keepdims=True) + EPS) * gn
        mix_ref[rows, :] = (o * gate_ref[rows, :]).astype(BF16)
        return carry

    lax.fori_loop(0, seq_len // blk_len, head_norm_gate, 0)


def _hgrn(p32, p16, norm_gain, state, layer, emit_state):
    _, batch, seq_len, _ = p32.shape
    n_chunks = seq_len // HGRN_CHUNK
    blk_len = min(HGRN_BLOCK, seq_len)
    has_s0 = state is not None
    f32_slabs = (P32_QA, P32_KF, P32_KB, P32_GATE_A)
    bf16_slabs = (P16_GF_HI, P16_GF_LO, P16_GB_HI, P16_GB_LO, P16_VA)
    in_specs = [_slab_spec(seq_len, s) for s in f32_slabs + bf16_slabs]
    in_specs.append(pl.BlockSpec((None, 1, HEAD_DIM), lambda b, h: (layer, 0, h)))
    args = [p32] * len(f32_slabs) + [p16] * len(bf16_slabs) + [norm_gain]
    if has_s0:
        in_specs.append(_state_in_spec(layer))
        args.append(state)
    mix_spec, mix_shape = _mix_out(batch, seq_len)
    out_specs, out_shape = [mix_spec], [mix_shape]
    if emit_state:
        out_specs.append(_state_out_spec())
        out_shape.append(jax.ShapeDtypeStruct((batch, 2, N_HEADS, HEAD_DIM, HEAD_DIM), F32))
    outs = pl.pallas_call(
        functools.partial(_hgrn_kernel, seq_len=seq_len, has_s0=has_s0, emit_state=emit_state),
        grid=(batch, N_HEADS),
        in_specs=in_specs,
        out_specs=out_specs,
        out_shape=out_shape,
        scratch_shapes=[
            pltpu.VMEM((2, blk_len, blk_len), BF16),
            pltpu.VMEM((2, seq_len, HEAD_DIM), F32),
            pltpu.VMEM((seq_len, HGRN_CHUNK), BF16),
            pltpu.VMEM((seq_len, 2 * HEAD_DIM), BF16),
            pltpu.VMEM((seq_len, HEAD_DIM), F32),
            pltpu.VMEM((n_chunks, HEAD_DIM, 2 * HEAD_DIM), F32),
            pltpu.VMEM((n_chunks, 1, 2 * HEAD_DIM), F32),
            pltpu.VMEM((n_chunks, HEAD_DIM, 2 * HEAD_DIM), BF16),
            pltpu.SMEM((n_chunks,), jnp.int32),
            pltpu.VMEM((HGRN_CHUNK, HEAD_DIM), F32),
            pltpu.VMEM((HGRN_CHUNK, HEAD_DIM), F32),
        ],
        compiler_params=pltpu.CompilerParams(vmem_limit_bytes=VMEM_LIMIT),
        name="hgrn_scan",
    )(*args)
    return (outs[0], outs[1]) if emit_state else (outs[0], None)


def _log_sigmoid(x):
    return jnp.minimum(x, 0.0) - jnp.log1p(jnp.exp(-jnp.abs(x)))


def _ret_kernel(*refs, layer, seq_len, has_s0, emit_state):
    logit_ref, q_ref, k_ref, v_ref, gate_ref, gn_ref = refs[:6]
    refs = refs[6:]
    s0_ref = None
    if has_s0:
        s0_ref, refs = refs[0], refs[1:]
    mix_ref, refs = refs[0], refs[1:]
    if emit_state:
        st_out_ref, refs = refs[0], refs[1:]
    a16_scr, o_scr, u_scr, st_scr = refs

    c_len = RET_CHUNK
    n_chunks = seq_len // c_len
    head = pl.program_id(1)
    lg_f = _log_sigmoid(jnp.full((1, HEAD_DIM), logit_ref[layer, 0, head], F32))
    lg_b = _log_sigmoid(jnp.full((1, HEAD_DIM), logit_ref[layer, 1, head], F32))

    t = lax.broadcasted_iota(jnp.int32, (c_len, HEAD_DIM), 0).astype(F32)
    q_dec_f = jnp.exp((t + 1.0) * lg_f)
    q_dec_b = jnp.exp((c_len - t) * lg_b)
    k_dec = jnp.concatenate([jnp.exp((c_len - 1.0 - t) * lg_f), jnp.exp(t * lg_b)], axis=1)
    chunk_dec = jnp.concatenate([jnp.exp(c_len * lg_f), jnp.exp(c_len * lg_b)], axis=1)
    t_idx = lax.broadcasted_iota(jnp.int32, (c_len, c_len), 0)
    s_idx = lax.broadcasted_iota(jnp.int32, (c_len, c_len), 1)
    dist = (t_idx - s_idx).astype(F32)
    decay = jnp.where(t_idx >= s_idx, jnp.exp(jnp.maximum(dist, 0.0) * lg_f), 0.0)
    decay = decay + jnp.where(t_idx <= s_idx, jnp.exp(jnp.maximum(-dist, 0.0) * lg_b), 0.0)

    def prepare(c, carry):
        rows = pl.ds(pl.multiple_of(c * c_len, c_len), c_len)
        k16 = k_ref[rows, :]
        a16_scr[rows, :] = (_dot_nt(q_ref[rows, :], k16) * decay).astype(BF16)
        k = k16.astype(F32)
        u_scr[c] = _state_increment(v_ref[rows, :], (jnp.concatenate([k, k], axis=1) * k_dec).astype(BF16))
        return carry

    lax.fori_loop(0, n_chunks, prepare, 0, unroll=min(4, n_chunks))

    s_f, s_b = _run_states(n_chunks, lambda c: chunk_dec, u_scr, st_scr, *_initial_states(s0_ref))
    if emit_state:
        st_out_ref[0] = s_f.T
        st_out_ref[1] = s_b.T

    def combine(c, carry):
        rows = pl.ds(pl.multiple_of(c * c_len, c_len), c_len)
        q16 = q_ref[rows, :]
        o = _dot(a16_scr[rows, :], v_ref[rows, :])
        o = o + q_dec_f * _dot_nt(q16, st_scr[c, :, :HEAD_DIM]) + q_dec_b * _dot_nt(q16, st_scr[c, :, HEAD_DIM:])
        o_scr[rows, :] = o
        return carry

    lax.fori_loop(0, n_chunks, combine, 0, unroll=min(8, n_chunks))

    gn = gn_ref[...]
    blk_len = min(RET_BLOCK, seq_len)

    def head_norm_gate(blk, carry):
        rows = pl.ds(pl.multiple_of(blk * blk_len, blk_len), blk_len)
        o = o_scr[rows, :]
        o = o - jnp.mean(o, axis=-1, keepdims=True)
        o = o * lax.rsqrt(jnp.mean(o * o, axis=-1, keepdims=True) + EPS) * gn
        mix_ref[rows, :] = (o * gate_ref[rows, :]).astype(BF16)
        return carry

    lax.fori_loop(0, seq_len // blk_len, head_norm_gate, 0)


def _ret(p32, p16, decay_logit, norm_gain, state, layer, emit_state):
    _, batch, seq_len, _ = p32.shape
    n_chunks = seq_len // RET_CHUNK
    has_s0 = state is not None
    in_specs = [pl.BlockSpec(memory_space=pltpu.SMEM)]
    in_specs += [_slab_spec(seq_len, s) for s in (P16_QR, P16_KR, P16_VR, P32_GATE_R)]
    in_specs.append(pl.BlockSpec((None, 1, HEAD_DIM), lambda b, h: (layer, 0, h)))
    args = [decay_logit, p16, p16, p16, p32, norm_gain]
    if has_s0:
        in_specs.append(_state_in_spec(layer))
        args.append(state)
    mix_spec, mix_shape = _mix_out(batch, seq_len)
    out_specs, out_shape = [mix_spec], [mix_shape]
    if emit_state:
        out_specs.append(_state_out_spec())
        out_shape.append(jax.ShapeDtypeStruct((batch, 2, N_HEADS, HEAD_DIM, HEAD_DIM), F32))
    outs = pl.pallas_call(
        functools.partial(_ret_kernel, layer=layer, seq_len=seq_len, has_s0=has_s0, emit_state=emit_state),
        grid=(batch, N_HEADS),
        in_specs=in_specs,
        out_specs=out_specs,
        out_shape=out_shape,
        scratch_shapes=[
            pltpu.VMEM((seq_len, RET_CHUNK), BF16),
            pltpu.VMEM((seq_len, HEAD_DIM), F32),
            pltpu.VMEM((n_chunks, HEAD_DIM, 2 * HEAD_DIM), F32),
            pltpu.VMEM((n_chunks, HEAD_DIM, 2 * HEAD_DIM), BF16),
        ],
        compiler_params=pltpu.CompilerParams(vmem_limit_bytes=VMEM_LIMIT),
        name="ret_scan",
    )(*args)
    return (outs[0], outs[1]) if emit_state else (outs[0], None)


def _out_ffn_kernel(*refs, d_ff, final):
    ma_ref, mb_ref, x_ref, mod_ref, gain_ref, wo_ref, wu_ref, wd_ref = refs[:8]
    refs = refs[8:]
    if final:
        gfin_ref, refs = refs[0], refs[1:]
    out_ref, act_scr = refs

    mod = mod_ref[...]
    mix = jnp.concatenate([ma_ref[hd] for hd in range(N_HEADS)] + [mb_ref[hd] for hd in range(N_HEADS)], axis=1)
    x1 = x_ref[...] + mod[2:3] * _dot(mix, wo_ref[...])
    h = _modnorm(x1, gain_ref[...], mod[4:5], mod[3:4]).astype(BF16)
    for c in range(d_ff // FFN_CHUNK):
        lo = c * FFN_CHUNK
        gate = _dot(h, wu_ref[:, lo : lo + FFN_CHUNK])
        up = _dot(h, wu_ref[:, d_ff + lo : d_ff + lo + FFN_CHUNK])
        act_scr[:, lo : lo + FFN_CHUNK] = (_silu(gate) * up).astype(BF16)
    x2 = x1 + mod[5:6] * _dot(act_scr[...], wd_ref[...])
    if final:
        x2 = x2 * lax.rsqrt(jnp.mean(x2 * x2, axis=-1, keepdims=True) + EPS) * gfin_ref[...]
    out_ref[...] = x2


def _out_ffn(mix_a, mix_b, x, mod, gain, w_out, w_up, w_down, final_gain, layer, tile_m, cond_index):
    n_tok, d_model = x.shape
    d_ff = w_down.shape[1]
    final = final_gain is not None
    const = lambda i: (layer, 0, 0)
    in_specs = [
        pl.BlockSpec((N_HEADS, tile_m, HEAD_DIM), lambda i: (0, i, 0)),
        pl.BlockSpec((N_HEADS, tile_m, HEAD_DIM), lambda i: (0, i, 0)),
        pl.BlockSpec((tile_m, d_model), lambda i: (i, 0)),
        pl.BlockSpec((None, None, N_MOD, d_model), lambda i: (layer, cond_index(i), 0, 0)),
        pl.BlockSpec((None, 1, d_model), const),
        pl.BlockSpec((None, 2 * N_HEADS * HEAD_DIM, d_model), const, pipeline_mode=pl.Buffered(1)),
        pl.BlockSpec((None, d_model, 2 * d_ff), const, pipeline_mode=pl.Buffered(1)),
        pl.BlockSpec((None, d_ff, d_model), const, pipeline_mode=pl.Buffered(1)),
    ]
    args = [mix_a, mix_b, x, mod, gain, w_out, w_up, w_down]
    if final:
        in_specs.append(pl.BlockSpec((1, d_model), lambda i: (0, 0)))
        args.append(final_gain)
    return pl.pallas_call(
        functools.partial(_out_ffn_kernel, d_ff=d_ff, final=final),
        grid=(n_tok // tile_m,),
        in_specs=in_specs,
        out_specs=pl.BlockSpec((tile_m, d_model), lambda i: (i, 0)),
        out_shape=jax.ShapeDtypeStruct((n_tok, d_model), F32),
        scratch_shapes=[pltpu.VMEM((tile_m, d_ff), BF16)],
        compiler_params=pltpu.CompilerParams(vmem_limit_bytes=VMEM_LIMIT),
        name="out_ffn",
    )(*args)


def kernel(x_prompt, x_sample, state_hgrn, state_ret, c, c_ctx, ada_w, ada_b, norm_mix, norm_ffn, w_in,
           hgrn_lb_logits, hgrn_norm, ret_decay_logit, ret_norm, w_out, w_up, w_down, norm_final):
    batch, seq, d_model = x_prompt.shape
    dec_batch, dec_seq, _ = x_sample.shape
    depth = ada_w.shape[0]
    tile_m = 512
    assert w_in.shape[2] == D_IN and state_hgrn.shape[3] == N_HEADS and state_ret.shape[3] == N_HEADS
    assert dec_batch < COND_ROWS and dec_seq % tile_m == 0 and (batch * seq) % tile_m == 0
    assert dec_seq % GRID_W == 0
    for n_l in (seq, dec_seq):
        assert n_l % min(HGRN_BLOCK, n_l) == 0 and min(HGRN_BLOCK, n_l) % HGRN_CHUNK == 0
        assert n_l % min(RET_BLOCK, n_l) == 0 and n_l % RET_CHUNK == 0

    cond = jnp.concatenate(
        [c.astype(F32), c_ctx.astype(F32)[None, :], jnp.zeros((COND_ROWS - dec_batch - 1, d_model), F32)], axis=0
    )
    mod = _modulation(cond, ada_w, ada_b)

    w_in16, w_out16, w_up16, w_down16 = (w.astype(BF16) for w in (w_in, w_out, w_up, w_down))
    gain_mix = norm_mix.reshape(depth, 1, d_model)
    gain_ffn = norm_ffn.reshape(depth, 1, d_model)
    gain_hgrn = hgrn_norm.reshape(depth, 1, -1)
    gain_ret = ret_norm.reshape(depth, 1, -1)
    gain_final = norm_final.reshape(1, d_model)
    rope_tables = _rope_tables(dec_seq)

    tiles_per_seq = dec_seq // tile_m
    groups = [
        [x_prompt.astype(F32).reshape(batch * seq, d_model), batch, seq, lambda i: dec_batch, None, None, None, True],
        [x_sample.astype(F32).reshape(dec_batch * dec_seq, d_model), dec_batch, dec_seq,
         lambda i: i // tiles_per_seq, state_hgrn, state_ret, rope_tables, False],
    ]
    new_h, new_r = [], []
    for layer in range(depth):
        final_gain = gain_final if layer == depth - 1 else None
        for grp in groups:
            x, n_b, n_l, cond_index, s0_h, s0_r, tables, emit = grp
            p32, p16 = _in_proj(x, mod, gain_mix, w_in16, hgrn_lb_logits, tables, layer, tile_m, cond_index, tiles_per_seq)
            p32 = p32.reshape(N_P32, n_b, n_l, HEAD_DIM)
            p16 = p16.reshape(N_P16, n_b, n_l, HEAD_DIM)
            mix_a, st_a = _hgrn(p32, p16, gain_hgrn, s0_h, layer, emit)
            mix_b, st_b = _ret(p32, p16, ret_decay_logit, gain_ret, s0_r, layer, emit)
            grp[0] = _out_ffn(
                mix_a.reshape(N_HEADS, n_b * n_l, HEAD_DIM), mix_b.reshape(N_HEADS, n_b * n_l, HEAD_DIM), x, mod,
                gain_ffn, w_out16, w_up16, w_down16, final_gain, layer, tile_m, cond_index,
            )
            if emit:
                new_h.append(st_a)
                new_r.append(st_b)
    y_prompt = groups[0][0].reshape(batch, seq, d_model).astype(x_prompt.dtype)
    y_sample = groups[1][0].reshape(dec_batch, dec_seq, d_model).astype(x_sample.dtype)
    new_state_hgrn = jnp.stack(new_h, axis=1).astype(state_hgrn.dtype)
    new_state_ret = jnp.stack(new_r, axis=1).astype(state_ret.dtype)
    return (y_prompt, y_sample, new_state_hgrn, new_state_ret)
```

```python
import functools

import jax
import jax.numpy as jnp
from jax import lax
from jax.experimental import pallas as pl
from jax.experimental.pallas import tpu as pltpu

F32 = jnp.float32
BF16 = jnp.bfloat16

N_HEADS = 4
HEAD_DIM = 128
N_SECTIONS = 9
SECTION = N_HEADS * HEAD_DIM
D_IN = N_SECTIONS * SECTION
GRID_W = 64
ROPE_BASE = 10000.0
EPS = 1e-6
GATE_FLOOR = 1e-12

P32_QA, P32_KF, P32_KB, P32_GATE_A, P32_GATE_R = range(5)
P16_GF_HI, P16_GF_LO, P16_GB_HI, P16_GB_LO, P16_VA, P16_QR, P16_KR, P16_VR = range(8)
N_P32 = 5 * N_HEADS
N_P16 = 8 * N_HEADS

N_MOD = 6
COND_ROWS = 16

HGRN_CHUNK = 64
HGRN_BLOCK = 256
RET_CHUNK = 128
RET_BLOCK = 256
SAFE_EXP_RANGE = 60.0
SAFE_Q_MAX = 1e8
FFN_CHUNK = 256
MOD_TILE_N = 1536
VMEM_LIMIT = 56 * 1024 * 1024


def _silu(x):
    h = 0.5 * x
    return h + h * jnp.tanh(h)


def _dot(a, b):
    return jnp.dot(a, b, preferred_element_type=F32)


def _dot_nt(a, b):
    return lax.dot_general(a, b, (((1,), (1,)), ((), ())), preferred_element_type=F32)


def _split_bf16(x):
    hi = x.astype(BF16)
    return hi, (x - hi.astype(F32)).astype(BF16)


def _modnorm(x, gain, scale, shift):
    ms = jnp.mean(x * x, axis=-1, keepdims=True)
    return x * lax.rsqrt(ms + EPS) * gain * (1.0 + scale) + shift


def _mod_kernel(cond_ref, w_ref, b_ref, out_ref):
    s_hi, s_lo = _split_bf16(_silu(cond_ref[...]))
    w_hi, w_lo = _split_bf16(w_ref[...])
    s_both = jnp.concatenate([s_hi, s_lo], axis=0)
    p = _dot(s_both, w_hi) + _dot(s_both, w_lo)
    out_ref[...] = p[:COND_ROWS] + p[COND_ROWS:] + b_ref[...]


def _modulation(cond, ada_w, ada_b):
    depth, d_model, n_out = ada_w.shape
    out = pl.pallas_call(
        _mod_kernel,
        grid=(depth, n_out // MOD_TILE_N),
        in_specs=[
            pl.BlockSpec((COND_ROWS, d_model), lambda l, j: (0, 0)),
            pl.BlockSpec((None, d_model, MOD_TILE_N), lambda l, j: (l, 0, j)),
            pl.BlockSpec((None, 1, MOD_TILE_N), lambda l, j: (l, 0, j)),
        ],
        out_specs=pl.BlockSpec((None, COND_ROWS, MOD_TILE_N), lambda l, j: (l, 0, j)),
        out_shape=jax.ShapeDtypeStruct((depth, COND_ROWS, n_out), F32),
        compiler_params=pltpu.CompilerParams(vmem_limit_bytes=VMEM_LIMIT),
        name="adaln_mod",
    )(cond, ada_w, ada_b.reshape(depth, 1, n_out))
    return out.reshape(depth, COND_ROWS, N_MOD, d_model)


def _rope(x, cos, sin_signed):
    lane = lax.broadcasted_iota(jnp.int32, x.shape, 1)
    quarter = HEAD_DIM // 4
    partner = jnp.where(lane % (2 * quarter) < quarter, pltpu.roll(x, HEAD_DIM - quarter, 1), pltpu.roll(x, quarter, 1))
    return x * cos + partner * sin_signed


def _hgrn_lower_bounds(logits, layer):
    e = jnp.exp(logits - jnp.max(logits, axis=0, keepdims=True))
    p = e / jnp.sum(e, axis=0, keepdims=True)
    cum = p[0]
    for i in range(1, layer + 1):
        cum = cum + p[i]
    return jnp.clip(cum - p[0], 0.0, 1.0)


def _in_proj_kernel(*refs, layer, use_rope):
    x_ref, mod_ref, gain_ref, w_ref, lbl_ref = refs[:5]
    refs = refs[5:]
    if use_rope:
        cos_ref, sin_ref, refs = refs[0], refs[1], refs[2:]
    p32_ref, p16_ref = refs

    mod = mod_ref[...]
    h = _modnorm(x_ref[...], gain_ref[...], mod[1:2], mod[0:1]).astype(BF16)
    lb = _hgrn_lower_bounds(lbl_ref[...], layer)

    def proj(j):
        return _dot(h, w_ref[:, j * SECTION : (j + 1) * SECTION])

    def put(ref, section, val):
        for hd in range(N_HEADS):
            ref[section * N_HEADS + hd] = val[:, hd * HEAD_DIM : (hd + 1) * HEAD_DIM].astype(ref.dtype)

    put(p32_ref, P32_QA, _silu(proj(0)))
    for d, (sec_hi, sec_lo, sec_k) in enumerate(
        ((P16_GF_HI, P16_GF_LO, P32_KF), (P16_GB_HI, P16_GB_LO, P32_KB))
    ):
        lb_d = lb[d : d + 1]
        c0, c1 = 0.5 + 0.5 * lb_d, 0.5 - 0.5 * lb_d
        ct = c1 * jnp.tanh(0.5 * proj(1 + d))
        g_hi, g_lo = _split_bf16(jnp.log(jnp.maximum(c0 + ct, GATE_FLOOR)))
        put(p16_ref, sec_hi, g_hi)
        put(p16_ref, sec_lo, g_lo)
        put(p32_ref, sec_k, c1 - ct)
    put(p16_ref, P16_VA, proj(3))
    put(p32_ref, P32_GATE_A, _silu(proj(4)))
    q = proj(5)
    k = proj(6) * (HEAD_DIM ** -0.5)
    if use_rope:
        cos, sin = cos_ref[...], sin_ref[...]
        heads = [slice(hd * HEAD_DIM, (hd + 1) * HEAD_DIM) for hd in range(N_HEADS)]
        q = jnp.concatenate([_rope(q[:, sl], cos, sin) for sl in heads], axis=1)
        k = jnp.concatenate([_rope(k[:, sl], cos, sin) for sl in heads], axis=1)
    put(p16_ref, P16_QR, q)
    put(p16_ref, P16_KR, k)
    put(p16_ref, P16_VR, proj(7))
    put(p32_ref, P32_GATE_R, _silu(proj(8)))


def _in_proj(x, mod, gain, w, lb_logits, rope_tables, layer, tile_m, cond_index, tiles_per_seq):
    n_tok, d_model = x.shape
    depth = lb_logits.shape[0]
    use_rope = rope_tables is not None
    in_specs = [
        pl.BlockSpec((tile_m, d_model), lambda i: (i, 0)),
        pl.BlockSpec((None, None, N_MOD, d_model), lambda i: (layer, cond_index(i), 0, 0)),
        pl.BlockSpec((None, 1, d_model), lambda i: (layer, 0, 0)),
        pl.BlockSpec((None, d_model, D_IN), lambda i: (layer, 0, 0), pipeline_mode=pl.Buffered(1)),
        pl.BlockSpec((depth, 2, SECTION), lambda i: (0, 0, 0)),
    ]
    args = [x, mod, gain, w, lb_logits]
    if use_rope:
        in_specs += [pl.BlockSpec((tile_m, HEAD_DIM), lambda i: (i % tiles_per_seq, 0))] * 2
        args += list(rope_tables)
    return pl.pallas_call(
        functools.partial(_in_proj_kernel, layer=layer, use_rope=use_rope),
        grid=(n_tok // tile_m,),
        in_specs=in_specs,
        out_specs=[
            pl.BlockSpec((N_P32, tile_m, HEAD_DIM), lambda i: (0, i, 0)),
            pl.BlockSpec((N_P16, tile_m, HEAD_DIM), lambda i: (0, i, 0)),
        ],
        out_shape=[
            jax.ShapeDtypeStruct((N_P32, n_tok, HEAD_DIM), F32),
            jax.ShapeDtypeStruct((N_P16, n_tok, HEAD_DIM), BF16),
        ],
        compiler_params=pltpu.CompilerParams(vmem_limit_bytes=VMEM_LIMIT),
        name="in_proj",
    )(*args)


def _rope_tables(seq_len):
    quarter = HEAD_DIM // 4
    inv = ROPE_BASE ** (-jnp.arange(quarter, dtype=F32) / quarter)
    pos = jnp.arange(seq_len, dtype=jnp.int32)
    ang_r = (pos // GRID_W).astype(F32)[:, None] * inv
    ang_c = (pos % GRID_W).astype(F32)[:, None] * inv
    cos = jnp.concatenate([jnp.cos(ang_r), jnp.cos(ang_r), jnp.cos(ang_c), jnp.cos(ang_c)], axis=1)
    sin = jnp.concatenate([-jnp.sin(ang_r), jnp.sin(ang_r), -jnp.sin(ang_c), jnp.sin(ang_c)], axis=1)
    return cos, sin


def _state_increment(v16, k_cat):
    return _dot(v16.astype(F32).T.astype(BF16), k_cat)


def _run_states(n_chunks, dec_of, u_scr, st_scr, s_f, s_b):
    def step(i, carry):
        s_f, s_b = carry
        cf, cb = i, n_chunks - 1 - i
        st_scr[cf, :, :HEAD_DIM] = s_f.astype(BF16)
        st_scr[cb, :, HEAD_DIM:] = s_b.astype(BF16)
        s_f = dec_of(cf)[:, :HEAD_DIM] * s_f + u_scr[cf, :, :HEAD_DIM]
        s_b = dec_of(cb)[:, HEAD_DIM:] * s_b + u_scr[cb, :, HEAD_DIM:]
        return s_f, s_b

    return lax.fori_loop(0, n_chunks, step, (s_f, s_b), unroll=2)


def _initial_states(s0_ref):
    if s0_ref is None:
        zero = jnp.zeros((HEAD_DIM, HEAD_DIM), F32)
        return zero, zero
    return s0_ref[0].T, s0_ref[1].T


def _slab_spec(seq_len, section):
    return pl.BlockSpec((None, None, seq_len, HEAD_DIM), lambda b, h: (section * N_HEADS + h, b, 0, 0))


def _state_in_spec(layer):
    return pl.BlockSpec((None, None, 2, None, HEAD_DIM, HEAD_DIM), lambda b, h: (b, layer, 0, h, 0, 0))


def _state_out_spec():
    return pl.BlockSpec((None, 2, None, HEAD_DIM, HEAD_DIM), lambda b, h: (b, 0, h, 0, 0))


def _mix_out(batch, seq_len):
    spec = pl.BlockSpec((None, None, seq_len, HEAD_DIM), lambda b, h: (h, b, 0, 0))
    return spec, jax.ShapeDtypeStruct((N_HEADS, batch, seq_len, HEAD_DIM), BF16)


def _cumsum_rows(g, reverse):
    n = g.shape[0]
    pos = lax.broadcasted_iota(jnp.int32, g.shape, 0)
    sh = 1
    while sh < n:
        if reverse:
            g = g + jnp.where(pos < n - sh, pltpu.roll(g, n - sh, 0), 0.0)
        else:
            g = g + jnp.where(pos >= sh, pltpu.roll(g, sh, 0), 0.0)
        sh *= 2
    return g


def _causal_mask(n, lower):
    t_idx = lax.broadcasted_iota(jnp.int32, (n, n), 0)
    s_idx = lax.broadcasted_iota(jnp.int32, (n, n), 1)
    return t_idx >= s_idx if lower else t_idx <= s_idx


def _hgrn_direct_scores(q, k, b, k_scr, b_scr):
    n = q.shape[0]
    k_scr[...] = k
    b_scr[...] = b
    col = lax.broadcasted_iota(jnp.int32, (n, n), 1)

    def body(s, acc):
        z = q * k_scr[pl.ds(s, 1), :] * jnp.exp(jnp.minimum(b - b_scr[pl.ds(s, 1), :], 0.0))
        return jnp.where(col == s, jnp.sum(z, axis=1, keepdims=True), acc)

    return lax.fori_loop(0, n, body, jnp.zeros((n, n), F32))


def _hgrn_kernel(*refs, seq_len, has_s0, emit_state):
    qa_ref, kf_ref, kb_ref, gate_ref, gf_hi_ref, gf_lo_ref, gb_hi_ref, gb_lo_ref, va_ref, gn_ref = refs[:10]
    refs = refs[10:]
    s0_ref = None
    if has_s0:
        s0_ref, refs = refs[0], refs[1:]
    mix_ref, refs = refs[0], refs[1:]
    if emit_state:
        st_out_ref, refs = refs[0], refs[1:]
    tri_scr, cum_scr, a16_scr, qe_scr, o_scr, u_scr, dec_scr, st_scr, safe_smem, k_scr, b_scr = refs

    c_len = HGRN_CHUNK
    blk_len = min(HGRN_BLOCK, seq_len)
    n_chunks = seq_len // c_len
    chunks_per_blk = blk_len // c_len
    directions = ((kf_ref, gf_hi_ref, gf_lo_ref, False), (kb_ref, gb_hi_ref, gb_lo_ref, True))
    causal = (_causal_mask(c_len, True), _causal_mask(c_len, False))

    t_idx = lax.broadcasted_iota(jnp.int32, (blk_len, blk_len), 0)
    s_idx = lax.broadcasted_iota(jnp.int32, (blk_len, blk_len), 1)
    same_chunk = (t_idx // c_len) == (s_idx // c_len)
    tri_scr[0] = jnp.where(jnp.logical_and(same_chunk, s_idx <= t_idx), 1.0, 0.0).astype(BF16)
    tri_scr[1] = jnp.where(jnp.logical_and(same_chunk, s_idx >= t_idx), 1.0, 0.0).astype(BF16)

    def cumulate(blk, carry):
        rows = pl.ds(pl.multiple_of(blk * blk_len, blk_len), blk_len)
        for d, (_, g_hi_ref, g_lo_ref, _) in enumerate(directions):
            cum = _dot(tri_scr[d], jnp.concatenate([g_hi_ref[rows, :], g_lo_ref[rows, :]], axis=1))
            cum_scr[d, rows, :] = cum[:, :HEAD_DIM] + cum[:, HEAD_DIM:]
        return carry

    lax.fori_loop(0, seq_len // blk_len, cumulate, 0, unroll=min(2, seq_len // blk_len))

    def prepare(blk, carry):
        r0 = pl.multiple_of(blk * blk_len, blk_len)
        rows = pl.ds(r0, blk_len)
        q = qa_ref[rows, :]
        v16 = va_ref[rows, :]
        q_ok = jnp.max(jnp.abs(q)) <= SAFE_Q_MAX
        gates = [(k_ref[rows, :], cum_scr[d, rows, :]) for d, (k_ref, _, _, _) in enumerate(directions)]
        for j in range(chunks_per_blk):
            sl = slice(j * c_len, (j + 1) * c_len)
            rows_j = pl.ds(r0 + j * c_len, c_len)
            c = blk * chunks_per_blk + j
            q_e, k_e, dec, span, scores = [], [], [], None, None
            for d, (k, b) in enumerate(gates):
                k_j, b_j = k[sl], b[sl]
                tot = b_j[0:1] if directions[d][3] else b_j[c_len - 1 : c_len]
                dist = b_j - b_j[c_len // 2 - 1 : c_len // 2]
                a = _dot_nt((q[sl] * jnp.exp(dist)).astype(BF16), (k_j * jnp.exp(-dist)).astype(BF16))
                a = jnp.where(causal[d], a, 0.0)
                scores = a if scores is None else scores + a
                q_e.append(q[sl] * jnp.exp(b_j))
                k_e.append(k_j * jnp.exp(tot - b_j))
                dec.append(jnp.exp(tot))
                ends = jnp.maximum(jnp.abs(dist[0:1]), jnp.abs(dist[c_len - 1 : c_len]))
                span = ends if span is None else jnp.maximum(span, ends)
            a16_scr[rows_j, :] = scores.astype(BF16)
            qe_scr[rows_j, :] = jnp.concatenate(q_e, axis=1).astype(BF16)
            u_scr[c] = _state_increment(v16[sl], jnp.concatenate(k_e, axis=1).astype(BF16))
            dec_scr[c] = jnp.concatenate(dec, axis=1)
            safe_smem[c] = jnp.logical_and(q_ok, jnp.max(span) <= SAFE_EXP_RANGE).astype(jnp.int32)
        return carry

    lax.fori_loop(0, seq_len // blk_len, prepare, 0, unroll=min(2, seq_len // blk_len))

    s_f, s_b = _run_states(n_chunks, lambda c: dec_scr[c], u_scr, st_scr, *_initial_states(s0_ref))
    if emit_state:
        st_out_ref[0] = s_f.T
        st_out_ref[1] = s_b.T

    def redo_scores(c, carry):
        @pl.when(safe_smem[c] == 0)
        def _():
            rows = pl.ds(pl.multiple_of(c * c_len, c_len), c_len)
            q = qa_ref[rows, :]
            scores = jnp.zeros((c_len, c_len), F32)
            for d, (k_ref, g_hi_ref, g_lo_ref, rev) in enumerate(directions):
                b = _cumsum_rows(g_hi_ref[rows, :].astype(F32) + g_lo_ref[rows, :].astype(F32), rev)
                scores = scores + jnp.where(causal[d], _hgrn_direct_scores(q, k_ref[rows, :], b, k_scr, b_scr), 0.0)
            a16_scr[rows, :] = scores.astype(BF16)

        return carry

    lax.fori_loop(0, n_chunks, redo_scores, 0)

    def combine(c, carry):
        rows = pl.ds(pl.multiple_of(c * c_len, c_len), c_len)
        o_scr[rows, :] = _dot(a16_scr[rows, :], va_ref[rows, :]) + _dot_nt(qe_scr[rows, :], st_scr[c])
        return carry

    lax.fori_loop(0, n_chunks, combine, 0, unroll=min(8, n_chunks))

    gn = gn_ref[...]

    def head_norm_gate(blk, carry):
        rows = pl.ds(pl.multiple_of(blk * blk_len, blk_len), blk_len)
        o = o_scr[rows, :]
        o = o * lax.rsqrt(jnp.mean(o * o, axis=-1, keepdims=True) + EPS) * gn
        mix_ref[rows, :] = (o * gate_ref[rows, :]).astype(BF16)
        return carry

    lax.fori_loop(0, seq_len // blk_len, head_norm_gate, 0, unroll=min(4, seq_len // blk_len))


def _hgrn(p32, p16, norm_gain, state, layer, emit_state):
    _, batch, seq_len, _ = p32.shape
    n_chunks = seq_len // HGRN_CHUNK
    blk_len = min(HGRN_BLOCK, seq_len)
    has_s0 = state is not None
    f32_slabs = (P32_QA, P32_KF, P32_KB, P32_GATE_A)
    bf16_slabs = (P16_GF_HI, P16_GF_LO, P16_GB_HI, P16_GB_LO, P16_VA)
    in_specs = [_slab_spec(seq_len, s) for s in f32_slabs + bf16_slabs]
    in_specs.append(pl.BlockSpec((None, 1, HEAD_DIM), lambda b, h: (layer, 0, h)))
    args = [p32] * len(f32_slabs) + [p16] * len(bf16_slabs) + [norm_gain]
    if has_s0:
        in_specs.append(_state_in_spec(layer))
        args.append(state)
    mix_spec, mix_shape = _mix_out(batch, seq_len)
    out_specs, out_shape = [mix_spec], [mix_shape]
    if emit_state:
        out_specs.append(_state_out_spec())
        out_shape.append(jax.ShapeDtypeStruct((batch, 2, N_HEADS, HEAD_DIM, HEAD_DIM), F32))
    outs = pl.pallas_call(
        functools.partial(_hgrn_kernel, seq_len=seq_len, has_s0=has_s0, emit_state=emit_state),
        grid=(batch, N_HEADS),
        in_specs=in_specs,
        out_specs=out_specs,
        out_shape=out_shape,
        scratch_shapes=[
            pltpu.VMEM((2, blk_len, blk_len), BF16),
            pltpu.VMEM((2, seq_len, HEAD_DIM), F32),
            pltpu.VMEM((seq_len, HGRN_CHUNK), BF16),
            pltpu.VMEM((seq_len, 2 * HEAD_DIM), BF16),
            pltpu.VMEM((seq_len, HEAD_DIM), F32),
            pltpu.VMEM((n_chunks, HEAD_DIM, 2 * HEAD_DIM), F32),
            pltpu.VMEM((n_chunks, 1, 2 * HEAD_DIM), F32),
            pltpu.VMEM((n_chunks, HEAD_DIM, 2 * HEAD_DIM), BF16),
            pltpu.SMEM((n_chunks,), jnp.int32),
            pltpu.VMEM((HGRN_CHUNK, HEAD_DIM), F32),
            pltpu.VMEM((HGRN_CHUNK, HEAD_DIM), F32),
        ],
        compiler_params=pltpu.CompilerParams(vmem_limit_bytes=VMEM_LIMIT),
        name="hgrn_scan",
    )(*args)
    return (outs[0], outs[1]) if emit_state else (outs[0], None)


def _log_sigmoid(x):
    return jnp.minimum(x, 0.0) - jnp.log1p(jnp.exp(-jnp.abs(x)))


def _ret_kernel(*refs, layer, seq_len, has_s0, emit_state):
    logit_ref, q_ref, k_ref, v_ref, gate_ref, gn_ref = refs[:6]
    refs = refs[6:]
    s0_ref = None
    if has_s0:
        s0_ref, refs = refs[0], refs[1:]
    mix_ref, refs = refs[0], refs[1:]
    if emit_state:
        st_out_ref, refs = refs[0], refs[1:]
    a16_scr, o_scr, u_scr, st_scr = refs

    c_len = RET_CHUNK
    n_chunks = seq_len // c_len
    head = pl.program_id(1)
    lg_f = _log_sigmoid(jnp.full((1, HEAD_DIM), logit_ref[layer, 0, head], F32))
    lg_b = _log_sigmoid(jnp.full((1, HEAD_DIM), logit_ref[layer, 1, head], F32))

    t = lax.broadcasted_iota(jnp.int32, (c_len, HEAD_DIM), 0).astype(F32)
    q_dec_f = jnp.exp((t + 1.0) * lg_f)
    q_dec_b = jnp.exp((c_len - t) * lg_b)
    k_dec = jnp.concatenate([jnp.exp((c_len - 1.0 - t) * lg_f), jnp.exp(t * lg_b)], axis=1)
    chunk_dec = jnp.concatenate([jnp.exp(c_len * lg_f), jnp.exp(c_len * lg_b)], axis=1)
    t_idx = lax.broadcasted_iota(jnp.int32, (c_len, c_len), 0)
    s_idx = lax.broadcasted_iota(jnp.int32, (c_len, c_len), 1)
    dist = (t_idx - s_idx).astype(F32)
    decay = jnp.where(t_idx >= s_idx, jnp.exp(jnp.maximum(dist, 0.0) * lg_f), 0.0)
    decay = decay + jnp.where(t_idx <= s_idx, jnp.exp(jnp.maximum(-dist, 0.0) * lg_b), 0.0)

    def prepare(c, carry):
        rows = pl.ds(pl.multiple_of(c * c_len, c_len), c_len)
        k16 = k_ref[rows, :]
        a16_scr[rows, :] = (_dot_nt(q_ref[rows, :], k16) * decay).astype(BF16)
        k = k16.astype(F32)
        u_scr[c] = _state_increment(v_ref[rows, :], (jnp.concatenate([k, k], axis=1) * k_dec).astype(BF16))
        return carry

    lax.fori_loop(0, n_chunks, prepare, 0, unroll=min(4, n_chunks))

    s_f, s_b = _run_states(n_chunks, lambda c: chunk_dec, u_scr, st_scr, *_initial_states(s0_ref))
    if emit_state:
        st_out_ref[0] = s_f.T
        st_out_ref[1] = s_b.T

    def combine(c, carry):
        rows = pl.ds(pl.multiple_of(c * c_len, c_len), c_len)
        q16 = q_ref[rows, :]
        o = _dot(a16_scr[rows, :], v_ref[rows, :])
        o = o + q_dec_f * _dot_nt(q16, st_scr[c, :, :HEAD_DIM]) + q_dec_b * _dot_nt(q16, st_scr[c, :, HEAD_DIM:])
        o_scr[rows, :] = o
        return carry

    lax.fori_loop(0, n_chunks, combine, 0, unroll=min(8, n_chunks))

    gn = gn_ref[...]
    blk_len = min(RET_BLOCK, seq_len)

    def head_norm_gate(blk, carry):
        rows = pl.ds(pl.multiple_of(blk * blk_len, blk_len), blk_len)
        o = o_scr[rows, :]
        o = o - jnp.mean(o, axis=-1, keepdims=True)
        o = o * lax.rsqrt(jnp.mean(o * o, axis=-1, keepdims=True) + EPS) * gn
        mix_ref[rows, :] = (o * gate_ref[rows, :]).astype(BF16)
        return carry

    lax.fori_loop(0, seq_len // blk_len, head_norm_gate, 0, unroll=min(4, seq_len // blk_len))


def _ret(p32, p16, decay_logit, norm_gain, state, layer, emit_state):
    _, batch, seq_len, _ = p32.shape
    n_chunks = seq_len // RET_CHUNK
    has_s0 = state is not None
    in_specs = [pl.BlockSpec(memory_space=pltpu.SMEM)]
    in_specs += [_slab_spec(seq_len, s) for s in (P16_QR, P16_KR, P16_VR, P32_GATE_R)]
    in_specs.append(pl.BlockSpec((None, 1, HEAD_DIM), lambda b, h: (layer, 0, h)))
    args = [decay_logit, p16, p16, p16, p32, norm_gain]
    if has_s0:
        in_specs.append(_state_in_spec(layer))
        args.append(state)
    mix_spec, mix_shape = _mix_out(batch, seq_len)
    out_specs, out_shape = [mix_spec], [mix_shape]
    if emit_state:
        out_specs.append(_state_out_spec())
        out_shape.append(jax.ShapeDtypeStruct((batch, 2, N_HEADS, HEAD_DIM, HEAD_DIM), F32))
    outs = pl.pallas_call(
        functools.partial(_ret_kernel, layer=layer, seq_len=seq_len, has_s0=has_s0, emit_state=emit_state),
        grid=(batch, N_HEADS),
        in_specs=in_specs,
        out_specs=out_specs,
        out_shape=out_shape,
        scratch_shapes=[
            pltpu.VMEM((seq_len, RET_CHUNK), BF16),
            pltpu.VMEM((seq_len, HEAD_DIM), F32),
            pltpu.VMEM((n_chunks, HEAD_DIM, 2 * HEAD_DIM), F32),
            pltpu.VMEM((n_chunks, HEAD_DIM, 2 * HEAD_DIM), BF16),
        ],
        compiler_params=pltpu.CompilerParams(vmem_limit_bytes=VMEM_LIMIT),
        name="ret_scan",
    )(*args)
    return (outs[0], outs[1]) if emit_state else (outs[0], None)


def _out_ffn_kernel(*refs, d_ff, final):
    ma_ref, mb_ref, x_ref, mod_ref, gain_ref, wo_ref, wu_ref, wd_ref = refs[:8]
    refs = refs[8:]
    if final:
        gfin_ref, refs = refs[0], refs[1:]
    out_ref, act_scr = refs

    mod = mod_ref[...]
    mix = jnp.concatenate([ma_ref[hd] for hd in range(N_HEADS)] + [mb_ref[hd] for hd in range(N_HEADS)], axis=1)
    x1 = x_ref[...] + mod[2:3] * _dot(mix, wo_ref[...])
    h = _modnorm(x1, gain_ref[...], mod[4:5], mod[3:4]).astype(BF16)
    for c in range(d_ff // FFN_CHUNK):
        lo = c * FFN_CHUNK
        gate = _dot(h, wu_ref[:, lo : lo + FFN_CHUNK])
        up = _dot(h, wu_ref[:, d_ff + lo : d_ff + lo + FFN_CHUNK])
        act_scr[:, lo : lo + FFN_CHUNK] = (_silu(gate) * up).astype(BF16)
    x2 = x1 + mod[5:6] * _dot(act_scr[...], wd_ref[...])
    if final:
        x2 = x2 * lax.rsqrt(jnp.mean(x2 * x2, axis=-1, keepdims=True) + EPS) * gfin_ref[...]
    out_ref[...] = x2


def _out_ffn(mix_a, mix_b, x, mod, gain, w_out, w_up, w_down, final_gain, layer, tile_m, cond_index):
    n_tok, d_model = x.shape
    d_ff = w_down.shape[1]
    final = final_gain is not None
    const = lambda i: (layer, 0, 0)
    in_specs = [
        pl.BlockSpec((N_HEADS, tile_m, HEAD_DIM), lambda i: (0, i, 0)),
        pl.BlockSpec((N_HEADS, tile_m, HEAD_DIM), lambda i: (0, i, 0)),
        pl.BlockSpec((tile_m, d_model), lambda i: (i, 0)),
        pl.BlockSpec((None, None, N_MOD, d_model), lambda i: (layer, cond_index(i), 0, 0)),
        pl.BlockSpec((None, 1, d_model), const),
        pl.BlockSpec((None, 2 * N_HEADS * HEAD_DIM, d_model), const, pipeline_mode=pl.Buffered(1)),
        pl.BlockSpec((None, d_model, 2 * d_ff), const, pipeline_mode=pl.Buffered(1)),
        pl.BlockSpec((None, d_ff, d_model), const, pipeline_mode=pl.Buffered(1)),
    ]
    args = [mix_a, mix_b, x, mod, gain, w_out, w_up, w_down]
    if final:
        in_specs.append(pl.BlockSpec((1, d_model), lambda i: (0, 0)))
        args.append(final_gain)
    return pl.pallas_call(
        functools.partial(_out_ffn_kernel, d_ff=d_ff, final=final),
        grid=(n_tok // tile_m,),
        in_specs=in_specs,
        out_specs=pl.BlockSpec((tile_m, d_model), lambda i: (i, 0)),
        out_shape=jax.ShapeDtypeStruct((n_tok, d_model), F32),
        scratch_shapes=[pltpu.VMEM((tile_m, d_ff), BF16)],
        compiler_params=pltpu.CompilerParams(vmem_limit_bytes=VMEM_LIMIT),
        name="out_ffn",
    )(*args)


def kernel(x_prompt, x_sample, state_hgrn, state_ret, c, c_ctx, ada_w, ada_b, norm_mix, norm_ffn, w_in,
           hgrn_lb_logits, hgrn_norm, ret_decay_logit, ret_norm, w_out, w_up, w_down, norm_final):
    batch, seq, d_model = x_prompt.shape
    dec_batch, dec_seq, _ = x_sample.shape
    depth = ada_w.shape[0]
    tile_m = 512
    assert w_in.shape[2] == D_IN and state_hgrn.shape[3] == N_HEADS and state_ret.shape[3] == N_HEADS
    assert dec_batch < COND_ROWS and dec_seq % tile_m == 0 and (batch * seq) % tile_m == 0
    assert dec_seq % GRID_W == 0
    for n_l in (seq, dec_seq):
        assert n_l % min(HGRN_BLOCK, n_l) == 0 and min(HGRN_BLOCK, n_l) % HGRN_CHUNK == 0
        assert n_l % min(RET_BLOCK, n_l) == 0 and n_l % RET_CHUNK == 0

    cond = jnp.concatenate(
        [c.astype(F32), c_ctx.astype(F32)[None, :], jnp.zeros((COND_ROWS - dec_batch - 1, d_model), F32)], axis=0
    )
    mod = _modulation(cond, ada_w, ada_b)

    w_in16, w_out16, w_up16, w_down16 = (w.astype(BF16) for w in (w_in, w_out, w_up, w_down))
    gain_mix = norm_mix.reshape(depth, 1, d_model)
    gain_ffn = norm_ffn.reshape(depth, 1, d_model)
    gain_hgrn = hgrn_norm.reshape(depth, 1, -1)
    gain_ret = ret_norm.reshape(depth, 1, -1)
    gain_final = norm_final.reshape(1, d_model)
    rope_tables = _rope_tables(dec_seq)

    tiles_per_seq = dec_seq // tile_m
    groups = [
        [x_prompt.astype(F32).reshape(batch * seq, d_model), batch, seq, lambda i: dec_batch, None, None, None, True],
        [x_sample.astype(F32).reshape(dec_batch * dec_seq, d_model), dec_batch, dec_seq,
         lambda i: i // tiles_per_seq, state_hgrn, state_ret, rope_tables, False],
    ]
    new_h, new_r = [], []
    for layer in range(depth):
        final_gain = gain_final if layer == depth - 1 else None
        for grp in groups:
            x, n_b, n_l, cond_index, s0_h, s0_r, tables, emit = grp
            p32, p16 = _in_proj(x, mod, gain_mix, w_in16, hgrn_lb_logits, tables, layer, tile_m, cond_index, tiles_per_seq)
            p32 = p32.reshape(N_P32, n_b, n_l, HEAD_DIM)
            p16 = p16.reshape(N_P16, n_b, n_l, HEAD_DIM)
            mix_a, st_a = _hgrn(p32, p16, gain_hgrn, s0_h, layer, emit)
            mix_b, st_b = _ret(p32, p16, ret_decay_logit, gain_ret, s0_r, layer, emit)
            grp[0] = _out_ffn(
                mix_a.reshape(N_HEADS, n_b * n_l, HEAD_DIM), mix_b.reshape(N_HEADS, n_b * n_l, HEAD_DIM), x, mod,
                gain_ffn, w_out16, w_up16, w_down16, final_gain, layer, tile_m, cond_index,
            )
            if emit:
                new_h.append(st_a)
                new_r.append(st_b)
    y_prompt = groups[0][0].reshape(batch, seq, d_model).astype(x_prompt.dtype)
    y_sample = groups[1][0].reshape(dec_batch, dec_seq, d_model).astype(x_sample.dtype)
    new_state_hgrn = jnp.stack(new_h, axis=1).astype(state_hgrn.dtype)
    new_state_ret = jnp.stack(new_r, axis=1).astype(state_ret.dtype)
    return (y_prompt, y_sample, new_state_hgrn, new_state_ret)
```

```python
import functools

import jax
import jax.numpy as jnp
from jax import lax
from jax.experimental import pallas as pl
from jax.experimental.pallas import tpu as pltpu

F32 = jnp.float32
BF16 = jnp.bfloat16

N_HEADS = 4
HEAD_DIM = 128
N_SECTIONS = 9
SECTION = N_HEADS * HEAD_DIM
D_IN = N_SECTIONS * SECTION
GRID_W = 64
ROPE_BASE = 10000.0
EPS = 1e-6
GATE_FLOOR = 1e-12

P32_QA, P32_KF, P32_KB, P32_GATE_A, P32_GATE_R = range(5)
P16_GF_HI, P16_GF_LO, P16_GB_HI, P16_GB_LO, P16_VA, P16_QR, P16_KR, P16_VR = range(8)
N_P32 = 5 * N_HEADS
N_P16 = 8 * N_HEADS

N_MOD = 6
COND_ROWS = 16

HGRN_CHUNK = 64
HGRN_BLOCK = 256
RET_CHUNK = 128
RET_BLOCK = 256
SCAN_ROWS_PER_STEP = 1024
SAFE_EXP_RANGE = 60.0
SAFE_Q_MAX = 1e8
FFN_CHUNK = 256
MOD_TILE_N = 1536
VMEM_LIMIT = 56 * 1024 * 1024


def _silu(x):
    h = 0.5 * x
    return h + h * jnp.tanh(h)


def _dot(a, b):
    return jnp.dot(a, b, preferred_element_type=F32)


def _dot_nt(a, b):
    return lax.dot_general(a, b, (((1,), (1,)), ((), ())), preferred_element_type=F32)


def _split_bf16(x):
    hi = x.astype(BF16)
    return hi, (x - hi.astype(F32)).astype(BF16)


def _modnorm(x, gain, scale, shift):
    ms = jnp.mean(x * x, axis=-1, keepdims=True)
    return x * lax.rsqrt(ms + EPS) * gain * (1.0 + scale) + shift


def _mod_kernel(cond_ref, w_ref, b_ref, out_ref):
    s_hi, s_lo = _split_bf16(_silu(cond_ref[...]))
    w_hi, w_lo = _split_bf16(w_ref[...])
    s_both = jnp.concatenate([s_hi, s_lo], axis=0)
    p = _dot(s_both, w_hi) + _dot(s_both, w_lo)
    out_ref[...] = p[:COND_ROWS] + p[COND_ROWS:] + b_ref[...]


def _modulation(cond, ada_w, ada_b):
    depth, d_model, n_out = ada_w.shape
    out = pl.pallas_call(
        _mod_kernel,
        grid=(depth, n_out // MOD_TILE_N),
        in_specs=[
            pl.BlockSpec((COND_ROWS, d_model), lambda l, j: (0, 0)),
            pl.BlockSpec((None, d_model, MOD_TILE_N), lambda l, j: (l, 0, j)),
            pl.BlockSpec((None, 1, MOD_TILE_N), lambda l, j: (l, 0, j)),
        ],
        out_specs=pl.BlockSpec((None, COND_ROWS, MOD_TILE_N), lambda l, j: (l, 0, j)),
        out_shape=jax.ShapeDtypeStruct((depth, COND_ROWS, n_out), F32),
        compiler_params=pltpu.CompilerParams(vmem_limit_bytes=VMEM_LIMIT),
        name="adaln_mod",
    )(cond, ada_w, ada_b.reshape(depth, 1, n_out))
    return out.reshape(depth, COND_ROWS, N_MOD, d_model)


def _rope(x, cos, sin_signed):
    lane = lax.broadcasted_iota(jnp.int32, x.shape, 1)
    quarter = HEAD_DIM // 4
    partner = jnp.where(lane % (2 * quarter) < quarter, pltpu.roll(x, HEAD_DIM - quarter, 1), pltpu.roll(x, quarter, 1))
    return x * cos + partner * sin_signed


def _hgrn_lower_bounds(logits, layer):
    e = jnp.exp(logits - jnp.max(logits, axis=0, keepdims=True))
    p = e / jnp.sum(e, axis=0, keepdims=True)
    cum = p[0]
    for i in range(1, layer + 1):
        cum = cum + p[i]
    return jnp.clip(cum - p[0], 0.0, 1.0)


def _in_proj_kernel(*refs, layer, use_rope):
    x_ref, mod_ref, gain_ref, w_ref, lbl_ref = refs[:5]
    refs = refs[5:]
    if use_rope:
        cos_ref, sin_ref, refs = refs[0], refs[1], refs[2:]
    p32_ref, p16_ref = refs

    mod = mod_ref[...]
    h = _modnorm(x_ref[...], gain_ref[...], mod[1:2], mod[0:1]).astype(BF16)
    lb = _hgrn_lower_bounds(lbl_ref[...], layer)

    def proj(j):
        return _dot(h, w_ref[:, j * SECTION : (j + 1) * SECTION])

    def put(ref, section, val):
        for hd in range(N_HEADS):
            ref[section * N_HEADS + hd] = val[:, hd * HEAD_DIM : (hd + 1) * HEAD_DIM].astype(ref.dtype)

    put(p32_ref, P32_QA, _silu(proj(0)))
    for d, (sec_hi, sec_lo, sec_k) in enumerate(
        ((P16_GF_HI, P16_GF_LO, P32_KF), (P16_GB_HI, P16_GB_LO, P32_KB))
    ):
        lb_d = lb[d : d + 1]
        c0, c1 = 0.5 + 0.5 * lb_d, 0.5 - 0.5 * lb_d
        ct = c1 * jnp.tanh(0.5 * proj(1 + d))
        g_hi, g_lo = _split_bf16(jnp.log(jnp.maximum(c0 + ct, GATE_FLOOR)))
        put(p16_ref, sec_hi, g_hi)
        put(p16_ref, sec_lo, g_lo)
        put(p32_ref, sec_k, c1 - ct)
    put(p16_ref, P16_VA, proj(3))
    put(p32_ref, P32_GATE_A, _silu(proj(4)))
    q = proj(5)
    k = proj(6) * (HEAD_DIM ** -0.5)
    if use_rope:
        cos, sin = cos_ref[...], sin_ref[...]
        heads = [slice(hd * HEAD_DIM, (hd + 1) * HEAD_DIM) for hd in range(N_HEADS)]
        q = jnp.concatenate([_rope(q[:, sl], cos, sin) for sl in heads], axis=1)
        k = jnp.concatenate([_rope(k[:, sl], cos, sin) for sl in heads], axis=1)
    put(p16_ref, P16_QR, q)
    put(p16_ref, P16_KR, k)
    put(p16_ref, P16_VR, proj(7))
    put(p32_ref, P32_GATE_R, _silu(proj(8)))


def _in_proj(x, mod, gain, w, lb_logits, rope_tables, layer, tile_m, cond_index, tiles_per_seq):
    n_tok, d_model = x.shape
    depth = lb_logits.shape[0]
    use_rope = rope_tables is not None
    in_specs = [
        pl.BlockSpec((tile_m, d_model), lambda i: (i, 0)),
        pl.BlockSpec((None, None, N_MOD, d_model), lambda i: (layer, cond_index(i), 0, 0)),
        pl.BlockSpec((None, 1, d_model), lambda i: (layer, 0, 0)),
        pl.BlockSpec((None, d_model, D_IN), lambda i: (layer, 0, 0), pipeline_mode=pl.Buffered(1)),
        pl.BlockSpec((depth, 2, SECTION), lambda i: (0, 0, 0)),
    ]
    args = [x, mod, gain, w, lb_logits]
    if use_rope:
        in_specs += [pl.BlockSpec((tile_m, HEAD_DIM), lambda i: (i % tiles_per_seq, 0))] * 2
        args += list(rope_tables)
    return pl.pallas_call(
        functools.partial(_in_proj_kernel, layer=layer, use_rope=use_rope),
        grid=(n_tok // tile_m,),
        in_specs=in_specs,
        out_specs=[
            pl.BlockSpec((N_P32, tile_m, HEAD_DIM), lambda i: (0, i, 0)),
            pl.BlockSpec((N_P16, tile_m, HEAD_DIM), lambda i: (0, i, 0)),
        ],
        out_shape=[
            jax.ShapeDtypeStruct((N_P32, n_tok, HEAD_DIM), F32),
            jax.ShapeDtypeStruct((N_P16, n_tok, HEAD_DIM), BF16),
        ],
        compiler_params=pltpu.CompilerParams(vmem_limit_bytes=VMEM_LIMIT),
        name="in_proj",
    )(*args)


def _rope_tables(seq_len):
    quarter = HEAD_DIM // 4
    inv = ROPE_BASE ** (-jnp.arange(quarter, dtype=F32) / quarter)
    pos = jnp.arange(seq_len, dtype=jnp.int32)
    ang_r = (pos // GRID_W).astype(F32)[:, None] * inv
    ang_c = (pos % GRID_W).astype(F32)[:, None] * inv
    cos = jnp.concatenate([jnp.cos(ang_r), jnp.cos(ang_r), jnp.cos(ang_c), jnp.cos(ang_c)], axis=1)
    sin = jnp.concatenate([-jnp.sin(ang_r), jnp.sin(ang_r), -jnp.sin(ang_c), jnp.sin(ang_c)], axis=1)
    return cos, sin


def _state_increment(v16, k_cat):
    return _dot(v16.astype(F32).T.astype(BF16), k_cat)


def _run_states(n_chunks, first, dec_of, u_scr, st_scr, s_f, s_b):
    def step(i, carry):
        s_f, s_b = carry
        cf, cb = first + i, first + n_chunks - 1 - i
        st_scr[cf, :, :HEAD_DIM] = s_f.astype(BF16)
        st_scr[cb, :, HEAD_DIM:] = s_b.astype(BF16)
        s_f = dec_of(cf)[:, :HEAD_DIM] * s_f + u_scr[cf, :, :HEAD_DIM]
        s_b = dec_of(cb)[:, HEAD_DIM:] * s_b + u_scr[cb, :, HEAD_DIM:]
        return s_f, s_b

    return lax.fori_loop(0, n_chunks, step, (s_f, s_b), unroll=2)


def _scan_states(n_chunks, n_seqs, dec_of, u_scr, st_scr, s0_ref, st_out_ref):
    per_seq = n_chunks // n_seqs
    for s in range(n_seqs):
        if s0_ref is None:
            s_f = s_b = jnp.zeros((HEAD_DIM, HEAD_DIM), F32)
        else:
            s_f, s_b = s0_ref[s, 0].T, s0_ref[s, 1].T
        s_f, s_b = _run_states(per_seq, s * per_seq, dec_of, u_scr, st_scr, s_f, s_b)
        if st_out_ref is not None:
            st_out_ref[s, 0] = s_f.T
            st_out_ref[s, 1] = s_b.T


def _slab_spec(seq_len, section):
    return pl.BlockSpec((None, None, seq_len, HEAD_DIM), lambda b, h: (section * N_HEADS + h, b, 0, 0))


def _state_in_spec(layer, n_seqs):
    return pl.BlockSpec((n_seqs, None, 2, None, HEAD_DIM, HEAD_DIM), lambda b, h: (b, layer, 0, h, 0, 0))


def _state_out_spec(n_seqs):
    return pl.BlockSpec((n_seqs, 2, None, HEAD_DIM, HEAD_DIM), lambda b, h: (b, 0, h, 0, 0))


def _fold_sequences(slabs, n_seqs):
    n, batch, seq_len, d = slabs.shape
    assert batch % n_seqs == 0
    return slabs.reshape(n, batch // n_seqs, n_seqs * seq_len, d)


def _mix_out(batch, seq_len):
    spec = pl.BlockSpec((None, None, seq_len, HEAD_DIM), lambda b, h: (h, b, 0, 0))
    return spec, jax.ShapeDtypeStruct((N_HEADS, batch, seq_len, HEAD_DIM), BF16)


def _cumsum_rows(g, reverse):
    n = g.shape[0]
    pos = lax.broadcasted_iota(jnp.int32, g.shape, 0)
    sh = 1
    while sh < n:
        if reverse:
            g = g + jnp.where(pos < n - sh, pltpu.roll(g, n - sh, 0), 0.0)
        else:
            g = g + jnp.where(pos >= sh, pltpu.roll(g, sh, 0), 0.0)
        sh *= 2
    return g


def _causal_mask(n, lower):
    t_idx = lax.broadcasted_iota(jnp.int32, (n, n), 0)
    s_idx = lax.broadcasted_iota(jnp.int32, (n, n), 1)
    return t_idx >= s_idx if lower else t_idx <= s_idx


def _hgrn_direct_scores(q, k, b, k_scr, b_scr):
    n = q.shape[0]
    k_scr[...] = k
    b_scr[...] = b
    col = lax.broadcasted_iota(jnp.int32, (n, n), 1)

    def body(s, acc):
        z = q * k_scr[pl.ds(s, 1), :] * jnp.exp(jnp.minimum(b - b_scr[pl.ds(s, 1), :], 0.0))
        return jnp.where(col == s, jnp.sum(z, axis=1, keepdims=True), acc)

    return lax.fori_loop(0, n, body, jnp.zeros((n, n), F32))


def _hgrn_kernel(*refs, seq_len, n_seqs, has_s0, emit_state):
    qa_ref, kf_ref, kb_ref, gate_ref, gf_hi_ref, gf_lo_ref, gb_hi_ref, gb_lo_ref, va_ref, gn_ref = refs[:10]
    refs = refs[10:]
    s0_ref = None
    if has_s0:
        s0_ref, refs = refs[0], refs[1:]
    mix_ref, refs = refs[0], refs[1:]
    st_out_ref = None
    if emit_state:
        st_out_ref, refs = refs[0], refs[1:]
    tri_scr, cum_scr, a16_scr, qe_scr, o_scr, u_scr, dec_scr, st_scr, safe_smem, k_scr, b_scr = refs

    c_len = HGRN_CHUNK
    blk_len = min(HGRN_BLOCK, seq_len)
    n_chunks = seq_len // c_len
    chunks_per_blk = blk_len // c_len
    directions = ((kf_ref, gf_hi_ref, gf_lo_ref, False), (kb_ref, gb_hi_ref, gb_lo_ref, True))
    causal = (_causal_mask(c_len, True), _causal_mask(c_len, False))

    t_idx = lax.broadcasted_iota(jnp.int32, (blk_len, blk_len), 0)
    s_idx = lax.broadcasted_iota(jnp.int32, (blk_len, blk_len), 1)
    same_chunk = (t_idx // c_len) == (s_idx // c_len)
    tri_scr[0] = jnp.where(jnp.logical_and(same_chunk, s_idx <= t_idx), 1.0, 0.0).astype(BF16)
    tri_scr[1] = jnp.where(jnp.logical_and(same_chunk, s_idx >= t_idx), 1.0, 0.0).astype(BF16)

    def cumulate(blk, carry):
        rows = pl.ds(pl.multiple_of(blk * blk_len, blk_len), blk_len)
        for d, (_, g_hi_ref, g_lo_ref, _) in enumerate(directions):
            cum = _dot(tri_scr[d], jnp.concatenate([g_hi_ref[rows, :], g_lo_ref[rows, :]], axis=1))
            cum_scr[d, rows, :] = cum[:, :HEAD_DIM] + cum[:, HEAD_DIM:]
        return carry

    lax.fori_loop(0, seq_len // blk_len, cumulate, 0, unroll=min(2, seq_len // blk_len))

    def prepare(blk, carry):
        r0 = pl.multiple_of(blk * blk_len, blk_len)
        rows = pl.ds(r0, blk_len)
        q = qa_ref[rows, :]
        v16 = va_ref[rows, :]
        q_ok = jnp.max(jnp.abs(q)) <= SAFE_Q_MAX
        gates = [(k_ref[rows, :], cum_scr[d, rows, :]) for d, (k_ref, _, _, _) in enumerate(directions)]
        for j in range(chunks_per_blk):
            sl = slice(j * c_len, (j + 1) * c_len)
            rows_j = pl.ds(r0 + j * c_len, c_len)
            c = blk * chunks_per_blk + j
            q_e, k_e, dec, span, scores = [], [], [], None, None
            for d, (k, b) in enumerate(gates):
                k_j, b_j = k[sl], b[sl]
                tot = b_j[0:1] if directions[d][3] else b_j[c_len - 1 : c_len]
                dist = b_j - b_j[c_len // 2 - 1 : c_len // 2]
                a = _dot_nt((q[sl] * jnp.exp(dist)).astype(BF16), (k_j * jnp.exp(-dist)).astype(BF16))
                a = jnp.where(causal[d], a, 0.0)
                scores = a if scores is None else scores + a
                q_e.append(q[sl] * jnp.exp(b_j))
                k_e.append(k_j * jnp.exp(tot - b_j))
                dec.append(jnp.exp(tot))
                ends = jnp.maximum(jnp.abs(dist[0:1]), jnp.abs(dist[c_len - 1 : c_len]))
                span = ends if span is None else jnp.maximum(span, ends)
            a16_scr[rows_j, :] = scores.astype(BF16)
            qe_scr[rows_j, :] = jnp.concatenate(q_e, axis=1).astype(BF16)
            u_scr[c] = _state_increment(v16[sl], jnp.concatenate(k_e, axis=1).astype(BF16))
            dec_scr[c] = jnp.concatenate(dec, axis=1)
            safe_smem[c] = jnp.logical_and(q_ok, jnp.max(span) <= SAFE_EXP_RANGE).astype(jnp.int32)
        return carry

    lax.fori_loop(0, seq_len // blk_len, prepare, 0, unroll=min(2, seq_len // blk_len))

    _scan_states(n_chunks, n_seqs, lambda c: dec_scr[c], u_scr, st_scr, s0_ref, st_out_ref)

    def redo_scores(c, carry):
        @pl.when(safe_smem[c] == 0)
        def _():
            rows = pl.ds(pl.multiple_of(c * c_len, c_len), c_len)
            q = qa_ref[rows, :]
            scores = jnp.zeros((c_len, c_len), F32)
            for d, (k_ref, g_hi_ref, g_lo_ref, rev) in enumerate(directions):
                b = _cumsum_rows(g_hi_ref[rows, :].astype(F32) + g_lo_ref[rows, :].astype(F32), rev)
                scores = scores + jnp.where(causal[d], _hgrn_direct_scores(q, k_ref[rows, :], b, k_scr, b_scr), 0.0)
            a16_scr[rows, :] = scores.astype(BF16)

        return carry

    lax.fori_loop(0, n_chunks, redo_scores, 0)

    def combine(c, carry):
        rows = pl.ds(pl.multiple_of(c * c_len, c_len), c_len)
        o_scr[rows, :] = _dot(a16_scr[rows, :], va_ref[rows, :]) + _dot_nt(qe_scr[rows, :], st_scr[c])
        return carry

    lax.fori_loop(0, n_chunks, combine, 0, unroll=min(8, n_chunks))

    gn = gn_ref[...]

    def head_norm_gate(blk, carry):
        rows = pl.ds(pl.multiple_of(blk * blk_len, blk_len), blk_len)
        o = o_scr[rows, :]
        o = o * lax.rsqrt(jnp.mean(o * o, axis=-1, keepdims=True) + EPS) * gn
        mix_ref[rows, :] = (o * gate_ref[rows, :]).astype(BF16)
        return carry

    lax.fori_loop(0, seq_len // blk_len, head_norm_gate, 0, unroll=min(4, seq_len // blk_len))


def _hgrn(p32, p16, norm_gain, state, layer, emit_state, n_seqs):
    p32, p16 = _fold_sequences(p32, n_seqs), _fold_sequences(p16, n_seqs)
    _, batch, seq_len, _ = p32.shape
    n_chunks = seq_len // HGRN_CHUNK
    blk_len = min(HGRN_BLOCK, seq_len)
    has_s0 = state is not None
    f32_slabs = (P32_QA, P32_KF, P32_KB, P32_GATE_A)
    bf16_slabs = (P16_GF_HI, P16_GF_LO, P16_GB_HI, P16_GB_LO, P16_VA)
    in_specs = [_slab_spec(seq_len, s) for s in f32_slabs + bf16_slabs]
    in_specs.append(pl.BlockSpec((None, 1, HEAD_DIM), lambda b, h: (layer, 0, h)))
    args = [p32] * len(f32_slabs) + [p16] * len(bf16_slabs) + [norm_gain]
    if has_s0:
        in_specs.append(_state_in_spec(layer, n_seqs))
        args.append(state)
    mix_spec, mix_shape = _mix_out(batch, seq_len)
    out_specs, out_shape = [mix_spec], [mix_shape]
    if emit_state:
        out_specs.append(_state_out_spec(n_seqs))
        out_shape.append(jax.ShapeDtypeStruct((batch * n_seqs, 2, N_HEADS, HEAD_DIM, HEAD_DIM), F32))
    outs = pl.pallas_call(
        functools.partial(_hgrn_kernel, seq_len=seq_len, n_seqs=n_seqs, has_s0=has_s0, emit_state=emit_state),
        grid=(batch, N_HEADS),
        in_specs=in_specs,
        out_specs=out_specs,
        out_shape=out_shape,
        scratch_shapes=[
            pltpu.VMEM((2, blk_len, blk_len), BF16),
            pltpu.VMEM((2, seq_len, HEAD_DIM), F32),
            pltpu.VMEM((seq_len, HGRN_CHUNK), BF16),
            pltpu.VMEM((seq_len, 2 * HEAD_DIM), BF16),
            pltpu.VMEM((seq_len, HEAD_DIM), F32),
            pltpu.VMEM((n_chunks, HEAD_DIM, 2 * HEAD_DIM), F32),
            pltpu.VMEM((n_chunks, 1, 2 * HEAD_DIM), F32),
            pltpu.VMEM((n_chunks, HEAD_DIM, 2 * HEAD_DIM), BF16),
            pltpu.SMEM((n_chunks,), jnp.int32),
            pltpu.VMEM((HGRN_CHUNK, HEAD_DIM), F32),
            pltpu.VMEM((HGRN_CHUNK, HEAD_DIM), F32),
        ],
        compiler_params=pltpu.CompilerParams(vmem_limit_bytes=VMEM_LIMIT),
        name="hgrn_scan",
    )(*args)
    return (outs[0], outs[1]) if emit_state else (outs[0], None)


def _log_sigmoid(x):
    return jnp.minimum(x, 0.0) - jnp.log1p(jnp.exp(-jnp.abs(x)))


def _ret_kernel(*refs, layer, seq_len, n_seqs, has_s0, emit_state):
    logit_ref, q_ref, k_ref, v_ref, gate_ref, gn_ref = refs[:6]
    refs = refs[6:]
    s0_ref = None
    if has_s0:
        s0_ref, refs = refs[0], refs[1:]
    mix_ref, refs = refs[0], refs[1:]
    st_out_ref = None
    if emit_state:
        st_out_ref, refs = refs[0], refs[1:]
    a16_scr, o_scr, u_scr, st_scr = refs

    c_len = RET_CHUNK
    n_chunks = seq_len // c_len
    head = pl.program_id(1)
    lg_f = _log_sigmoid(jnp.full((1, HEAD_DIM), logit_ref[layer, 0, head], F32))
    lg_b = _log_sigmoid(jnp.full((1, HEAD_DIM), logit_ref[layer, 1, head], F32))

    t = lax.broadcasted_iota(jnp.int32, (c_len, HEAD_DIM), 0).astype(F32)
    q_dec_f = jnp.exp((t + 1.0) * lg_f)
    q_dec_b = jnp.exp((c_len - t) * lg_b)
    k_dec = jnp.concatenate([jnp.exp((c_len - 1.0 - t) * lg_f), jnp.exp(t * lg_b)], axis=1)
    chunk_dec = jnp.concatenate([jnp.exp(c_len * lg_f), jnp.exp(c_len * lg_b)], axis=1)
    t_idx = lax.broadcasted_iota(jnp.int32, (c_len, c_len), 0)
    s_idx = lax.broadcasted_iota(jnp.int32, (c_len, c_len), 1)
    dist = (t_idx - s_idx).astype(F32)
    decay = jnp.where(t_idx >= s_idx, jnp.exp(jnp.maximum(dist, 0.0) * lg_f), 0.0)
    decay = decay + jnp.where(t_idx <= s_idx, jnp.exp(jnp.maximum(-dist, 0.0) * lg_b), 0.0)

    def prepare(c, carry):
        rows = pl.ds(pl.multiple_of(c * c_len, c_len), c_len)
        k16 = k_ref[rows, :]
        a16_scr[rows, :] = (_dot_nt(q_ref[rows, :], k16) * decay).astype(BF16)
        k = k16.astype(F32)
        u_scr[c] = _state_increment(v_ref[rows, :], (jnp.concatenate([k, k], axis=1) * k_dec).astype(BF16))
        return carry

    lax.fori_loop(0, n_chunks, prepare, 0, unroll=min(4, n_chunks))

    _scan_states(n_chunks, n_seqs, lambda c: chunk_dec, u_scr, st_scr, s0_ref, st_out_ref)

    def combine(c, carry):
        rows = pl.ds(pl.multiple_of(c * c_len, c_len), c_len)
        q16 = q_ref[rows, :]
        o = _dot(a16_scr[rows, :], v_ref[rows, :])
        o = o + q_dec_f * _dot_nt(q16, st_scr[c, :, :HEAD_DIM]) + q_dec_b * _dot_nt(q16, st_scr[c, :, HEAD_DIM:])
        o_scr[rows, :] = o
        return carry

    lax.fori_loop(0, n_chunks, combine, 0, unroll=min(8, n_chunks))

    gn = gn_ref[...]
    blk_len = min(RET_BLOCK, seq_len)

    def head_norm_gate(blk, carry):
        rows = pl.ds(pl.multiple_of(blk * blk_len, blk_len), blk_len)
        o = o_scr[rows, :]
        o = o - jnp.mean(o, axis=-1, keepdims=True)
        o = o * lax.rsqrt(jnp.mean(o * o, axis=-1, keepdims=True) + EPS) * gn
        mix_ref[rows, :] = (o * gate_ref[rows, :]).astype(BF16)
        return carry

    lax.fori_loop(0, seq_len // blk_len, head_norm_gate, 0, unroll=min(4, seq_len // blk_len))


def _ret(p32, p16, decay_logit, norm_gain, state, layer, emit_state, n_seqs):
    p32, p16 = _fold_sequences(p32, n_seqs), _fold_sequences(p16, n_seqs)
    _, batch, seq_len, _ = p32.shape
    n_chunks = seq_len // RET_CHUNK
    has_s0 = state is not None
    in_specs = [pl.BlockSpec(memory_space=pltpu.SMEM)]
    in_specs += [_slab_spec(seq_len, s) for s in (P16_QR, P16_KR, P16_VR, P32_GATE_R)]
    in_specs.append(pl.BlockSpec((None, 1, HEAD_DIM), lambda b, h: (layer, 0, h)))
    args = [decay_logit, p16, p16, p16, p32, norm_gain]
    if has_s0:
        in_specs.append(_state_in_spec(layer, n_seqs))
        args.append(state)
    mix_spec, mix_shape = _mix_out(batch, seq_len)
    out_specs, out_shape = [mix_spec], [mix_shape]
    if emit_state:
        out_specs.append(_state_out_spec(n_seqs))
        out_shape.append(jax.ShapeDtypeStruct((batch * n_seqs, 2, N_HEADS, HEAD_DIM, HEAD_DIM), F32))
    outs = pl.pallas_call(
        functools.partial(
            _ret_kernel, layer=layer, seq_len=seq_len, n_seqs=n_seqs, has_s0=has_s0, emit_state=emit_state
        ),
        grid=(batch, N_HEADS),
        in_specs=in_specs,
        out_specs=out_specs,
        out_shape=out_shape,
        scratch_shapes=[
            pltpu.VMEM((seq_len, RET_CHUNK), BF16),
            pltpu.VMEM((seq_len, HEAD_DIM), F32),
            pltpu.VMEM((n_chunks, HEAD_DIM, 2 * HEAD_DIM), F32),
            pltpu.VMEM((n_chunks, HEAD_DIM, 2 * HEAD_DIM), BF16),
        ],
        compiler_params=pltpu.CompilerParams(vmem_limit_bytes=VMEM_LIMIT),
        name="ret_scan",
    )(*args)
    return (outs[0], outs[1]) if emit_state else (outs[0], None)


def _out_ffn_kernel(*refs, d_ff, final):
    ma_ref, mb_ref, x_ref, mod_ref, gain_ref, wo_ref, wu_ref, wd_ref = refs[:8]
    refs = refs[8:]
    if final:
        gfin_ref, refs = refs[0], refs[1:]
    out_ref, act_scr = refs

    mod = mod_ref[...]
    mix = jnp.concatenate([ma_ref[hd] for hd in range(N_HEADS)] + [mb_ref[hd] for hd in range(N_HEADS)], axis=1)
    x1 = x_ref[...] + mod[2:3] * _dot(mix, wo_ref[...])
    h = _modnorm(x1, gain_ref[...], mod[4:5], mod[3:4]).astype(BF16)
    for c in range(d_ff // FFN_CHUNK):
        lo = c * FFN_CHUNK
        gate = _dot(h, wu_ref[:, lo : lo + FFN_CHUNK])
        up = _dot(h, wu_ref[:, d_ff + lo : d_ff + lo + FFN_CHUNK])
        act_scr[:, lo : lo + FFN_CHUNK] = (_silu(gate) * up).astype(BF16)
    x2 = x1 + mod[5:6] * _dot(act_scr[...], wd_ref[...])
    if final:
        x2 = x2 * lax.rsqrt(jnp.mean(x2 * x2, axis=-1, keepdims=True) + EPS) * gfin_ref[...]
    out_ref[...] = x2


def _out_ffn(mix_a, mix_b, x, mod, gain, w_out, w_up, w_down, final_gain, layer, tile_m, cond_index):
    n_tok, d_model = x.shape
    d_ff = w_down.shape[1]
    final = final_gain is not None
    const = lambda i: (layer, 0, 0)
    in_specs = [
        pl.BlockSpec((N_HEADS, tile_m, HEAD_DIM), lambda i: (0, i, 0)),
        pl.BlockSpec((N_HEADS, tile_m, HEAD_DIM), lambda i: (0, i, 0)),
        pl.BlockSpec((tile_m, d_model), lambda i: (i, 0)),
        pl.BlockSpec((None, None, N_MOD, d_model), lambda i: (layer, cond_index(i), 0, 0)),
        pl.BlockSpec((None, 1, d_model), const),
        pl.BlockSpec((None, 2 * N_HEADS * HEAD_DIM, d_model), const, pipeline_mode=pl.Buffered(1)),
        pl.BlockSpec((None, d_model, 2 * d_ff), const, pipeline_mode=pl.Buffered(1)),
        pl.BlockSpec((None, d_ff, d_model), const, pipeline_mode=pl.Buffered(1)),
    ]
    args = [mix_a, mix_b, x, mod, gain, w_out, w_up, w_down]
    if final:
        in_specs.append(pl.BlockSpec((1, d_model), lambda i: (0, 0)))
        args.append(final_gain)
    return pl.pallas_call(
        functools.partial(_out_ffn_kernel, d_ff=d_ff, final=final),
        grid=(n_tok // tile_m,),
        in_specs=in_specs,
        out_specs=pl.BlockSpec((tile_m, d_model), lambda i: (i, 0)),
        out_shape=jax.ShapeDtypeStruct((n_tok, d_model), F32),
        scratch_shapes=[pltpu.VMEM((tile_m, d_ff), BF16)],
        compiler_params=pltpu.CompilerParams(vmem_limit_bytes=VMEM_LIMIT),
        name="out_ffn",
    )(*args)


def kernel(x_prompt, x_sample, state_hgrn, state_ret, c, c_ctx, ada_w, ada_b, norm_mix, norm_ffn, w_in,
           hgrn_lb_logits, hgrn_norm, ret_decay_logit, ret_norm, w_out, w_up, w_down, norm_final):
    batch, seq, d_model = x_prompt.shape
    dec_batch, dec_seq, _ = x_sample.shape
    depth = ada_w.shape[0]
    tile_m = 512
    assert w_in.shape[2] == D_IN and state_hgrn.shape[3] == N_HEADS and state_ret.shape[3] == N_HEADS
    assert dec_batch < COND_ROWS and dec_seq % tile_m == 0 and (batch * seq) % tile_m == 0
    assert dec_seq % GRID_W == 0
    for n_l in (seq, dec_seq):
        assert n_l % min(HGRN_BLOCK, n_l) == 0 and min(HGRN_BLOCK, n_l) % HGRN_CHUNK == 0
        assert n_l % min(RET_BLOCK, n_l) == 0 and n_l % RET_CHUNK == 0

    cond = jnp.concatenate(
        [c.astype(F32), c_ctx.astype(F32)[None, :], jnp.zeros((COND_ROWS - dec_batch - 1, d_model), F32)], axis=0
    )
    mod = _modulation(cond, ada_w, ada_b)

    w_in16, w_out16, w_up16, w_down16 = (w.astype(BF16) for w in (w_in, w_out, w_up, w_down))
    gain_mix = norm_mix.reshape(depth, 1, d_model)
    gain_ffn = norm_ffn.reshape(depth, 1, d_model)
    gain_hgrn = hgrn_norm.reshape(depth, 1, -1)
    gain_ret = ret_norm.reshape(depth, 1, -1)
    gain_final = norm_final.reshape(1, d_model)
    rope_tables = _rope_tables(dec_seq)

    tiles_per_seq = dec_seq // tile_m
    groups = [
        [x_prompt.astype(F32).reshape(batch * seq, d_model), batch, seq, lambda i: dec_batch, None, None, None, True],
        [x_sample.astype(F32).reshape(dec_batch * dec_seq, d_model), dec_batch, dec_seq,
         lambda i: i // tiles_per_seq, state_hgrn, state_ret, rope_tables, False],
    ]
    new_h, new_r = [], []
    for layer in range(depth):
        final_gain = gain_final if layer == depth - 1 else None
        for grp in groups:
            x, n_b, n_l, cond_index, s0_h, s0_r, tables, emit = grp
            p32, p16 = _in_proj(x, mod, gain_mix, w_in16, hgrn_lb_logits, tables, layer, tile_m, cond_index, tiles_per_seq)
            p32 = p32.reshape(N_P32, n_b, n_l, HEAD_DIM)
            p16 = p16.reshape(N_P16, n_b, n_l, HEAD_DIM)
            n_seqs = max(1, min(n_b, SCAN_ROWS_PER_STEP // n_l))
            mix_a, st_a = _hgrn(p32, p16, gain_hgrn, s0_h, layer, emit, n_seqs)
            mix_b, st_b = _ret(p32, p16, ret_decay_logit, gain_ret, s0_r, layer, emit, n_seqs)
            grp[0] = _out_ffn(
                mix_a.reshape(N_HEADS, n_b * n_l, HEAD_DIM), mix_b.reshape(N_HEADS, n_b * n_l, HEAD_DIM), x, mod,
                gain_ffn, w_out16, w_up16, w_down16, final_gain, layer, tile_m, cond_index,
            )
            if emit:
                new_h.append(st_a)
                new_r.append(st_b)
    y_prompt = groups[0][0].reshape(batch, seq, d_model).astype(x_prompt.dtype)
    y_sample = groups[1][0].reshape(dec_batch, dec_seq, d_model).astype(x_sample.dtype)
    new_state_hgrn = jnp.stack(new_h, axis=1).astype(state_hgrn.dtype)
    new_state_ret = jnp.stack(new_r, axis=1).astype(state_ret.dtype)
    return (y_prompt, y_sample, new_state_hgrn, new_state_ret)
```

```python
import functools

import jax
import jax.numpy as jnp
from jax import lax
from jax.experimental import pallas as pl
from jax.experimental.pallas import tpu as pltpu

F32 = jnp.float32
BF16 = jnp.bfloat16

N_HEADS = 4
HEAD_DIM = 128
N_SECTIONS = 9
SECTION = N_HEADS * HEAD_DIM
D_IN = N_SECTIONS * SECTION
GRID_W = 64
ROPE_BASE = 10000.0
EPS = 1e-6
GATE_FLOOR = 1e-12

P32_QA, P32_KF, P32_KB, P32_GATE_A, P32_GATE_R = range(5)
P16_GF_HI, P16_GF_LO, P16_GB_HI, P16_GB_LO, P16_VA, P16_QR, P16_KR, P16_VR = range(8)
N_P32 = 5 * N_HEADS
N_P16 = 8 * N_HEADS

N_MOD = 6
COND_ROWS = 16

HGRN_CHUNK = 64
HGRN_BLOCK = 256
RET_CHUNK = 128
RET_BLOCK = 256
SCAN_ROWS_PER_STEP = 1024
SAFE_EXP_RANGE = 80.0
SAFE_Q_MAX = 1e3
FFN_CHUNK = 256
MOD_TILE_N = 1536
VMEM_LIMIT = 56 * 1024 * 1024


def _silu(x):
    h = 0.5 * x
    return h + h * jnp.tanh(h)


def _dot(a, b):
    return jnp.dot(a, b, preferred_element_type=F32)


def _dot_nt(a, b):
    return lax.dot_general(a, b, (((1,), (1,)), ((), ())), preferred_element_type=F32)


def _split_bf16(x):
    hi = x.astype(BF16)
    return hi, (x - hi.astype(F32)).astype(BF16)


def _modnorm(x, gain, scale, shift):
    ms = jnp.mean(x * x, axis=-1, keepdims=True)
    return x * lax.rsqrt(ms + EPS) * gain * (1.0 + scale) + shift


def _mod_kernel(cond_ref, w_ref, b_ref, out_ref):
    s_hi, s_lo = _split_bf16(_silu(cond_ref[...]))
    w_hi, w_lo = _split_bf16(w_ref[...])
    s_both = jnp.concatenate([s_hi, s_lo], axis=0)
    p = _dot(s_both, w_hi) + _dot(s_both, w_lo)
    out_ref[...] = p[:COND_ROWS] + p[COND_ROWS:] + b_ref[...]


def _modulation(cond, ada_w, ada_b):
    depth, d_model, n_out = ada_w.shape
    out = pl.pallas_call(
        _mod_kernel,
        grid=(depth, n_out // MOD_TILE_N),
        in_specs=[
            pl.BlockSpec((COND_ROWS, d_model), lambda l, j: (0, 0)),
            pl.BlockSpec((None, d_model, MOD_TILE_N), lambda l, j: (l, 0, j)),
            pl.BlockSpec((None, 1, MOD_TILE_N), lambda l, j: (l, 0, j)),
        ],
        out_specs=pl.BlockSpec((None, COND_ROWS, MOD_TILE_N), lambda l, j: (l, 0, j)),
        out_shape=jax.ShapeDtypeStruct((depth, COND_ROWS, n_out), F32),
        compiler_params=pltpu.CompilerParams(vmem_limit_bytes=VMEM_LIMIT),
        name="adaln_mod",
    )(cond, ada_w, ada_b.reshape(depth, 1, n_out))
    return out.reshape(depth, COND_ROWS, N_MOD, d_model)


def _rope(x, cos, sin_signed):
    lane = lax.broadcasted_iota(jnp.int32, x.shape, 1)
    quarter = HEAD_DIM // 4
    partner = jnp.where(lane % (2 * quarter) < quarter, pltpu.roll(x, HEAD_DIM - quarter, 1), pltpu.roll(x, quarter, 1))
    return x * cos + partner * sin_signed


def _hgrn_lower_bounds(logits, layer):
    e = jnp.exp(logits - jnp.max(logits, axis=0, keepdims=True))
    p = e / jnp.sum(e, axis=0, keepdims=True)
    cum = p[0]
    for i in range(1, layer + 1):
        cum = cum + p[i]
    return jnp.clip(cum - p[0], 0.0, 1.0)


def _in_proj_kernel(*refs, layer, use_rope):
    x_ref, mod_ref, gain_ref, w_ref, lbl_ref = refs[:5]
    refs = refs[5:]
    if use_rope:
        cos_ref, sin_ref, refs = refs[0], refs[1], refs[2:]
    p32_ref, p16_ref = refs

    mod = mod_ref[...]
    h = _modnorm(x_ref[...], gain_ref[...], mod[1:2], mod[0:1]).astype(BF16)
    lb = _hgrn_lower_bounds(lbl_ref[...], layer)

    def proj(j):
        return _dot(h, w_ref[:, j * SECTION : (j + 1) * SECTION])

    def put(ref, section, val):
        for hd in range(N_HEADS):
            ref[section * N_HEADS + hd] = val[:, hd * HEAD_DIM : (hd + 1) * HEAD_DIM].astype(ref.dtype)

    put(p32_ref, P32_QA, _silu(proj(0)))
    for d, (sec_hi, sec_lo, sec_k) in enumerate(
        ((P16_GF_HI, P16_GF_LO, P32_KF), (P16_GB_HI, P16_GB_LO, P32_KB))
    ):
        lb_d = lb[d : d + 1]
        c0, c1 = 0.5 + 0.5 * lb_d, 0.5 - 0.5 * lb_d
        ct = c1 * jnp.tanh(0.5 * proj(1 + d))
        g_hi, g_lo = _split_bf16(jnp.log(jnp.maximum(c0 + ct, GATE_FLOOR)))
        put(p16_ref, sec_hi, g_hi)
        put(p16_ref, sec_lo, g_lo)
        put(p32_ref, sec_k, c1 - ct)
    put(p16_ref, P16_VA, proj(3))
    put(p32_ref, P32_GATE_A, _silu(proj(4)))
    q = proj(5)
    k = proj(6) * (HEAD_DIM ** -0.5)
    if use_rope:
        cos, sin = cos_ref[...], sin_ref[...]
        heads = [slice(hd * HEAD_DIM, (hd + 1) * HEAD_DIM) for hd in range(N_HEADS)]
        q = jnp.concatenate([_rope(q[:, sl], cos, sin) for sl in heads], axis=1)
        k = jnp.concatenate([_rope(k[:, sl], cos, sin) for sl in heads], axis=1)
    put(p16_ref, P16_QR, q)
    put(p16_ref, P16_KR, k)
    put(p16_ref, P16_VR, proj(7))
    put(p32_ref, P32_GATE_R, _silu(proj(8)))


def _in_proj(x, mod, gain, w, lb_logits, rope_tables, layer, tile_m, cond_index, tiles_per_seq):
    n_tok, d_model = x.shape
    depth = lb_logits.shape[0]
    use_rope = rope_tables is not None
    in_specs = [
        pl.BlockSpec((tile_m, d_model), lambda i: (i, 0)),
        pl.BlockSpec((None, None, N_MOD, d_model), lambda i: (layer, cond_index(i), 0, 0)),
        pl.BlockSpec((None, 1, d_model), lambda i: (layer, 0, 0)),
        pl.BlockSpec((None, d_model, D_IN), lambda i: (layer, 0, 0), pipeline_mode=pl.Buffered(1)),
        pl.BlockSpec((depth, 2, SECTION), lambda i: (0, 0, 0)),
    ]
    args = [x, mod, gain, w, lb_logits]
    if use_rope:
        in_specs += [pl.BlockSpec((tile_m, HEAD_DIM), lambda i: (i % tiles_per_seq, 0))] * 2
        args += list(rope_tables)
    return pl.pallas_call(
        functools.partial(_in_proj_kernel, layer=layer, use_rope=use_rope),
        grid=(n_tok // tile_m,),
        in_specs=in_specs,
        out_specs=[
            pl.BlockSpec((N_P32, tile_m, HEAD_DIM), lambda i: (0, i, 0)),
            pl.BlockSpec((N_P16, tile_m, HEAD_DIM), lambda i: (0, i, 0)),
        ],
        out_shape=[
            jax.ShapeDtypeStruct((N_P32, n_tok, HEAD_DIM), F32),
            jax.ShapeDtypeStruct((N_P16, n_tok, HEAD_DIM), BF16),
        ],
        compiler_params=pltpu.CompilerParams(vmem_limit_bytes=VMEM_LIMIT),
        name="in_proj",
    )(*args)


def _rope_tables(seq_len):
    quarter = HEAD_DIM // 4
    inv = ROPE_BASE ** (-jnp.arange(quarter, dtype=F32) / quarter)
    pos = jnp.arange(seq_len, dtype=jnp.int32)
    ang_r = (pos // GRID_W).astype(F32)[:, None] * inv
    ang_c = (pos % GRID_W).astype(F32)[:, None] * inv
    cos = jnp.concatenate([jnp.cos(ang_r), jnp.cos(ang_r), jnp.cos(ang_c), jnp.cos(ang_c)], axis=1)
    sin = jnp.concatenate([-jnp.sin(ang_r), jnp.sin(ang_r), -jnp.sin(ang_c), jnp.sin(ang_c)], axis=1)
    return cos, sin


def _skewed_stages(n_steps, produce, consume):
    assert n_steps % 2 == 0
    produce(0, 0)

    def body(pair, carry):
        i = 2 * pair
        produce(i + 1, 1)
        consume(i, 0)
        produce(jnp.minimum(i + 2, n_steps - 1), 0)
        consume(i + 1, 1)
        return carry

    lax.fori_loop(0, n_steps // 2, body, 0)


def _state_increment(v16, k_cat):
    return _dot(v16.astype(F32).T.astype(BF16), k_cat)


def _run_states(n_chunks, first, dec_of, u_scr, st_scr, s_f, s_b):
    def step(i, carry):
        s_f, s_b = carry
        cf, cb = first + i, first + n_chunks - 1 - i
        st_scr[cf, :, :HEAD_DIM] = s_f.astype(BF16)
        st_scr[cb, :, HEAD_DIM:] = s_b.astype(BF16)
        s_f = dec_of(cf)[:, :HEAD_DIM] * s_f + u_scr[cf, :, :HEAD_DIM]
        s_b = dec_of(cb)[:, HEAD_DIM:] * s_b + u_scr[cb, :, HEAD_DIM:]
        return s_f, s_b

    return lax.fori_loop(0, n_chunks, step, (s_f, s_b), unroll=2)


def _scan_states(n_chunks, n_seqs, dec_of, u_scr, st_scr, s0_ref, st_out_ref):
    per_seq = n_chunks // n_seqs
    for s in range(n_seqs):
        if s0_ref is None:
            s_f = s_b = jnp.zeros((HEAD_DIM, HEAD_DIM), F32)
        else:
            s_f, s_b = s0_ref[s, 0].T, s0_ref[s, 1].T
        s_f, s_b = _run_states(per_seq, s * per_seq, dec_of, u_scr, st_scr, s_f, s_b)
        if st_out_ref is not None:
            st_out_ref[s, 0] = s_f.T
            st_out_ref[s, 1] = s_b.T


def _slab_spec(seq_len, section):
    return pl.BlockSpec((None, None, seq_len, HEAD_DIM), lambda b, h: (section * N_HEADS + h, b, 0, 0))


def _state_in_spec(layer, n_seqs):
    return pl.BlockSpec((n_seqs, None, 2, None, HEAD_DIM, HEAD_DIM), lambda b, h: (b, layer, 0, h, 0, 0))


def _state_out_spec(n_seqs):
    return pl.BlockSpec((n_seqs, 2, None, HEAD_DIM, HEAD_DIM), lambda b, h: (b, 0, h, 0, 0))


def _fold_sequences(slabs, n_seqs):
    n, batch, seq_len, d = slabs.shape
    assert batch % n_seqs == 0
    return slabs.reshape(n, batch // n_seqs, n_seqs * seq_len, d)


def _mix_out(batch, seq_len):
    spec = pl.BlockSpec((None, None, seq_len, HEAD_DIM), lambda b, h: (h, b, 0, 0))
    return spec, jax.ShapeDtypeStruct((N_HEADS, batch, seq_len, HEAD_DIM), BF16)


def _cumsum_rows(g, reverse):
    n = g.shape[0]
    pos = lax.broadcasted_iota(jnp.int32, g.shape, 0)
    sh = 1
    while sh < n:
        if reverse:
            g = g + jnp.where(pos < n - sh, pltpu.roll(g, n - sh, 0), 0.0)
        else:
            g = g + jnp.where(pos >= sh, pltpu.roll(g, sh, 0), 0.0)
        sh *= 2
    return g


def _causal_mask(n, lower):
    t_idx = lax.broadcasted_iota(jnp.int32, (n, n), 0)
    s_idx = lax.broadcasted_iota(jnp.int32, (n, n), 1)
    return t_idx >= s_idx if lower else t_idx <= s_idx


def _hgrn_direct_scores(q, k, b, k_scr, b_scr):
    n = q.shape[0]
    k_scr[...] = k
    b_scr[...] = b
    col = lax.broadcasted_iota(jnp.int32, (n, n), 1)

    def body(s, acc):
        z = q * k_scr[pl.ds(s, 1), :] * jnp.exp(jnp.minimum(b - b_scr[pl.ds(s, 1), :], 0.0))
        return jnp.where(col == s, jnp.sum(z, axis=1, keepdims=True), acc)

    return lax.fori_loop(0, n, body, jnp.zeros((n, n), F32))


def _hgrn_kernel(*refs, seq_len, n_seqs, has_s0, emit_state):
    qa_ref, kf_ref, kb_ref, gate_ref, gf_hi_ref, gf_lo_ref, gb_hi_ref, gb_lo_ref, va_ref, gn_ref = refs[:10]
    refs = refs[10:]
    s0_ref = None
    if has_s0:
        s0_ref, refs = refs[0], refs[1:]
    mix_ref, refs = refs[0], refs[1:]
    st_out_ref = None
    if emit_state:
        st_out_ref, refs = refs[0], refs[1:]
    tri_scr, cum_scr, a16_scr, qe_scr, o_scr, u_scr, dec_scr, st_scr, safe_smem, k_scr, b_scr = refs

    c_len = HGRN_CHUNK
    blk_len = min(HGRN_BLOCK, seq_len)
    n_chunks = seq_len // c_len
    n_blocks = seq_len // blk_len
    chunks_per_blk = blk_len // c_len
    directions = ((kf_ref, gf_hi_ref, gf_lo_ref, False), (kb_ref, gb_hi_ref, gb_lo_ref, True))
    causal = (_causal_mask(c_len, True), _causal_mask(c_len, False))

    t_idx = lax.broadcasted_iota(jnp.int32, (blk_len, blk_len), 0)
    s_idx = lax.broadcasted_iota(jnp.int32, (blk_len, blk_len), 1)
    same_chunk = (t_idx // c_len) == (s_idx // c_len)
    tri_scr[0] = jnp.where(jnp.logical_and(same_chunk, s_idx <= t_idx), 1.0, 0.0).astype(BF16)
    tri_scr[1] = jnp.where(jnp.logical_and(same_chunk, s_idx >= t_idx), 1.0, 0.0).astype(BF16)

    def cumulate(blk, slot):
        rows = pl.ds(pl.multiple_of(blk * blk_len, blk_len), blk_len)
        for d, (_, g_hi_ref, g_lo_ref, _) in enumerate(directions):
            cum = _dot(tri_scr[d], jnp.concatenate([g_hi_ref[rows, :], g_lo_ref[rows, :]], axis=1))
            cum_scr[slot, d] = cum[:, :HEAD_DIM] + cum[:, HEAD_DIM:]

    def prepare(blk, slot):
        r0 = pl.multiple_of(blk * blk_len, blk_len)
        rows = pl.ds(r0, blk_len)
        q = qa_ref[rows, :]
        v16 = va_ref[rows, :]
        q_ok = jnp.max(jnp.abs(q)) <= SAFE_Q_MAX
        gates = [(k_ref[rows, :], cum_scr[slot, d]) for d, (k_ref, _, _, _) in enumerate(directions)]
        for j in range(chunks_per_blk):
            sl = slice(j * c_len, (j + 1) * c_len)
            rows_j = pl.ds(r0 + j * c_len, c_len)
            c = blk * chunks_per_blk + j
            q_e, k_e, dec, span, scores = [], [], [], None, None
            for d, (k, b) in enumerate(gates):
                k_j, b_j = k[sl], b[sl]
                tot = b_j[0:1] if directions[d][3] else b_j[c_len - 1 : c_len]
                dist = b_j - b_j[c_len // 2 - 1 : c_len // 2]
                a = _dot_nt((q[sl] * jnp.exp(dist)).astype(BF16), (k_j * jnp.exp(-dist)).astype(BF16))
                a = jnp.where(causal[d], a, 0.0)
                scores = a if scores is None else scores + a
                q_e.append(q[sl] * jnp.exp(b_j))
                k_e.append(k_j * jnp.exp(tot - b_j))
                dec.append(jnp.exp(tot))
                ends = jnp.maximum(jnp.abs(dist[0:1]), jnp.abs(dist[c_len - 1 : c_len]))
                span = ends if span is None else jnp.maximum(span, ends)
            a16_scr[rows_j, :] = scores.astype(BF16)
            qe_scr[rows_j, :] = jnp.concatenate(q_e, axis=1).astype(BF16)
            u_scr[c] = _state_increment(v16[sl], jnp.concatenate(k_e, axis=1).astype(BF16))
            dec_scr[c] = jnp.concatenate(dec, axis=1)
            safe_smem[c] = jnp.logical_and(q_ok, jnp.max(span) <= SAFE_EXP_RANGE).astype(jnp.int32)

    _skewed_stages(n_blocks, cumulate, prepare)

    _scan_states(n_chunks, n_seqs, lambda c: dec_scr[c], u_scr, st_scr, s0_ref, st_out_ref)

    def redo_scores(c, carry):
        @pl.when(safe_smem[c] == 0)
        def _():
            rows = pl.ds(pl.multiple_of(c * c_len, c_len), c_len)
            q = qa_ref[rows, :]
            scores = jnp.zeros((c_len, c_len), F32)
            for d, (k_ref, g_hi_ref, g_lo_ref, rev) in enumerate(directions):
                b = _cumsum_rows(g_hi_ref[rows, :].astype(F32) + g_lo_ref[rows, :].astype(F32), rev)
                scores = scores + jnp.where(causal[d], _hgrn_direct_scores(q, k_ref[rows, :], b, k_scr, b_scr), 0.0)
            a16_scr[rows, :] = scores.astype(BF16)

        return carry

    lax.fori_loop(0, n_chunks, redo_scores, 0)

    def combine(blk, slot):
        for j in range(chunks_per_blk):
            c = blk * chunks_per_blk + j
            rows = pl.ds(pl.multiple_of(c * c_len, c_len), c_len)
            o = _dot(a16_scr[rows, :], va_ref[rows, :]) + _dot_nt(qe_scr[rows, :], st_scr[c])
            o_scr[slot, j * c_len : (j + 1) * c_len, :] = o

    gn = gn_ref[...]

    def head_norm_gate(blk, slot):
        rows = pl.ds(pl.multiple_of(blk * blk_len, blk_len), blk_len)
        o = o_scr[slot]
        o = o * lax.rsqrt(jnp.mean(o * o, axis=-1, keepdims=True) + EPS) * gn
        mix_ref[rows, :] = (o * gate_ref[rows, :]).astype(BF16)

    _skewed_stages(n_blocks, combine, head_norm_gate)


def _hgrn(p32, p16, norm_gain, state, layer, emit_state, n_seqs):
    p32, p16 = _fold_sequences(p32, n_seqs), _fold_sequences(p16, n_seqs)
    _, batch, seq_len, _ = p32.shape
    n_chunks = seq_len // HGRN_CHUNK
    blk_len = min(HGRN_BLOCK, seq_len)
    has_s0 = state is not None
    f32_slabs = (P32_QA, P32_KF, P32_KB, P32_GATE_A)
    bf16_slabs = (P16_GF_HI, P16_GF_LO, P16_GB_HI, P16_GB_LO, P16_VA)
    in_specs = [_slab_spec(seq_len, s) for s in f32_slabs + bf16_slabs]
    in_specs.append(pl.BlockSpec((None, 1, HEAD_DIM), lambda b, h: (layer, 0, h)))
    args = [p32] * len(f32_slabs) + [p16] * len(bf16_slabs) + [norm_gain]
    if has_s0:
        in_specs.append(_state_in_spec(layer, n_seqs))
        args.append(state)
    mix_spec, mix_shape = _mix_out(batch, seq_len)
    out_specs, out_shape = [mix_spec], [mix_shape]
    if emit_state:
        out_specs.append(_state_out_spec(n_seqs))
        out_shape.append(jax.ShapeDtypeStruct((batch * n_seqs, 2, N_HEADS, HEAD_DIM, HEAD_DIM), F32))
    outs = pl.pallas_call(
        functools.partial(_hgrn_kernel, seq_len=seq_len, n_seqs=n_seqs, has_s0=has_s0, emit_state=emit_state),
        grid=(batch, N_HEADS),
        in_specs=in_specs,
        out_specs=out_specs,
        out_shape=out_shape,
        scratch_shapes=[
            pltpu.VMEM((2, blk_len, blk_len), BF16),
            pltpu.VMEM((2, 2, blk_len, HEAD_DIM), F32),
            pltpu.VMEM((seq_len, HGRN_CHUNK), BF16),
            pltpu.VMEM((seq_len, 2 * HEAD_DIM), BF16),
            pltpu.VMEM((2, blk_len, HEAD_DIM), F32),
            pltpu.VMEM((n_chunks, HEAD_DIM, 2 * HEAD_DIM), F32),
            pltpu.VMEM((n_chunks, 1, 2 * HEAD_DIM), F32),
            pltpu.VMEM((n_chunks, HEAD_DIM, 2 * HEAD_DIM), BF16),
            pltpu.SMEM((n_chunks,), jnp.int32),
            pltpu.VMEM((HGRN_CHUNK, HEAD_DIM), F32),
            pltpu.VMEM((HGRN_CHUNK, HEAD_DIM), F32),
        ],
        compiler_params=pltpu.CompilerParams(vmem_limit_bytes=VMEM_LIMIT),
        name="hgrn_scan",
    )(*args)
    return (outs[0], outs[1]) if emit_state else (outs[0], None)


def _log_sigmoid(x):
    return jnp.minimum(x, 0.0) - jnp.log1p(jnp.exp(-jnp.abs(x)))


def _ret_kernel(*refs, layer, seq_len, n_seqs, has_s0, emit_state):
    logit_ref, q_ref, k_ref, v_ref, gate_ref, gn_ref = refs[:6]
    refs = refs[6:]
    s0_ref = None
    if has_s0:
        s0_ref, refs = refs[0], refs[1:]
    mix_ref, refs = refs[0], refs[1:]
    st_out_ref = None
    if emit_state:
        st_out_ref, refs = refs[0], refs[1:]
    a16_scr, o_scr, u_scr, st_scr = refs

    c_len = RET_CHUNK
    n_chunks = seq_len // c_len
    head = pl.program_id(1)
    lg_f = _log_sigmoid(jnp.full((1, HEAD_DIM), logit_ref[layer, 0, head], F32))
    lg_b = _log_sigmoid(jnp.full((1, HEAD_DIM), logit_ref[layer, 1, head], F32))

    t = lax.broadcasted_iota(jnp.int32, (c_len, HEAD_DIM), 0).astype(F32)
    q_dec_f = jnp.exp((t + 1.0) * lg_f)
    q_dec_b = jnp.exp((c_len - t) * lg_b)
    k_dec = jnp.concatenate([jnp.exp((c_len - 1.0 - t) * lg_f), jnp.exp(t * lg_b)], axis=1)
    chunk_dec = jnp.concatenate([jnp.exp(c_len * lg_f), jnp.exp(c_len * lg_b)], axis=1)
    t_idx = lax.broadcasted_iota(jnp.int32, (c_len, c_len), 0)
    s_idx = lax.broadcasted_iota(jnp.int32, (c_len, c_len), 1)
    dist = (t_idx - s_idx).astype(F32)
    decay = jnp.where(t_idx >= s_idx, jnp.exp(jnp.maximum(dist, 0.0) * lg_f), 0.0)
    decay = decay + jnp.where(t_idx <= s_idx, jnp.exp(jnp.maximum(-dist, 0.0) * lg_b), 0.0)

    def prepare(c, carry):
        rows = pl.ds(pl.multiple_of(c * c_len, c_len), c_len)
        k16 = k_ref[rows, :]
        a16_scr[rows, :] = (_dot_nt(q_ref[rows, :], k16) * decay).astype(BF16)
        k = k16.astype(F32)
        u_scr[c] = _state_increment(v_ref[rows, :], (jnp.concatenate([k, k], axis=1) * k_dec).astype(BF16))
        return carry

    lax.fori_loop(0, n_chunks, prepare, 0, unroll=min(4, n_chunks))

    _scan_states(n_chunks, n_seqs, lambda c: chunk_dec, u_scr, st_scr, s0_ref, st_out_ref)

    blk_len = min(RET_BLOCK, seq_len)
    chunks_per_blk = blk_len // c_len

    def combine(blk, slot):
        for j in range(chunks_per_blk):
            c = blk * chunks_per_blk + j
            rows = pl.ds(pl.multiple_of(c * c_len, c_len), c_len)
            q16 = q_ref[rows, :]
            o = _dot(a16_scr[rows, :], v_ref[rows, :])
            o = o + q_dec_f * _dot_nt(q16, st_scr[c, :, :HEAD_DIM]) + q_dec_b * _dot_nt(q16, st_scr[c, :, HEAD_DIM:])
            o_scr[slot, j * c_len : (j + 1) * c_len, :] = o

    gn = gn_ref[...]

    def head_norm_gate(blk, slot):
        rows = pl.ds(pl.multiple_of(blk * blk_len, blk_len), blk_len)
        o = o_scr[slot]
        o = o - jnp.mean(o, axis=-1, keepdims=True)
        o = o * lax.rsqrt(jnp.mean(o * o, axis=-1, keepdims=True) + EPS) * gn
        mix_ref[rows, :] = (o * gate_ref[rows, :]).astype(BF16)

    _skewed_stages(seq_len // blk_len, combine, head_norm_gate)


def _ret(p32, p16, decay_logit, norm_gain, state, layer, emit_state, n_seqs):
    p32, p16 = _fold_sequences(p32, n_seqs), _fold_sequences(p16, n_seqs)
    _, batch, seq_len, _ = p32.shape
    n_chunks = seq_len // RET_CHUNK
    has_s0 = state is not None
    in_specs = [pl.BlockSpec(memory_space=pltpu.SMEM)]
    in_specs += [_slab_spec(seq_len, s) for s in (P16_QR, P16_KR, P16_VR, P32_GATE_R)]
    in_specs.append(pl.BlockSpec((None, 1, HEAD_DIM), lambda b, h: (layer, 0, h)))
    args = [decay_logit, p16, p16, p16, p32, norm_gain]
    if has_s0:
        in_specs.append(_state_in_spec(layer, n_seqs))
        args.append(state)
    mix_spec, mix_shape = _mix_out(batch, seq_len)
    out_specs, out_shape = [mix_spec], [mix_shape]
    if emit_state:
        out_specs.append(_state_out_spec(n_seqs))
        out_shape.append(jax.ShapeDtypeStruct((batch * n_seqs, 2, N_HEADS, HEAD_DIM, HEAD_DIM), F32))
    outs = pl.pallas_call(
        functools.partial(
            _ret_kernel, layer=layer, seq_len=seq_len, n_seqs=n_seqs, has_s0=has_s0, emit_state=emit_state
        ),
        grid=(batch, N_HEADS),
        in_specs=in_specs,
        out_specs=out_specs,
        out_shape=out_shape,
        scratch_shapes=[
            pltpu.VMEM((seq_len, RET_CHUNK), BF16),
            pltpu.VMEM((2, min(RET_BLOCK, seq_len), HEAD_DIM), F32),
            pltpu.VMEM((n_chunks, HEAD_DIM, 2 * HEAD_DIM), F32),
            pltpu.VMEM((n_chunks, HEAD_DIM, 2 * HEAD_DIM), BF16),
        ],
        compiler_params=pltpu.CompilerParams(vmem_limit_bytes=VMEM_LIMIT),
        name="ret_scan",
    )(*args)
    return (outs[0], outs[1]) if emit_state else (outs[0], None)


def _out_ffn_kernel(*refs, d_ff, final):
    ma_ref, mb_ref, x_ref, mod_ref, gain_ref, wo_ref, wu_ref, wd_ref = refs[:8]
    refs = refs[8:]
    if final:
        gfin_ref, refs = refs[0], refs[1:]
    out_ref, act_scr = refs

    mod = mod_ref[...]
    mix = jnp.concatenate([ma_ref[hd] for hd in range(N_HEADS)] + [mb_ref[hd] for hd in range(N_HEADS)], axis=1)
    x1 = x_ref[...] + mod[2:3] * _dot(mix, wo_ref[...])
    h = _modnorm(x1, gain_ref[...], mod[4:5], mod[3:4]).astype(BF16)
    for c in range(d_ff // FFN_CHUNK):
        lo = c * FFN_CHUNK
        gate = _dot(h, wu_ref[:, lo : lo + FFN_CHUNK])
        up = _dot(h, wu_ref[:, d_ff + lo : d_ff + lo + FFN_CHUNK])
        act_scr[:, lo : lo + FFN_CHUNK] = (_silu(gate) * up).astype(BF16)
    x2 = x1 + mod[5:6] * _dot(act_scr[...], wd_ref[...])
    if final:
        x2 = x2 * lax.rsqrt(jnp.mean(x2 * x2, axis=-1, keepdims=True) + EPS) * gfin_ref[...]
    out_ref[...] = x2


def _out_ffn(mix_a, mix_b, x, mod, gain, w_out, w_up, w_down, final_gain, layer, tile_m, cond_index):
    n_tok, d_model = x.shape
    d_ff = w_down.shape[1]
    final = final_gain is not None
    const = lambda i: (layer, 0, 0)
    in_specs = [
        pl.BlockSpec((N_HEADS, tile_m, HEAD_DIM), lambda i: (0, i, 0)),
        pl.BlockSpec((N_HEADS, tile_m, HEAD_DIM), lambda i: (0, i, 0)),
        pl.BlockSpec((tile_m, d_model), lambda i: (i, 0)),
        pl.BlockSpec((None, None, N_MOD, d_model), lambda i: (layer, cond_index(i), 0, 0)),
        pl.BlockSpec((None, 1, d_model), const),
        pl.BlockSpec((None, 2 * N_HEADS * HEAD_DIM, d_model), const, pipeline_mode=pl.Buffered(1)),
        pl.BlockSpec((None, d_model, 2 * d_ff), const, pipeline_mode=pl.Buffered(1)),
        pl.BlockSpec((None, d_ff, d_model), const, pipeline_mode=pl.Buffered(1)),
    ]
    args = [mix_a, mix_b, x, mod, gain, w_out, w_up, w_down]
    if final:
        in_specs.append(pl.BlockSpec((1, d_model), lambda i: (0, 0)))
        args.append(final_gain)
    return pl.pallas_call(
        functools.partial(_out_ffn_kernel, d_ff=d_ff, final=final),
        grid=(n_tok // tile_m,),
        in_specs=in_specs,
        out_specs=pl.BlockSpec((tile_m, d_model), lambda i: (i, 0)),
        out_shape=jax.ShapeDtypeStruct((n_tok, d_model), F32),
        scratch_shapes=[pltpu.VMEM((tile_m, d_ff), BF16)],
        compiler_params=pltpu.CompilerParams(vmem_limit_bytes=VMEM_LIMIT),
        name="out_ffn",
    )(*args)


def kernel(x_prompt, x_sample, state_hgrn, state_ret, c, c_ctx, ada_w, ada_b, norm_mix, norm_ffn, w_in,
           hgrn_lb_logits, hgrn_norm, ret_decay_logit, ret_norm, w_out, w_up, w_down, norm_final):
    batch, seq, d_model = x_prompt.shape
    dec_batch, dec_seq, _ = x_sample.shape
    depth = ada_w.shape[0]
    tile_m = 512
    assert w_in.shape[2] == D_IN and state_hgrn.shape[3] == N_HEADS and state_ret.shape[3] == N_HEADS
    assert dec_batch < COND_ROWS and dec_seq % tile_m == 0 and (batch * seq) % tile_m == 0
    assert dec_seq % GRID_W == 0
    for n_l in (seq, dec_seq):
        assert n_l % min(HGRN_BLOCK, n_l) == 0 and min(HGRN_BLOCK, n_l) % HGRN_CHUNK == 0
        assert n_l % min(RET_BLOCK, n_l) == 0 and n_l % RET_CHUNK == 0

    cond = jnp.concatenate(
        [c.astype(F32), c_ctx.astype(F32)[None, :], jnp.zeros((COND_ROWS - dec_batch - 1, d_model), F32)], axis=0
    )
    mod = _modulation(cond, ada_w, ada_b)

    w_in16, w_out16, w_up16, w_down16 = (w.astype(BF16) for w in (w_in, w_out, w_up, w_down))
    gain_mix = norm_mix.reshape(depth, 1, d_model)
    gain_ffn = norm_ffn.reshape(depth, 1, d_model)
    gain_hgrn = hgrn_norm.reshape(depth, 1, -1)
    gain_ret = ret_norm.reshape(depth, 1, -1)
    gain_final = norm_final.reshape(1, d_model)
    rope_tables = _rope_tables(dec_seq)

    tiles_per_seq = dec_seq // tile_m
    groups = [
        [x_prompt.astype(F32).reshape(batch * seq, d_model), batch, seq, lambda i: dec_batch, None, None, None, True],
        [x_sample.astype(F32).reshape(dec_batch * dec_seq, d_model), dec_batch, dec_seq,
         lambda i: i // tiles_per_seq, state_hgrn, state_ret, rope_tables, False],
    ]
    new_h, new_r = [], []
    for layer in range(depth):
        final_gain = gain_final if layer == depth - 1 else None
        for grp in groups:
            x, n_b, n_l, cond_index, s0_h, s0_r, tables, emit = grp
            p32, p16 = _in_proj(x, mod, gain_mix, w_in16, hgrn_lb_logits, tables, layer, tile_m, cond_index, tiles_per_seq)
            p32 = p32.reshape(N_P32, n_b, n_l, HEAD_DIM)
            p16 = p16.reshape(N_P16, n_b, n_l, HEAD_DIM)
            n_seqs = max(1, min(n_b, SCAN_ROWS_PER_STEP // n_l))
            mix_a, st_a = _hgrn(p32, p16, gain_hgrn, s0_h, layer, emit, n_seqs)
            mix_b, st_b = _ret(p32, p16, ret_decay_logit, gain_ret, s0_r, layer, emit, n_seqs)
            grp[0] = _out_ffn(
                mix_a.reshape(N_HEADS, n_b * n_l, HEAD_DIM), mix_b.reshape(N_HEADS, n_b * n_l, HEAD_DIM), x, mod,
                gain_ffn, w_out16, w_up16, w_down16, final_gain, layer, tile_m, cond_index,
            )
            if emit:
                new_h.append(st_a)
                new_r.append(st_b)
    y_prompt = groups[0][0].reshape(batch, seq, d_model).astype(x_prompt.dtype)
    y_sample = groups[1][0].reshape(dec_batch, dec_seq, d_model).astype(x_sample.dtype)
    new_state_hgrn = jnp.stack(new_h, axis=1).astype(state_hgrn.dtype)
    new_state_ret = jnp.stack(new_r, axis=1).astype(state_ret.dtype)
    return (y_prompt, y_sample, new_state_hgrn, new_state_ret)
```

```python
import functools

import jax
import jax.numpy as jnp
from jax import lax
from jax.experimental import pallas as pl
from jax.experimental.pallas import tpu as pltpu

F32 = jnp.float32
BF16 = jnp.bfloat16

N_HEADS = 4
HEAD_DIM = 128
N_SECTIONS = 9
SECTION = N_HEADS * HEAD_DIM
D_IN = N_SECTIONS * SECTION
GRID_W = 64
ROPE_BASE = 10000.0
EPS = 1e-6
GATE_FLOOR = 1e-12

P32_QA, P32_KF, P32_KB, P32_GATE_A, P32_GATE_R = range(5)
P16_GF_HI, P16_GF_LO, P16_GB_HI, P16_GB_LO, P16_VA, P16_QR, P16_KR, P16_VR = range(8)
N_P32 = 5 * N_HEADS
N_P16 = 8 * N_HEADS

N_MOD = 6
COND_ROWS = 16

HGRN_CHUNK = 64
HGRN_BLOCK = 256
RET_CHUNK = 128
RET_BLOCK = 256
SCAN_ROWS_PER_STEP = 1024
SAFE_EXP_RANGE = 80.0
SAFE_Q_MAX = 1e3
FFN_CHUNK = 256
MOD_TILE_N = 1536
VMEM_LIMIT = 56 * 1024 * 1024


def _silu(x):
    h = 0.5 * x
    return h + h * jnp.tanh(h)


def _dot(a, b):
    return jnp.dot(a, b, preferred_element_type=F32)


def _dot_nt(a, b):
    return lax.dot_general(a, b, (((1,), (1,)), ((), ())), preferred_element_type=F32)


def _split_bf16(x):
    hi = x.astype(BF16)
    return hi, (x - hi.astype(F32)).astype(BF16)


def _modnorm(x, gain, scale, shift):
    ms = jnp.mean(x * x, axis=-1, keepdims=True)
    return x * lax.rsqrt(ms + EPS) * gain * (1.0 + scale) + shift


def _mod_kernel(cond_ref, w_ref, b_ref, out_ref):
    s_hi, s_lo = _split_bf16(_silu(cond_ref[...]))
    w_hi, w_lo = _split_bf16(w_ref[...])
    s_both = jnp.concatenate([s_hi, s_lo], axis=0)
    p = _dot(s_both, w_hi) + _dot(s_both, w_lo)
    out_ref[...] = p[:COND_ROWS] + p[COND_ROWS:] + b_ref[...]


def _modulation(cond, ada_w, ada_b):
    depth, d_model, n_out = ada_w.shape
    out = pl.pallas_call(
        _mod_kernel,
        grid=(depth, n_out // MOD_TILE_N),
        in_specs=[
            pl.BlockSpec((COND_ROWS, d_model), lambda l, j: (0, 0)),
            pl.BlockSpec((None, d_model, MOD_TILE_N), lambda l, j: (l, 0, j)),
            pl.BlockSpec((None, 1, MOD_TILE_N), lambda l, j: (l, 0, j)),
        ],
        out_specs=pl.BlockSpec((None, COND_ROWS, MOD_TILE_N), lambda l, j: (l, 0, j)),
        out_shape=jax.ShapeDtypeStruct((depth, COND_ROWS, n_out), F32),
        compiler_params=pltpu.CompilerParams(vmem_limit_bytes=VMEM_LIMIT),
        name="adaln_mod",
    )(cond, ada_w, ada_b.reshape(depth, 1, n_out))
    return out.reshape(depth, COND_ROWS, N_MOD, d_model)


def _rope(x, cos, sin_signed):
    lane = lax.broadcasted_iota(jnp.int32, x.shape, 1)
    quarter = HEAD_DIM // 4
    partner = jnp.where(lane % (2 * quarter) < quarter, pltpu.roll(x, HEAD_DIM - quarter, 1), pltpu.roll(x, quarter, 1))
    return x * cos + partner * sin_signed


def _hgrn_lower_bounds(logits, layer):
    e = jnp.exp(logits - jnp.max(logits, axis=0, keepdims=True))
    p = e / jnp.sum(e, axis=0, keepdims=True)
    cum = p[0]
    for i in range(1, layer + 1):
        cum = cum + p[i]
    return jnp.clip(cum - p[0], 0.0, 1.0)


def _in_proj_kernel(*refs, layer, use_rope):
    x_ref, mod_ref, gain_ref, w_ref, lbl_ref = refs[:5]
    refs = refs[5:]
    if use_rope:
        cos_ref, sin_ref, refs = refs[0], refs[1], refs[2:]
    p32_ref, p16_ref = refs

    mod = mod_ref[...]
    h = _modnorm(x_ref[...], gain_ref[...], mod[1:2], mod[0:1]).astype(BF16)
    lb = _hgrn_lower_bounds(lbl_ref[...], layer)

    def proj(j):
        return _dot(h, w_ref[:, j * SECTION : (j + 1) * SECTION])

    def put(ref, section, val):
        for hd in range(N_HEADS):
            ref[section * N_HEADS + hd] = val[:, hd * HEAD_DIM : (hd + 1) * HEAD_DIM].astype(ref.dtype)

    put(p32_ref, P32_QA, _silu(proj(0)))
    for d, (sec_hi, sec_lo, sec_k) in enumerate(
        ((P16_GF_HI, P16_GF_LO, P32_KF), (P16_GB_HI, P16_GB_LO, P32_KB))
    ):
        lb_d = lb[d : d + 1]
        c0, c1 = 0.5 + 0.5 * lb_d, 0.5 - 0.5 * lb_d
        ct = c1 * jnp.tanh(0.5 * proj(1 + d))
        g_hi, g_lo = _split_bf16(jnp.log(jnp.maximum(c0 + ct, GATE_FLOOR)))
        put(p16_ref, sec_hi, g_hi)
        put(p16_ref, sec_lo, g_lo)
        put(p32_ref, sec_k, c1 - ct)
    put(p16_ref, P16_VA, proj(3))
    put(p32_ref, P32_GATE_A, _silu(proj(4)))
    q = proj(5)
    k = proj(6) * (HEAD_DIM ** -0.5)
    if use_rope:
        cos, sin = cos_ref[...], sin_ref[...]
        heads = [slice(hd * HEAD_DIM, (hd + 1) * HEAD_DIM) for hd in range(N_HEADS)]
        q = jnp.concatenate([_rope(q[:, sl], cos, sin) for sl in heads], axis=1)
        k = jnp.concatenate([_rope(k[:, sl], cos, sin) for sl in heads], axis=1)
    put(p16_ref, P16_QR, q)
    put(p16_ref, P16_KR, k)
    put(p16_ref, P16_VR, proj(7))
    put(p32_ref, P32_GATE_R, _silu(proj(8)))


def _in_proj(x, mod, gain, w, lb_logits, rope_tables, layer, tile_m, cond_index, tiles_per_seq):
    n_tok, d_model = x.shape
    depth = lb_logits.shape[0]
    use_rope = rope_tables is not None
    in_specs = [
        pl.BlockSpec((tile_m, d_model), lambda i: (i, 0)),
        pl.BlockSpec((None, None, N_MOD, d_model), lambda i: (layer, cond_index(i), 0, 0)),
        pl.BlockSpec((None, 1, d_model), lambda i: (layer, 0, 0)),
        pl.BlockSpec((None, d_model, D_IN), lambda i: (layer, 0, 0), pipeline_mode=pl.Buffered(1)),
        pl.BlockSpec((depth, 2, SECTION), lambda i: (0, 0, 0)),
    ]
    args = [x, mod, gain, w, lb_logits]
    if use_rope:
        in_specs += [pl.BlockSpec((tile_m, HEAD_DIM), lambda i: (i % tiles_per_seq, 0))] * 2
        args += list(rope_tables)
    return pl.pallas_call(
        functools.partial(_in_proj_kernel, layer=layer, use_rope=use_rope),
        grid=(n_tok // tile_m,),
        in_specs=in_specs,
        out_specs=[
            pl.BlockSpec((N_P32, tile_m, HEAD_DIM), lambda i: (0, i, 0)),
            pl.BlockSpec((N_P16, tile_m, HEAD_DIM), lambda i: (0, i, 0)),
        ],
        out_shape=[
            jax.ShapeDtypeStruct((N_P32, n_tok, HEAD_DIM), F32),
            jax.ShapeDtypeStruct((N_P16, n_tok, HEAD_DIM), BF16),
        ],
        compiler_params=pltpu.CompilerParams(vmem_limit_bytes=VMEM_LIMIT),
        name="in_proj",
    )(*args)


def _rope_tables(seq_len):
    quarter = HEAD_DIM // 4
    inv = ROPE_BASE ** (-jnp.arange(quarter, dtype=F32) / quarter)
    pos = jnp.arange(seq_len, dtype=jnp.int32)
    ang_r = (pos // GRID_W).astype(F32)[:, None] * inv
    ang_c = (pos % GRID_W).astype(F32)[:, None] * inv
    cos = jnp.concatenate([jnp.cos(ang_r), jnp.cos(ang_r), jnp.cos(ang_c), jnp.cos(ang_c)], axis=1)
    sin = jnp.concatenate([-jnp.sin(ang_r), jnp.sin(ang_r), -jnp.sin(ang_c), jnp.sin(ang_c)], axis=1)
    return cos, sin


def _skewed_stages(n_steps, produce, consume, steps_per_trip=2):
    steps_per_trip = min(steps_per_trip, n_steps)
    assert steps_per_trip % 2 == 0 and n_steps % steps_per_trip == 0
    produce(0, 0)

    def body(trip, carry):
        first = steps_per_trip * trip
        for j in range(steps_per_trip):
            produce(jnp.minimum(first + j + 1, n_steps - 1), (j + 1) % 2)
            consume(first + j, j % 2)
        return carry

    lax.fori_loop(0, n_steps // steps_per_trip, body, 0)


def _state_increment(v16, k_cat):
    return _dot(v16.astype(F32).T.astype(BF16), k_cat)


def _run_states(n_chunks, first, dec_of, u_scr, st_scr, s_f, s_b):
    def step(i, carry):
        s_f, s_b = carry
        cf, cb = first + i, first + n_chunks - 1 - i
        st_scr[cf, :, :HEAD_DIM] = s_f.astype(BF16)
        st_scr[cb, :, HEAD_DIM:] = s_b.astype(BF16)
        s_f = dec_of(cf)[:, :HEAD_DIM] * s_f + u_scr[cf, :, :HEAD_DIM]
        s_b = dec_of(cb)[:, HEAD_DIM:] * s_b + u_scr[cb, :, HEAD_DIM:]
        return s_f, s_b

    return lax.fori_loop(0, n_chunks, step, (s_f, s_b), unroll=2)


def _scan_states(n_chunks, n_seqs, dec_of, u_scr, st_scr, s0_ref, st_out_ref):
    per_seq = n_chunks // n_seqs
    for s in range(n_seqs):
        if s0_ref is None:
            s_f = s_b = jnp.zeros((HEAD_DIM, HEAD_DIM), F32)
        else:
            s_f, s_b = s0_ref[s, 0].T, s0_ref[s, 1].T
        s_f, s_b = _run_states(per_seq, s * per_seq, dec_of, u_scr, st_scr, s_f, s_b)
        if st_out_ref is not None:
            st_out_ref[s, 0] = s_f.T
            st_out_ref[s, 1] = s_b.T


def _slab_spec(seq_len, section):
    return pl.BlockSpec((None, None, seq_len, HEAD_DIM), lambda b, h: (section * N_HEADS + h, b, 0, 0))


def _state_in_spec(layer, n_seqs):
    return pl.BlockSpec((n_seqs, None, 2, None, HEAD_DIM, HEAD_DIM), lambda b, h: (b, layer, 0, h, 0, 0))


def _state_out_spec(n_seqs):
    return pl.BlockSpec((n_seqs, 2, None, HEAD_DIM, HEAD_DIM), lambda b, h: (b, 0, h, 0, 0))


def _fold_sequences(slabs, n_seqs):
    n, batch, seq_len, d = slabs.shape
    assert batch % n_seqs == 0
    return slabs.reshape(n, batch // n_seqs, n_seqs * seq_len, d)


def _mix_out(batch, seq_len):
    spec = pl.BlockSpec((None, None, seq_len, HEAD_DIM), lambda b, h: (h, b, 0, 0))
    return spec, jax.ShapeDtypeStruct((N_HEADS, batch, seq_len, HEAD_DIM), BF16)


def _cumsum_rows(g, reverse):
    n = g.shape[0]
    pos = lax.broadcasted_iota(jnp.int32, g.shape, 0)
    sh = 1
    while sh < n:
        if reverse:
            g = g + jnp.where(pos < n - sh, pltpu.roll(g, n - sh, 0), 0.0)
        else:
            g = g + jnp.where(pos >= sh, pltpu.roll(g, sh, 0), 0.0)
        sh *= 2
    return g


def _causal_mask(n, lower):
    t_idx = lax.broadcasted_iota(jnp.int32, (n, n), 0)
    s_idx = lax.broadcasted_iota(jnp.int32, (n, n), 1)
    return t_idx >= s_idx if lower else t_idx <= s_idx


def _hgrn_direct_scores(q, k, b, k_scr, b_scr):
    n = q.shape[0]
    k_scr[...] = k
    b_scr[...] = b
    col = lax.broadcasted_iota(jnp.int32, (n, n), 1)

    def body(s, acc):
        z = q * k_scr[pl.ds(s, 1), :] * jnp.exp(jnp.minimum(b - b_scr[pl.ds(s, 1), :], 0.0))
        return jnp.where(col == s, jnp.sum(z, axis=1, keepdims=True), acc)

    return lax.fori_loop(0, n, body, jnp.zeros((n, n), F32))


def _hgrn_kernel(*refs, seq_len, n_seqs, has_s0, emit_state):
    qa_ref, kf_ref, kb_ref, gate_ref, gf_hi_ref, gf_lo_ref, gb_hi_ref, gb_lo_ref, va_ref, gn_ref = refs[:10]
    refs = refs[10:]
    s0_ref = None
    if has_s0:
        s0_ref, refs = refs[0], refs[1:]
    mix_ref, refs = refs[0], refs[1:]
    st_out_ref = None
    if emit_state:
        st_out_ref, refs = refs[0], refs[1:]
    tri_scr, cum_scr, a16_scr, qe_scr, o_scr, u_scr, dec_scr, st_scr, safe_smem, k_scr, b_scr = refs

    c_len = HGRN_CHUNK
    blk_len = min(HGRN_BLOCK, seq_len)
    n_chunks = seq_len // c_len
    n_blocks = seq_len // blk_len
    chunks_per_blk = blk_len // c_len
    directions = ((kf_ref, gf_hi_ref, gf_lo_ref, False), (kb_ref, gb_hi_ref, gb_lo_ref, True))
    causal = (_causal_mask(c_len, True), _causal_mask(c_len, False))

    t_idx = lax.broadcasted_iota(jnp.int32, (blk_len, blk_len), 0)
    s_idx = lax.broadcasted_iota(jnp.int32, (blk_len, blk_len), 1)
    same_chunk = (t_idx // c_len) == (s_idx // c_len)
    tri_scr[0] = jnp.where(jnp.logical_and(same_chunk, s_idx <= t_idx), 1.0, 0.0).astype(BF16)
    tri_scr[1] = jnp.where(jnp.logical_and(same_chunk, s_idx >= t_idx), 1.0, 0.0).astype(BF16)

    def cumulate(blk, slot):
        rows = pl.ds(pl.multiple_of(blk * blk_len, blk_len), blk_len)
        for d, (_, g_hi_ref, g_lo_ref, _) in enumerate(directions):
            cum = _dot(tri_scr[d], jnp.concatenate([g_hi_ref[rows, :], g_lo_ref[rows, :]], axis=1))
            cum_scr[slot, d] = cum[:, :HEAD_DIM] + cum[:, HEAD_DIM:]

    def prepare(blk, slot):
        r0 = pl.multiple_of(blk * blk_len, blk_len)
        rows = pl.ds(r0, blk_len)
        q = qa_ref[rows, :]
        v16 = va_ref[rows, :]
        q_ok = jnp.max(jnp.abs(q)) <= SAFE_Q_MAX
        gates = [(k_ref[rows, :], cum_scr[slot, d]) for d, (k_ref, _, _, _) in enumerate(directions)]
        for j in range(chunks_per_blk):
            sl = slice(j * c_len, (j + 1) * c_len)
            rows_j = pl.ds(r0 + j * c_len, c_len)
            c = blk * chunks_per_blk + j
            q_e, k_e, dec, span, scores = [], [], [], None, None
            for d, (k, b) in enumerate(gates):
                k_j, b_j = k[sl], b[sl]
                tot = b_j[0:1] if directions[d][3] else b_j[c_len - 1 : c_len]
                dist = b_j - b_j[c_len // 2 - 1 : c_len // 2]
                a = _dot_nt((q[sl] * jnp.exp(dist)).astype(BF16), (k_j * jnp.exp(-dist)).astype(BF16))
                a = jnp.where(causal[d], a, 0.0)
                scores = a if scores is None else scores + a
                q_e.append(q[sl] * jnp.exp(b_j))
                k_e.append(k_j * jnp.exp(tot - b_j))
                dec.append(jnp.exp(tot))
                ends = jnp.maximum(jnp.abs(dist[0:1]), jnp.abs(dist[c_len - 1 : c_len]))
                span = ends if span is None else jnp.maximum(span, ends)
            a16_scr[rows_j, :] = scores.astype(BF16)
            qe_scr[rows_j, :] = jnp.concatenate(q_e, axis=1).astype(BF16)
            u_scr[c] = _state_increment(v16[sl], jnp.concatenate(k_e, axis=1).astype(BF16))
            dec_scr[c] = jnp.concatenate(dec, axis=1)
            safe_smem[c] = jnp.logical_and(q_ok, jnp.max(span) <= SAFE_EXP_RANGE).astype(jnp.int32)

    _skewed_stages(n_blocks, cumulate, prepare, steps_per_trip=4)

    _scan_states(n_chunks, n_seqs, lambda c: dec_scr[c], u_scr, st_scr, s0_ref, st_out_ref)

    def redo_scores(c, carry):
        @pl.when(safe_smem[c] == 0)
        def _():
            rows = pl.ds(pl.multiple_of(c * c_len, c_len), c_len)
            q = qa_ref[rows, :]
            scores = jnp.zeros((c_len, c_len), F32)
            for d, (k_ref, g_hi_ref, g_lo_ref, rev) in enumerate(directions):
                b = _cumsum_rows(g_hi_ref[rows, :].astype(F32) + g_lo_ref[rows, :].astype(F32), rev)
                scores = scores + jnp.where(causal[d], _hgrn_direct_scores(q, k_ref[rows, :], b, k_scr, b_scr), 0.0)
            a16_scr[rows, :] = scores.astype(BF16)

        return carry

    lax.fori_loop(0, n_chunks, redo_scores, 0)

    def combine(blk, slot):
        for j in range(chunks_per_blk):
            c = blk * chunks_per_blk + j
            rows = pl.ds(pl.multiple_of(c * c_len, c_len), c_len)
            o = _dot(a16_scr[rows, :], va_ref[rows, :]) + _dot_nt(qe_scr[rows, :], st_scr[c])
            o_scr[slot, j * c_len : (j + 1) * c_len, :] = o

    gn = gn_ref[...]

    def head_norm_gate(blk, slot):
        rows = pl.ds(pl.multiple_of(blk * blk_len, blk_len), blk_len)
        o = o_scr[slot]
        o = o * lax.rsqrt(jnp.mean(o * o, axis=-1, keepdims=True) + EPS) * gn
        mix_ref[rows, :] = (o * gate_ref[rows, :]).astype(BF16)

    _skewed_stages(n_blocks, combine, head_norm_gate, steps_per_trip=8)


def _hgrn(p32, p16, norm_gain, state, layer, emit_state, n_seqs):
    p32, p16 = _fold_sequences(p32, n_seqs), _fold_sequences(p16, n_seqs)
    _, batch, seq_len, _ = p32.shape
    n_chunks = seq_len // HGRN_CHUNK
    blk_len = min(HGRN_BLOCK, seq_len)
    has_s0 = state is not None
    f32_slabs = (P32_QA, P32_KF, P32_KB, P32_GATE_A)
    bf16_slabs = (P16_GF_HI, P16_GF_LO, P16_GB_HI, P16_GB_LO, P16_VA)
    in_specs = [_slab_spec(seq_len, s) for s in f32_slabs + bf16_slabs]
    in_specs.append(pl.BlockSpec((None, 1, HEAD_DIM), lambda b, h: (layer, 0, h)))
    args = [p32] * len(f32_slabs) + [p16] * len(bf16_slabs) + [norm_gain]
    if has_s0:
        in_specs.append(_state_in_spec(layer, n_seqs))
        args.append(state)
    mix_spec, mix_shape = _mix_out(batch, seq_len)
    out_specs, out_shape = [mix_spec], [mix_shape]
    if emit_state:
        out_specs.append(_state_out_spec(n_seqs))
        out_shape.append(jax.ShapeDtypeStruct((batch * n_seqs, 2, N_HEADS, HEAD_DIM, HEAD_DIM), F32))
    outs = pl.pallas_call(
        functools.partial(_hgrn_kernel, seq_len=seq_len, n_seqs=n_seqs, has_s0=has_s0, emit_state=emit_state),
        grid=(batch, N_HEADS),
        in_specs=in_specs,
        out_specs=out_specs,
        out_shape=out_shape,
        scratch_shapes=[
            pltpu.VMEM((2, blk_len, blk_len), BF16),
            pltpu.VMEM((2, 2, blk_len, HEAD_DIM), F32),
            pltpu.VMEM((seq_len, HGRN_CHUNK), BF16),
            pltpu.VMEM((seq_len, 2 * HEAD_DIM), BF16),
            pltpu.VMEM((2, blk_len, HEAD_DIM), F32),
            pltpu.VMEM((n_chunks, HEAD_DIM, 2 * HEAD_DIM), F32),
            pltpu.VMEM((n_chunks, 1, 2 * HEAD_DIM), F32),
            pltpu.VMEM((n_chunks, HEAD_DIM, 2 * HEAD_DIM), BF16),
            pltpu.SMEM((n_chunks,), jnp.int32),
            pltpu.VMEM((HGRN_CHUNK, HEAD_DIM), F32),
            pltpu.VMEM((HGRN_CHUNK, HEAD_DIM), F32),
        ],
        compiler_params=pltpu.CompilerParams(vmem_limit_bytes=VMEM_LIMIT),
        name="hgrn_scan",
    )(*args)
    return (outs[0], outs[1]) if emit_state else (outs[0], None)


def _log_sigmoid(x):
    return jnp.minimum(x, 0.0) - jnp.log1p(jnp.exp(-jnp.abs(x)))


def _ret_kernel(*refs, layer, seq_len, n_seqs, has_s0, emit_state):
    logit_ref, q_ref, k_ref, v_ref, gate_ref, gn_ref = refs[:6]
    refs = refs[6:]
    s0_ref = None
    if has_s0:
        s0_ref, refs = refs[0], refs[1:]
    mix_ref, refs = refs[0], refs[1:]
    st_out_ref = None
    if emit_state:
        st_out_ref, refs = refs[0], refs[1:]
    a16_scr, o_scr, u_scr, st_scr = refs

    c_len = RET_CHUNK
    n_chunks = seq_len // c_len
    head = pl.program_id(1)
    lg_f = _log_sigmoid(jnp.full((1, HEAD_DIM), logit_ref[layer, 0, head], F32))
    lg_b = _log_sigmoid(jnp.full((1, HEAD_DIM), logit_ref[layer, 1, head], F32))

    t = lax.broadcasted_iota(jnp.int32, (c_len, HEAD_DIM), 0).astype(F32)
    q_dec_f = jnp.exp((t + 1.0) * lg_f)
    q_dec_b = jnp.exp((c_len - t) * lg_b)
    k_dec = jnp.concatenate([jnp.exp((c_len - 1.0 - t) * lg_f), jnp.exp(t * lg_b)], axis=1)
    chunk_dec = jnp.concatenate([jnp.exp(c_len * lg_f), jnp.exp(c_len * lg_b)], axis=1)
    t_idx = lax.broadcasted_iota(jnp.int32, (c_len, c_len), 0)
    s_idx = lax.broadcasted_iota(jnp.int32, (c_len, c_len), 1)
    dist = (t_idx - s_idx).astype(F32)
    decay = jnp.where(t_idx >= s_idx, jnp.exp(jnp.maximum(dist, 0.0) * lg_f), 0.0)
    decay = decay + jnp.where(t_idx <= s_idx, jnp.exp(jnp.maximum(-dist, 0.0) * lg_b), 0.0)

    def prepare(c, carry):
        rows = pl.ds(pl.multiple_of(c * c_len, c_len), c_len)
        k16 = k_ref[rows, :]
        a16_scr[rows, :] = (_dot_nt(q_ref[rows, :], k16) * decay).astype(BF16)
        k = k16.astype(F32)
        u_scr[c] = _state_increment(v_ref[rows, :], (jnp.concatenate([k, k], axis=1) * k_dec).astype(BF16))
        return carry

    lax.fori_loop(0, n_chunks, prepare, 0, unroll=min(16, n_chunks))

    _scan_states(n_chunks, n_seqs, lambda c: chunk_dec, u_scr, st_scr, s0_ref, st_out_ref)

    blk_len = min(RET_BLOCK, seq_len)
    chunks_per_blk = blk_len // c_len

    def combine(blk, slot):
        for j in range(chunks_per_blk):
            c = blk * chunks_per_blk + j
            rows = pl.ds(pl.multiple_of(c * c_len, c_len), c_len)
            q16 = q_ref[rows, :]
            o = _dot(a16_scr[rows, :], v_ref[rows, :])
            o = o + q_dec_f * _dot_nt(q16, st_scr[c, :, :HEAD_DIM]) + q_dec_b * _dot_nt(q16, st_scr[c, :, HEAD_DIM:])
            o_scr[slot, j * c_len : (j + 1) * c_len, :] = o

    gn = gn_ref[...]

    def head_norm_gate(blk, slot):
        rows = pl.ds(pl.multiple_of(blk * blk_len, blk_len), blk_len)
        o = o_scr[slot]
        o = o - jnp.mean(o, axis=-1, keepdims=True)
        o = o * lax.rsqrt(jnp.mean(o * o, axis=-1, keepdims=True) + EPS) * gn
        mix_ref[rows, :] = (o * gate_ref[rows, :]).astype(BF16)

    _skewed_stages(seq_len // blk_len, combine, head_norm_gate, steps_per_trip=8)


def _ret(p32, p16, decay_logit, norm_gain, state, layer, emit_state, n_seqs):
    p32, p16 = _fold_sequences(p32, n_seqs), _fold_sequences(p16, n_seqs)
    _, batch, seq_len, _ = p32.shape
    n_chunks = seq_len // RET_CHUNK
    has_s0 = state is not None
    in_specs = [pl.BlockSpec(memory_space=pltpu.SMEM)]
    in_specs += [_slab_spec(seq_len, s) for s in (P16_QR, P16_KR, P16_VR, P32_GATE_R)]
    in_specs.append(pl.BlockSpec((None, 1, HEAD_DIM), lambda b, h: (layer, 0, h)))
    args = [decay_logit, p16, p16, p16, p32, norm_gain]
    if has_s0:
        in_specs.append(_state_in_spec(layer, n_seqs))
        args.append(state)
    mix_spec, mix_shape = _mix_out(batch, seq_len)
    out_specs, out_shape = [mix_spec], [mix_shape]
    if emit_state:
        out_specs.append(_state_out_spec(n_seqs))
        out_shape.append(jax.ShapeDtypeStruct((batch * n_seqs, 2, N_HEADS, HEAD_DIM, HEAD_DIM), F32))
    outs = pl.pallas_call(
        functools.partial(
            _ret_kernel, layer=layer, seq_len=seq_len, n_seqs=n_seqs, has_s0=has_s0, emit_state=emit_state
        ),
        grid=(batch, N_HEADS),
        in_specs=in_specs,
        out_specs=out_specs,
        out_shape=out_shape,
        scratch_shapes=[
            pltpu.VMEM((seq_len, RET_CHUNK), BF16),
            pltpu.VMEM((2, min(RET_BLOCK, seq_len), HEAD_DIM), F32),
            pltpu.VMEM((n_chunks, HEAD_DIM, 2 * HEAD_DIM), F32),
            pltpu.VMEM((n_chunks, HEAD_DIM, 2 * HEAD_DIM), BF16),
        ],
        compiler_params=pltpu.CompilerParams(vmem_limit_bytes=VMEM_LIMIT),
        name="ret_scan",
    )(*args)
    return (outs[0], outs[1]) if emit_state else (outs[0], None)


def _out_ffn_kernel(*refs, d_ff, final):
    ma_ref, mb_ref, x_ref, mod_ref, gain_ref, wo_ref, wu_ref, wd_ref = refs[:8]
    refs = refs[8:]
    if final:
        gfin_ref, refs = refs[0], refs[1:]
    out_ref, act_scr = refs

    mod = mod_ref[...]
    mix = jnp.concatenate([ma_ref[hd] for hd in range(N_HEADS)] + [mb_ref[hd] for hd in range(N_HEADS)], axis=1)
    x1 = x_ref[...] + mod[2:3] * _dot(mix, wo_ref[...])
    h = _modnorm(x1, gain_ref[...], mod[4:5], mod[3:4]).astype(BF16)
    for c in range(d_ff // FFN_CHUNK):
        lo = c * FFN_CHUNK
        gate = _dot(h, wu_ref[:, lo : lo + FFN_CHUNK])
        up = _dot(h, wu_ref[:, d_ff + lo : d_ff + lo + FFN_CHUNK])
        act_scr[:, lo : lo + FFN_CHUNK] = (_silu(gate) * up).astype(BF16)
    x2 = x1 + mod[5:6] * _dot(act_scr[...], wd_ref[...])
    if final:
        x2 = x2 * lax.rsqrt(jnp.mean(x2 * x2, axis=-1, keepdims=True) + EPS) * gfin_ref[...]
    out_ref[...] = x2


def _out_ffn(mix_a, mix_b, x, mod, gain, w_out, w_up, w_down, final_gain, layer, tile_m, cond_index):
    n_tok, d_model = x.shape
    d_ff = w_down.shape[1]
    final = final_gain is not None
    const = lambda i: (layer, 0, 0)
    in_specs = [
        pl.BlockSpec((N_HEADS, tile_m, HEAD_DIM), lambda i: (0, i, 0)),
        pl.BlockSpec((N_HEADS, tile_m, HEAD_DIM), lambda i: (0, i, 0)),
        pl.BlockSpec((tile_m, d_model), lambda i: (i, 0)),
        pl.BlockSpec((None, None, N_MOD, d_model), lambda i: (layer, cond_index(i), 0, 0)),
        pl.BlockSpec((None, 1, d_model), const),
        pl.BlockSpec((None, 2 * N_HEADS * HEAD_DIM, d_model), const, pipeline_mode=pl.Buffered(1)),
        pl.BlockSpec((None, d_model, 2 * d_ff), const, pipeline_mode=pl.Buffered(1)),
        pl.BlockSpec((None, d_ff, d_model), const, pipeline_mode=pl.Buffered(1)),
    ]
    args = [mix_a, mix_b, x, mod, gain, w_out, w_up, w_down]
    if final:
        in_specs.append(pl.BlockSpec((1, d_model), lambda i: (0, 0)))
        args.append(final_gain)
    return pl.pallas_call(
        functools.partial(_out_ffn_kernel, d_ff=d_ff, final=final),
        grid=(n_tok // tile_m,),
        in_specs=in_specs,
        out_specs=pl.BlockSpec((tile_m, d_model), lambda i: (i, 0)),
        out_shape=jax.ShapeDtypeStruct((n_tok, d_model), F32),
        scratch_shapes=[pltpu.VMEM((tile_m, d_ff), BF16)],
        compiler_params=pltpu.CompilerParams(vmem_limit_bytes=VMEM_LIMIT),
        name="out_ffn",
    )(*args)


def kernel(x_prompt, x_sample, state_hgrn, state_ret, c, c_ctx, ada_w, ada_b, norm_mix, norm_ffn, w_in,
           hgrn_lb_logits, hgrn_norm, ret_decay_logit, ret_norm, w_out, w_up, w_down, norm_final):
    batch, seq, d_model = x_prompt.shape
    dec_batch, dec_seq, _ = x_sample.shape
    depth = ada_w.shape[0]
    tile_m = 512
    assert w_in.shape[2] == D_IN and state_hgrn.shape[3] == N_HEADS and state_ret.shape[3] == N_HEADS
    assert dec_batch < COND_ROWS and dec_seq % tile_m == 0 and (batch * seq) % tile_m == 0
    assert dec_seq % GRID_W == 0
    for n_l in (seq, dec_seq):
        assert n_l % min(HGRN_BLOCK, n_l) == 0 and min(HGRN_BLOCK, n_l) % HGRN_CHUNK == 0
        assert n_l % min(RET_BLOCK, n_l) == 0 and n_l % RET_CHUNK == 0

    cond = jnp.concatenate(
        [c.astype(F32), c_ctx.astype(F32)[None, :], jnp.zeros((COND_ROWS - dec_batch - 1, d_model), F32)], axis=0
    )
    mod = _modulation(cond, ada_w, ada_b)

    w_in16, w_out16, w_up16, w_down16 = (w.astype(BF16) for w in (w_in, w_out, w_up, w_down))
    gain_mix = norm_mix.reshape(depth, 1, d_model)
    gain_ffn = norm_ffn.reshape(depth, 1, d_model)
    gain_hgrn = hgrn_norm.reshape(depth, 1, -1)
    gain_ret = ret_norm.reshape(depth, 1, -1)
    gain_final = norm_final.reshape(1, d_model)
    rope_tables = _rope_tables(dec_seq)

    tiles_per_seq = dec_seq // tile_m
    groups = [
        [x_prompt.astype(F32).reshape(batch * seq, d_model), batch, seq, lambda i: dec_batch, None, None, None, True],
        [x_sample.astype(F32).reshape(dec_batch * dec_seq, d_model), dec_batch, dec_seq,
         lambda i: i // tiles_per_seq, state_hgrn, state_ret, rope_tables, False],
    ]
    new_h, new_r = [], []
    for layer in range(depth):
        final_gain = gain_final if layer == depth - 1 else None
        for grp in groups:
            x, n_b, n_l, cond_index, s0_h, s0_r, tables, emit = grp
            p32, p16 = _in_proj(x, mod, gain_mix, w_in16, hgrn_lb_logits, tables, layer, tile_m, cond_index, tiles_per_seq)
            p32 = p32.reshape(N_P32, n_b, n_l, HEAD_DIM)
            p16 = p16.reshape(N_P16, n_b, n_l, HEAD_DIM)
            n_seqs = max(1, min(n_b, SCAN_ROWS_PER_STEP // n_l))
            mix_a, st_a = _hgrn(p32, p16, gain_hgrn, s0_h, layer, emit, n_seqs)
            mix_b, st_b = _ret(p32, p16, ret_decay_logit, gain_ret, s0_r, layer, emit, n_seqs)
            grp[0] = _out_ffn(
                mix_a.reshape(N_HEADS, n_b * n_l, HEAD_DIM), mix_b.reshape(N_HEADS, n_b * n_l, HEAD_DIM), x, mod,
                gain_ffn, w_out16, w_up16, w_down16, final_gain, layer, tile_m, cond_index,
            )
            if emit:
                new_h.append(st_a)
                new_r.append(st_b)
    y_prompt = groups[0][0].reshape(batch, seq, d_model).astype(x_prompt.dtype)
    y_sample = groups[1][0].reshape(dec_batch, dec_seq, d_model).astype(x_sample.dtype)
    new_state_hgrn = jnp.stack(new_h, axis=1).astype(state_hgrn.dtype)
    new_state_ret = jnp.stack(new_r, axis=1).astype(state_ret.dtype)
    return (y_prompt, y_sample, new_state_hgrn, new_state_ret)
```

```python
import functools

import jax
import jax.numpy as jnp
from jax import lax
from jax.experimental import pallas as pl
from jax.experimental.pallas import tpu as pltpu

F32 = jnp.float32
BF16 = jnp.bfloat16

N_HEADS = 4
HEAD_DIM = 128
N_SECTIONS = 9
SECTION = N_HEADS * HEAD_DIM
D_IN = N_SECTIONS * SECTION
GRID_W = 64
ROPE_BASE = 10000.0
EPS = 1e-6
GATE_FLOOR = 1e-12

P32_QA, P32_KF, P32_KB, P32_GATE_A, P32_GATE_R = range(5)
P16_GF_HI, P16_GF_LO, P16_GB_HI, P16_GB_LO, P16_VA, P16_QR, P16_KR, P16_VR = range(8)
N_P32 = 5 * N_HEADS
N_P16 = 8 * N_HEADS

N_MOD = 6
COND_ROWS = 16

HGRN_CHUNK = 64
HGRN_BLOCK = 256
RET_CHUNK = 128
RET_BLOCK = 256
SCAN_ROWS_PER_STEP = 1024
SAFE_EXP_RANGE = 80.0
SAFE_Q_MAX = 1e3
FFN_CHUNK = 256
MOD_TILE_N = 1536
VMEM_LIMIT = 56 * 1024 * 1024


def _silu(x):
    h = 0.5 * x
    return h + h * jnp.tanh(h)


def _dot(a, b):
    return jnp.dot(a, b, preferred_element_type=F32)


def _dot_nt(a, b):
    return lax.dot_general(a, b, (((1,), (1,)), ((), ())), preferred_element_type=F32)


def _split_bf16(x):
    hi = x.astype(BF16)
    return hi, (x - hi.astype(F32)).astype(BF16)


def _modnorm(x, gain, scale, shift):
    ms = jnp.mean(x * x, axis=-1, keepdims=True)
    return x * lax.rsqrt(ms + EPS) * gain * (1.0 + scale) + shift


def _mod_kernel(cond_ref, w_ref, b_ref, out_ref):
    s_hi, s_lo = _split_bf16(_silu(cond_ref[...]))
    w_hi, w_lo = _split_bf16(w_ref[...])
    s_both = jnp.concatenate([s_hi, s_lo], axis=0)
    p = _dot(s_both, w_hi) + _dot(s_both, w_lo)
    out_ref[...] = p[:COND_ROWS] + p[COND_ROWS:] + b_ref[...]


def _modulation(cond, ada_w, ada_b):
    depth, d_model, n_out = ada_w.shape
    out = pl.pallas_call(
        _mod_kernel,
        grid=(depth, n_out // MOD_TILE_N),
        in_specs=[
            pl.BlockSpec((COND_ROWS, d_model), lambda l, j: (0, 0)),
            pl.BlockSpec((None, d_model, MOD_TILE_N), lambda l, j: (l, 0, j)),
            pl.BlockSpec((None, 1, MOD_TILE_N), lambda l, j: (l, 0, j)),
        ],
        out_specs=pl.BlockSpec((None, COND_ROWS, MOD_TILE_N), lambda l, j: (l, 0, j)),
        out_shape=jax.ShapeDtypeStruct((depth, COND_ROWS, n_out), F32),
        compiler_params=pltpu.CompilerParams(vmem_limit_bytes=VMEM_LIMIT),
        name="adaln_mod",
    )(cond, ada_w, ada_b.reshape(depth, 1, n_out))
    return out.reshape(depth, COND_ROWS, N_MOD, d_model)


def _rope(x, cos, sin_signed):
    lane = lax.broadcasted_iota(jnp.int32, x.shape, 1)
    quarter = HEAD_DIM // 4
    partner = jnp.where(lane % (2 * quarter) < quarter, pltpu.roll(x, HEAD_DIM - quarter, 1), pltpu.roll(x, quarter, 1))
    return x * cos + partner * sin_signed


def _hgrn_lower_bounds(logits, layer):
    e = jnp.exp(logits - jnp.max(logits, axis=0, keepdims=True))
    p = e / jnp.sum(e, axis=0, keepdims=True)
    cum = p[0]
    for i in range(1, layer + 1):
        cum = cum + p[i]
    return jnp.clip(cum - p[0], 0.0, 1.0)


def _in_proj_kernel(*refs, layer, use_rope):
    x_ref, mod_ref, gain_ref, w_ref, lbl_ref = refs[:5]
    refs = refs[5:]
    if use_rope:
        cos_ref, sin_ref, refs = refs[0], refs[1], refs[2:]
    p32_ref, p16_ref = refs

    mod = mod_ref[...]
    h = _modnorm(x_ref[...], gain_ref[...], mod[1:2], mod[0:1]).astype(BF16)
    lb = _hgrn_lower_bounds(lbl_ref[...], layer)

    def proj(j):
        return _dot(h, w_ref[:, j * SECTION : (j + 1) * SECTION])

    def put(ref, section, val):
        for hd in range(N_HEADS):
            ref[section * N_HEADS + hd] = val[:, hd * HEAD_DIM : (hd + 1) * HEAD_DIM].astype(ref.dtype)

    put(p32_ref, P32_QA, _silu(proj(0)))
    for d, (sec_hi, sec_lo, sec_k) in enumerate(
        ((P16_GF_HI, P16_GF_LO, P32_KF), (P16_GB_HI, P16_GB_LO, P32_KB))
    ):
        lb_d = lb[d : d + 1]
        c0, c1 = 0.5 + 0.5 * lb_d, 0.5 - 0.5 * lb_d
        ct = c1 * jnp.tanh(0.5 * proj(1 + d))
        g_hi, g_lo = _split_bf16(jnp.log(jnp.maximum(c0 + ct, GATE_FLOOR)))
        put(p16_ref, sec_hi, g_hi)
        put(p16_ref, sec_lo, g_lo)
        put(p32_ref, sec_k, c1 - ct)
    put(p16_ref, P16_VA, proj(3))
    put(p32_ref, P32_GATE_A, _silu(proj(4)))
    q = proj(5)
    k = proj(6) * (HEAD_DIM ** -0.5)
    if use_rope:
        cos, sin = cos_ref[...], sin_ref[...]
        heads = [slice(hd * HEAD_DIM, (hd + 1) * HEAD_DIM) for hd in range(N_HEADS)]
        q = jnp.concatenate([_rope(q[:, sl], cos, sin) for sl in heads], axis=1)
        k = jnp.concatenate([_rope(k[:, sl], cos, sin) for sl in heads], axis=1)
    put(p16_ref, P16_QR, q)
    put(p16_ref, P16_KR, k)
    put(p16_ref, P16_VR, proj(7))
    put(p32_ref, P32_GATE_R, _silu(proj(8)))


def _in_proj(x, mod, gain, w, lb_logits, rope_tables, layer, tile_m, cond_index, tiles_per_seq):
    n_tok, d_model = x.shape
    depth = lb_logits.shape[0]
    use_rope = rope_tables is not None
    in_specs = [
        pl.BlockSpec((tile_m, d_model), lambda i: (i, 0)),
        pl.BlockSpec((None, None, N_MOD, d_model), lambda i: (layer, cond_index(i), 0, 0)),
        pl.BlockSpec((None, 1, d_model), lambda i: (layer, 0, 0)),
        pl.BlockSpec((None, d_model, D_IN), lambda i: (layer, 0, 0), pipeline_mode=pl.Buffered(1)),
        pl.BlockSpec((depth, 2, SECTION), lambda i: (0, 0, 0)),
    ]
    args = [x, mod, gain, w, lb_logits]
    if use_rope:
        in_specs += [pl.BlockSpec((tile_m, HEAD_DIM), lambda i: (i % tiles_per_seq, 0))] * 2
        args += list(rope_tables)
    return pl.pallas_call(
        functools.partial(_in_proj_kernel, layer=layer, use_rope=use_rope),
        grid=(n_tok // tile_m,),
        in_specs=in_specs,
        out_specs=[
            pl.BlockSpec((N_P32, tile_m, HEAD_DIM), lambda i: (0, i, 0)),
            pl.BlockSpec((N_P16, tile_m, HEAD_DIM), lambda i: (0, i, 0)),
        ],
        out_shape=[
            jax.ShapeDtypeStruct((N_P32, n_tok, HEAD_DIM), F32),
            jax.ShapeDtypeStruct((N_P16, n_tok, HEAD_DIM), BF16),
        ],
        compiler_params=pltpu.CompilerParams(vmem_limit_bytes=VMEM_LIMIT),
        name="in_proj",
    )(*args)


def _rope_tables(seq_len):
    quarter = HEAD_DIM // 4
    inv = ROPE_BASE ** (-jnp.arange(quarter, dtype=F32) / quarter)
    pos = jnp.arange(seq_len, dtype=jnp.int32)
    ang_r = (pos // GRID_W).astype(F32)[:, None] * inv
    ang_c = (pos % GRID_W).astype(F32)[:, None] * inv
    cos = jnp.concatenate([jnp.cos(ang_r), jnp.cos(ang_r), jnp.cos(ang_c), jnp.cos(ang_c)], axis=1)
    sin = jnp.concatenate([-jnp.sin(ang_r), jnp.sin(ang_r), -jnp.sin(ang_c), jnp.sin(ang_c)], axis=1)
    return cos, sin


def _skewed_stages(n_steps, produce, consume, steps_per_trip=2):
    steps_per_trip = min(steps_per_trip, n_steps)
    assert steps_per_trip % 2 == 0 and n_steps % steps_per_trip == 0
    produce(0, 0)

    def body(trip, carry):
        first = steps_per_trip * trip
        for j in range(steps_per_trip):
            produce(jnp.minimum(first + j + 1, n_steps - 1), (j + 1) % 2)
            consume(first + j, j % 2)
        return carry

    lax.fori_loop(0, n_steps // steps_per_trip, body, 0)


def _state_increment(v16, k_cat):
    return _dot(v16.astype(F32).T.astype(BF16), k_cat)


def _run_states(n_chunks, first, dec_of, u_scr, st_scr, s_f, s_b):
    def step(i, carry):
        s_f, s_b = carry
        cf, cb = first + i, first + n_chunks - 1 - i
        st_scr[cf, :, :HEAD_DIM] = s_f.astype(BF16)
        st_scr[cb, :, HEAD_DIM:] = s_b.astype(BF16)
        s_f = dec_of(cf)[:, :HEAD_DIM] * s_f + u_scr[cf, :, :HEAD_DIM]
        s_b = dec_of(cb)[:, HEAD_DIM:] * s_b + u_scr[cb, :, HEAD_DIM:]
        return s_f, s_b

    return lax.fori_loop(0, n_chunks, step, (s_f, s_b), unroll=2)


def _scan_states(n_chunks, n_seqs, dec_of, u_scr, st_scr, s0_ref, st_out_ref):
    per_seq = n_chunks // n_seqs
    for s in range(n_seqs):
        if s0_ref is None:
            s_f = s_b = jnp.zeros((HEAD_DIM, HEAD_DIM), F32)
        else:
            s_f, s_b = s0_ref[s, 0].T, s0_ref[s, 1].T
        s_f, s_b = _run_states(per_seq, s * per_seq, dec_of, u_scr, st_scr, s_f, s_b)
        if st_out_ref is not None:
            st_out_ref[s, 0] = s_f.T
            st_out_ref[s, 1] = s_b.T


def _slab_spec(seq_len, section):
    return pl.BlockSpec((None, None, seq_len, HEAD_DIM), lambda b, h: (section * N_HEADS + h, b, 0, 0))


def _state_in_spec(layer, n_seqs):
    return pl.BlockSpec((n_seqs, None, 2, None, HEAD_DIM, HEAD_DIM), lambda b, h: (b, layer, 0, h, 0, 0))


def _state_out_spec(n_seqs):
    return pl.BlockSpec((n_seqs, 2, None, HEAD_DIM, HEAD_DIM), lambda b, h: (b, 0, h, 0, 0))


def _fold_sequences(slabs, n_seqs):
    n, batch, seq_len, d = slabs.shape
    assert batch % n_seqs == 0
    return slabs.reshape(n, batch // n_seqs, n_seqs * seq_len, d)


def _mix_out(batch, seq_len):
    spec = pl.BlockSpec((None, None, seq_len, HEAD_DIM), lambda b, h: (h, b, 0, 0))
    return spec, jax.ShapeDtypeStruct((N_HEADS, batch, seq_len, HEAD_DIM), BF16)


def _cumsum_rows(g, reverse):
    n = g.shape[0]
    pos = lax.broadcasted_iota(jnp.int32, g.shape, 0)
    sh = 1
    while sh < n:
        if reverse:
            g = g + jnp.where(pos < n - sh, pltpu.roll(g, n - sh, 0), 0.0)
        else:
            g = g + jnp.where(pos >= sh, pltpu.roll(g, sh, 0), 0.0)
        sh *= 2
    return g


def _causal_mask(n, lower):
    t_idx = lax.broadcasted_iota(jnp.int32, (n, n), 0)
    s_idx = lax.broadcasted_iota(jnp.int32, (n, n), 1)
    return t_idx >= s_idx if lower else t_idx <= s_idx


def _hgrn_direct_scores(q, k, b, k_scr, b_scr):
    n = q.shape[0]
    k_scr[...] = k
    b_scr[...] = b
    col = lax.broadcasted_iota(jnp.int32, (n, n), 1)

    def body(s, acc):
        z = q * k_scr[pl.ds(s, 1), :] * jnp.exp(jnp.minimum(b - b_scr[pl.ds(s, 1), :], 0.0))
        return jnp.where(col == s, jnp.sum(z, axis=1, keepdims=True), acc)

    return lax.fori_loop(0, n, body, jnp.zeros((n, n), F32))


def _hgrn_kernel(*refs, seq_len, n_seqs, has_s0, emit_state):
    qa_ref, kf_ref, kb_ref, gate_ref, gf_hi_ref, gf_lo_ref, gb_hi_ref, gb_lo_ref, va_ref, gn_ref = refs[:10]
    refs = refs[10:]
    s0_ref = None
    if has_s0:
        s0_ref, refs = refs[0], refs[1:]
    mix_ref, refs = refs[0], refs[1:]
    st_out_ref = None
    if emit_state:
        st_out_ref, refs = refs[0], refs[1:]
    tri_scr, cum_scr, a16_scr, qe_scr, o_scr, u_scr, dec_scr, st_scr, safe_smem, k_scr, b_scr = refs

    c_len = HGRN_CHUNK
    blk_len = min(HGRN_BLOCK, seq_len)
    n_chunks = seq_len // c_len
    n_blocks = seq_len // blk_len
    chunks_per_blk = blk_len // c_len
    directions = ((kf_ref, gf_hi_ref, gf_lo_ref, False), (kb_ref, gb_hi_ref, gb_lo_ref, True))
    causal = (_causal_mask(c_len, True), _causal_mask(c_len, False))

    @pl.when(jnp.logical_and(pl.program_id(0) == 0, pl.program_id(1) == 0))
    def _():
        t_idx = lax.broadcasted_iota(jnp.int32, (blk_len, blk_len), 0)
        s_idx = lax.broadcasted_iota(jnp.int32, (blk_len, blk_len), 1)
        same_chunk = (t_idx // c_len) == (s_idx // c_len)
        tri_scr[0] = jnp.where(jnp.logical_and(same_chunk, s_idx <= t_idx), 1.0, 0.0).astype(BF16)
        tri_scr[1] = jnp.where(jnp.logical_and(same_chunk, s_idx >= t_idx), 1.0, 0.0).astype(BF16)

    safe_smem[n_chunks] = 0

    def cumulate(blk, slot):
        rows = pl.ds(pl.multiple_of(blk * blk_len, blk_len), blk_len)
        for d, (_, g_hi_ref, g_lo_ref, _) in enumerate(directions):
            cum = _dot(tri_scr[d], jnp.concatenate([g_hi_ref[rows, :], g_lo_ref[rows, :]], axis=1))
            cum_scr[slot, d] = cum[:, :HEAD_DIM] + cum[:, HEAD_DIM:]

    def prepare(blk, slot):
        r0 = pl.multiple_of(blk * blk_len, blk_len)
        rows = pl.ds(r0, blk_len)
        q = qa_ref[rows, :]
        v16 = va_ref[rows, :]
        q_ok = jnp.max(jnp.abs(q)) <= SAFE_Q_MAX
        gates = [(k_ref[rows, :], cum_scr[slot, d]) for d, (k_ref, _, _, _) in enumerate(directions)]
        for j in range(chunks_per_blk):
            sl = slice(j * c_len, (j + 1) * c_len)
            rows_j = pl.ds(r0 + j * c_len, c_len)
            c = blk * chunks_per_blk + j
            q_e, k_e, dec, span, scores = [], [], [], None, None
            for d, (k, b) in enumerate(gates):
                k_j, b_j = k[sl], b[sl]
                tot = b_j[0:1] if directions[d][3] else b_j[c_len - 1 : c_len]
                dist = b_j - b_j[c_len // 2 - 1 : c_len // 2]
                a = _dot_nt((q[sl] * jnp.exp(dist)).astype(BF16), (k_j * jnp.exp(-dist)).astype(BF16))
                a = jnp.where(causal[d], a, 0.0)
                scores = a if scores is None else scores + a
                q_e.append(q[sl] * jnp.exp(b_j))
                k_e.append(k_j * jnp.exp(tot - b_j))
                dec.append(jnp.exp(tot))
                ends = jnp.maximum(jnp.abs(dist[0:1]), jnp.abs(dist[c_len - 1 : c_len]))
                span = ends if span is None else jnp.maximum(span, ends)
            a16_scr[rows_j, :] = scores.astype(BF16)
            qe_scr[rows_j, :] = jnp.concatenate(q_e, axis=1).astype(BF16)
            u_scr[c] = _state_increment(v16[sl], jnp.concatenate(k_e, axis=1).astype(BF16))
            dec_scr[c] = jnp.concatenate(dec, axis=1)
            safe = jnp.logical_and(q_ok, jnp.max(span) <= SAFE_EXP_RANGE).astype(jnp.int32)
            safe_smem[c] = safe
            safe_smem[n_chunks] = safe_smem[n_chunks] + (1 - safe)

    _skewed_stages(n_blocks, cumulate, prepare, steps_per_trip=8)

    _scan_states(n_chunks, n_seqs, lambda c: dec_scr[c], u_scr, st_scr, s0_ref, st_out_ref)

    def redo_scores(c, carry):
        @pl.when(safe_smem[c] == 0)
        def _():
            rows = pl.ds(pl.multiple_of(c * c_len, c_len), c_len)
            q = qa_ref[rows, :]
            scores = jnp.zeros((c_len, c_len), F32)
            for d, (k_ref, g_hi_ref, g_lo_ref, rev) in enumerate(directions):
                b = _cumsum_rows(g_hi_ref[rows, :].astype(F32) + g_lo_ref[rows, :].astype(F32), rev)
                scores = scores + jnp.where(causal[d], _hgrn_direct_scores(q, k_ref[rows, :], b, k_scr, b_scr), 0.0)
            a16_scr[rows, :] = scores.astype(BF16)

        return carry

    @pl.when(safe_smem[n_chunks] > 0)
    def _():
        lax.fori_loop(0, n_chunks, redo_scores, 0)

    def combine(blk, slot):
        for j in range(chunks_per_blk):
            c = blk * chunks_per_blk + j
            rows = pl.ds(pl.multiple_of(c * c_len, c_len), c_len)
            o = _dot(a16_scr[rows, :], va_ref[rows, :]) + _dot_nt(qe_scr[rows, :], st_scr[c])
            o_scr[slot, j * c_len : (j + 1) * c_len, :] = o

    gn = gn_ref[...]

    def head_norm_gate(blk, slot):
        rows = pl.ds(pl.multiple_of(blk * blk_len, blk_len), blk_len)
        o = o_scr[slot]
        o = o * lax.rsqrt(jnp.mean(o * o, axis=-1, keepdims=True) + EPS) * gn
        mix_ref[rows, :] = (o * gate_ref[rows, :]).astype(BF16)

    _skewed_stages(n_blocks, combine, head_norm_gate, steps_per_trip=8)


def _hgrn(p32, p16, norm_gain, state, layer, emit_state, n_seqs):
    p32, p16 = _fold_sequences(p32, n_seqs), _fold_sequences(p16, n_seqs)
    _, batch, seq_len, _ = p32.shape
    n_chunks = seq_len // HGRN_CHUNK
    blk_len = min(HGRN_BLOCK, seq_len)
    has_s0 = state is not None
    f32_slabs = (P32_QA, P32_KF, P32_KB, P32_GATE_A)
    bf16_slabs = (P16_GF_HI, P16_GF_LO, P16_GB_HI, P16_GB_LO, P16_VA)
    in_specs = [_slab_spec(seq_len, s) for s in f32_slabs + bf16_slabs]
    in_specs.append(pl.BlockSpec((None, 1, HEAD_DIM), lambda b, h: (layer, 0, h)))
    args = [p32] * len(f32_slabs) + [p16] * len(bf16_slabs) + [norm_gain]
    if has_s0:
        in_specs.append(_state_in_spec(layer, n_seqs))
        args.append(state)
    mix_spec, mix_shape = _mix_out(batch, seq_len)
    out_specs, out_shape = [mix_spec], [mix_shape]
    if emit_state:
        out_specs.append(_state_out_spec(n_seqs))
        out_shape.append(jax.ShapeDtypeStruct((batch * n_seqs, 2, N_HEADS, HEAD_DIM, HEAD_DIM), F32))
    outs = pl.pallas_call(
        functools.partial(_hgrn_kernel, seq_len=seq_len, n_seqs=n_seqs, has_s0=has_s0, emit_state=emit_state),
        grid=(batch, N_HEADS),
        in_specs=in_specs,
        out_specs=out_specs,
        out_shape=out_shape,
        scratch_shapes=[
            pltpu.VMEM((2, blk_len, blk_len), BF16),
            pltpu.VMEM((2, 2, blk_len, HEAD_DIM), F32),
            pltpu.VMEM((seq_len, HGRN_CHUNK), BF16),
            pltpu.VMEM((seq_len, 2 * HEAD_DIM), BF16),
            pltpu.VMEM((2, blk_len, HEAD_DIM), F32),
            pltpu.VMEM((n_chunks, HEAD_DIM, 2 * HEAD_DIM), F32),
            pltpu.VMEM((n_chunks, 1, 2 * HEAD_DIM), F32),
            pltpu.VMEM((n_chunks, HEAD_DIM, 2 * HEAD_DIM), BF16),
            pltpu.SMEM((n_chunks + 1,), jnp.int32),
            pltpu.VMEM((HGRN_CHUNK, HEAD_DIM), F32),
            pltpu.VMEM((HGRN_CHUNK, HEAD_DIM), F32),
        ],
        compiler_params=pltpu.CompilerParams(vmem_limit_bytes=VMEM_LIMIT),
        name="hgrn_scan",
    )(*args)
    return (outs[0], outs[1]) if emit_state else (outs[0], None)


def _log_sigmoid(x):
    return jnp.minimum(x, 0.0) - jnp.log1p(jnp.exp(-jnp.abs(x)))


def _ret_kernel(*refs, layer, seq_len, n_seqs, has_s0, emit_state):
    logit_ref, q_ref, k_ref, v_ref, gate_ref, gn_ref = refs[:6]
    refs = refs[6:]
    s0_ref = None
    if has_s0:
        s0_ref, refs = refs[0], refs[1:]
    mix_ref, refs = refs[0], refs[1:]
    st_out_ref = None
    if emit_state:
        st_out_ref, refs = refs[0], refs[1:]
    a16_scr, o_scr, u_scr, st_scr = refs

    c_len = RET_CHUNK
    n_chunks = seq_len // c_len
    head = pl.program_id(1)
    lg_f = _log_sigmoid(jnp.full((1, HEAD_DIM), logit_ref[layer, 0, head], F32))
    lg_b = _log_sigmoid(jnp.full((1, HEAD_DIM), logit_ref[layer, 1, head], F32))

    t = lax.broadcasted_iota(jnp.int32, (c_len, HEAD_DIM), 0).astype(F32)
    q_dec_f = jnp.exp((t + 1.0) * lg_f)
    q_dec_b = jnp.exp((c_len - t) * lg_b)
    k_dec = jnp.concatenate([jnp.exp((c_len - 1.0 - t) * lg_f), jnp.exp(t * lg_b)], axis=1)
    chunk_dec = jnp.concatenate([jnp.exp(c_len * lg_f), jnp.exp(c_len * lg_b)], axis=1)
    t_idx = lax.broadcasted_iota(jnp.int32, (c_len, c_len), 0)
    s_idx = lax.broadcasted_iota(jnp.int32, (c_len, c_len), 1)
    dist = (t_idx - s_idx).astype(F32)
    decay = jnp.where(t_idx >= s_idx, jnp.exp(jnp.maximum(dist, 0.0) * lg_f), 0.0)
    decay = decay + jnp.where(t_idx <= s_idx, jnp.exp(jnp.maximum(-dist, 0.0) * lg_b), 0.0)

    def prepare(c, carry):
        rows = pl.ds(pl.multiple_of(c * c_len, c_len), c_len)
        k16 = k_ref[rows, :]
        a16_scr[rows, :] = (_dot_nt(q_ref[rows, :], k16) * decay).astype(BF16)
        k = k16.astype(F32)
        u_scr[c] = _state_increment(v_ref[rows, :], (jnp.concatenate([k, k], axis=1) * k_dec).astype(BF16))
        return carry

    lax.fori_loop(0, n_chunks, prepare, 0, unroll=min(16, n_chunks))

    _scan_states(n_chunks, n_seqs, lambda c: chunk_dec, u_scr, st_scr, s0_ref, st_out_ref)

    blk_len = min(RET_BLOCK, seq_len)
    chunks_per_blk = blk_len // c_len

    def combine(blk, slot):
        for j in range(chunks_per_blk):
            c = blk * chunks_per_blk + j
            rows = pl.ds(pl.multiple_of(c * c_len, c_len), c_len)
            q16 = q_ref[rows, :]
            o = _dot(a16_scr[rows, :], v_ref[rows, :])
            o = o + q_dec_f * _dot_nt(q16, st_scr[c, :, :HEAD_DIM]) + q_dec_b * _dot_nt(q16, st_scr[c, :, HEAD_DIM:])
            o_scr[slot, j * c_len : (j + 1) * c_len, :] = o

    gn = gn_ref[...]

    def head_norm_gate(blk, slot):
        rows = pl.ds(pl.multiple_of(blk * blk_len, blk_len), blk_len)
        o = o_scr[slot]
        o = o - jnp.mean(o, axis=-1, keepdims=True)
        o = o * lax.rsqrt(jnp.mean(o * o, axis=-1, keepdims=True) + EPS) * gn
        mix_ref[rows, :] = (o * gate_ref[rows, :]).astype(BF16)

    _skewed_stages(seq_len // blk_len, combine, head_norm_gate, steps_per_trip=8)


def _ret(p32, p16, decay_logit, norm_gain, state, layer, emit_state, n_seqs):
    p32, p16 = _fold_sequences(p32, n_seqs), _fold_sequences(p16, n_seqs)
    _, batch, seq_len, _ = p32.shape
    n_chunks = seq_len // RET_CHUNK
    has_s0 = state is not None
    in_specs = [pl.BlockSpec(memory_space=pltpu.SMEM)]
    in_specs += [_slab_spec(seq_len, s) for s in (P16_QR, P16_KR, P16_VR, P32_GATE_R)]
    in_specs.append(pl.BlockSpec((None, 1, HEAD_DIM), lambda b, h: (layer, 0, h)))
    args = [decay_logit, p16, p16, p16, p32, norm_gain]
    if has_s0:
        in_specs.append(_state_in_spec(layer, n_seqs))
        args.append(state)
    mix_spec, mix_shape = _mix_out(batch, seq_len)
    out_specs, out_shape = [mix_spec], [mix_shape]
    if emit_state:
        out_specs.append(_state_out_spec(n_seqs))
        out_shape.append(jax.ShapeDtypeStruct((batch * n_seqs, 2, N_HEADS, HEAD_DIM, HEAD_DIM), F32))
    outs = pl.pallas_call(
        functools.partial(
            _ret_kernel, layer=layer, seq_len=seq_len, n_seqs=n_seqs, has_s0=has_s0, emit_state=emit_state
        ),
        grid=(batch, N_HEADS),
        in_specs=in_specs,
        out_specs=out_specs,
        out_shape=out_shape,
        scratch_shapes=[
            pltpu.VMEM((seq_len, RET_CHUNK), BF16),
            pltpu.VMEM((2, min(RET_BLOCK, seq_len), HEAD_DIM), F32),
            pltpu.VMEM((n_chunks, HEAD_DIM, 2 * HEAD_DIM), F32),
            pltpu.VMEM((n_chunks, HEAD_DIM, 2 * HEAD_DIM), BF16),
        ],
        compiler_params=pltpu.CompilerParams(vmem_limit_bytes=VMEM_LIMIT),
        name="ret_scan",
    )(*args)
    return (outs[0], outs[1]) if emit_state else (outs[0], None)


def _out_ffn_kernel(*refs, d_ff, final):
    ma_ref, mb_ref, x_ref, mod_ref, gain_ref, wo_ref, wu_ref, wd_ref = refs[:8]
    refs = refs[8:]
    if final:
        gfin_ref, refs = refs[0], refs[1:]
    out_ref, act_scr = refs

    mod = mod_ref[...]
    mix = jnp.concatenate([ma_ref[hd] for hd in range(N_HEADS)] + [mb_ref[hd] for hd in range(N_HEADS)], axis=1)
    x1 = x_ref[...] + mod[2:3] * _dot(mix, wo_ref[...])
    h = _modnorm(x1, gain_ref[...], mod[4:5], mod[3:4]).astype(BF16)
    for c in range(d_ff // FFN_CHUNK):
        lo = c * FFN_CHUNK
        gate = _dot(h, wu_ref[:, lo : lo + FFN_CHUNK])
        up = _dot(h, wu_ref[:, d_ff + lo : d_ff + lo + FFN_CHUNK])
        act_scr[:, lo : lo + FFN_CHUNK] = (_silu(gate) * up).astype(BF16)
    x2 = x1 + mod[5:6] * _dot(act_scr[...], wd_ref[...])
    if final:
        x2 = x2 * lax.rsqrt(jnp.mean(x2 * x2, axis=-1, keepdims=True) + EPS) * gfin_ref[...]
    out_ref[...] = x2


def _out_ffn(mix_a, mix_b, x, mod, gain, w_out, w_up, w_down, final_gain, layer, tile_m, cond_index):
    n_tok, d_model = x.shape
    d_ff = w_down.shape[1]
    final = final_gain is not None
    const = lambda i: (layer, 0, 0)
    in_specs = [
        pl.BlockSpec((N_HEADS, tile_m, HEAD_DIM), lambda i: (0, i, 0)),
        pl.BlockSpec((N_HEADS, tile_m, HEAD_DIM), lambda i: (0, i, 0)),
        pl.BlockSpec((tile_m, d_model), lambda i: (i, 0)),
        pl.BlockSpec((None, None, N_MOD, d_model), lambda i: (layer, cond_index(i), 0, 0)),
        pl.BlockSpec((None, 1, d_model), const),
        pl.BlockSpec((None, 2 * N_HEADS * HEAD_DIM, d_model), const, pipeline_mode=pl.Buffered(1)),
        pl.BlockSpec((None, d_model, 2 * d_ff), const, pipeline_mode=pl.Buffered(1)),
        pl.BlockSpec((None, d_ff, d_model), const, pipeline_mode=pl.Buffered(1)),
    ]
    args = [mix_a, mix_b, x, mod, gain, w_out, w_up, w_down]
    if final:
        in_specs.append(pl.BlockSpec((1, d_model), lambda i: (0, 0)))
        args.append(final_gain)
    return pl.pallas_call(
        functools.partial(_out_ffn_kernel, d_ff=d_ff, final=final),
        grid=(n_tok // tile_m,),
        in_specs=in_specs,
        out_specs=pl.BlockSpec((tile_m, d_model), lambda i: (i, 0)),
        out_shape=jax.ShapeDtypeStruct((n_tok, d_model), F32),
        scratch_shapes=[pltpu.VMEM((tile_m, d_ff), BF16)],
        compiler_params=pltpu.CompilerParams(vmem_limit_bytes=VMEM_LIMIT),
        name="out_ffn",
    )(*args)


def kernel(x_prompt, x_sample, state_hgrn, state_ret, c, c_ctx, ada_w, ada_b, norm_mix, norm_ffn, w_in,
           hgrn_lb_logits, hgrn_norm, ret_decay_logit, ret_norm, w_out, w_up, w_down, norm_final):
    batch, seq, d_model = x_prompt.shape
    dec_batch, dec_seq, _ = x_sample.shape
    depth = ada_w.shape[0]
    tile_m = 512
    assert w_in.shape[2] == D_IN and state_hgrn.shape[3] == N_HEADS and state_ret.shape[3] == N_HEADS
    assert dec_batch < COND_ROWS and dec_seq % tile_m == 0 and (batch * seq) % tile_m == 0
    assert dec_seq % GRID_W == 0
    for n_l in (seq, dec_seq):
        assert n_l % min(HGRN_BLOCK, n_l) == 0 and min(HGRN_BLOCK, n_l) % HGRN_CHUNK == 0
        assert n_l % min(RET_BLOCK, n_l) == 0 and n_l % RET_CHUNK == 0

    cond = jnp.concatenate(
        [c.astype(F32), c_ctx.astype(F32)[None, :], jnp.zeros((COND_ROWS - dec_batch - 1, d_model), F32)], axis=0
    )
    mod = _modulation(cond, ada_w, ada_b)

    w_in16, w_out16, w_up16, w_down16 = (w.astype(BF16) for w in (w_in, w_out, w_up, w_down))
    gain_mix = norm_mix.reshape(depth, 1, d_model)
    gain_ffn = norm_ffn.reshape(depth, 1, d_model)
    gain_hgrn = hgrn_norm.reshape(depth, 1, -1)
    gain_ret = ret_norm.reshape(depth, 1, -1)
    gain_final = norm_final.reshape(1, d_model)
    rope_tables = _rope_tables(dec_seq)

    tiles_per_seq = dec_seq // tile_m
    groups = [
        [x_prompt.astype(F32).reshape(batch * seq, d_model), batch, seq, lambda i: dec_batch, None, None, None, True],
        [x_sample.astype(F32).reshape(dec_batch * dec_seq, d_model), dec_batch, dec_seq,
         lambda i: i // tiles_per_seq, state_hgrn, state_ret, rope_tables, False],
    ]
    new_h, new_r = [], []
    for layer in range(depth):
        final_gain = gain_final if layer == depth - 1 else None
        for grp in groups:
            x, n_b, n_l, cond_index, s0_h, s0_r, tables, emit = grp
            p32, p16 = _in_proj(x, mod, gain_mix, w_in16, hgrn_lb_logits, tables, layer, tile_m, cond_index, tiles_per_seq)
            p32 = p32.reshape(N_P32, n_b, n_l, HEAD_DIM)
            p16 = p16.reshape(N_P16, n_b, n_l, HEAD_DIM)
            n_seqs = max(1, min(n_b, SCAN_ROWS_PER_STEP // n_l))
            mix_a, st_a = _hgrn(p32, p16, gain_hgrn, s0_h, layer, emit, n_seqs)
            mix_b, st_b = _ret(p32, p16, ret_decay_logit, gain_ret, s0_r, layer, emit, n_seqs)
            grp[0] = _out_ffn(
                mix_a.reshape(N_HEADS, n_b * n_l, HEAD_DIM), mix_b.reshape(N_HEADS, n_b * n_l, HEAD_DIM), x, mod,
                gain_ffn, w_out16, w_up16, w_down16, final_gain, layer, tile_m, cond_index,
            )
            if emit:
                new_h.append(st_a)
                new_r.append(st_b)
    y_prompt = groups[0][0].reshape(batch, seq, d_model).astype(x_prompt.dtype)
    y_sample = groups[1][0].reshape(dec_batch, dec_seq, d_model).astype(x_sample.dtype)
    new_state_hgrn = jnp.stack(new_h, axis=1).astype(state_hgrn.dtype)
    new_state_ret = jnp.stack(new_r, axis=1).astype(state_ret.dtype)
    return (y_prompt, y_sample, new_state_hgrn, new_state_ret)
```

```python
import functools

import jax
import jax.numpy as jnp
from jax import lax
from jax.experimental import pallas as pl
from jax.experimental.pallas import tpu as pltpu

F32 = jnp.float32
BF16 = jnp.bfloat16

N_HEADS = 4
HEAD_DIM = 128
N_SECTIONS = 9
SECTION = N_HEADS * HEAD_DIM
D_IN = N_SECTIONS * SECTION
GRID_W = 64
ROPE_BASE = 10000.0
EPS = 1e-6
GATE_FLOOR = 1e-12

P32_QA, P32_KF, P32_KB, P32_GATE_A, P32_GATE_R = range(5)
P16_GF_HI, P16_GF_LO, P16_GB_HI, P16_GB_LO, P16_VA, P16_QR, P16_KR, P16_VR = range(8)
N_P32 = 5 * N_HEADS
N_P16 = 8 * N_HEADS

N_MOD = 6
COND_ROWS = 16

HGRN_CHUNK = 64
HGRN_BLOCK = 256
RET_CHUNK = 128
RET_BLOCK = 256
SCAN_ROWS_PER_STEP = 2048
SAFE_EXP_RANGE = 80.0
SAFE_Q_MAX = 1e3
FFN_CHUNK = 256
MOD_TILE_N = 1536
VMEM_LIMIT = 56 * 1024 * 1024


def _silu(x):
    h = 0.5 * x
    return h + h * jnp.tanh(h)


def _dot(a, b):
    return jnp.dot(a, b, preferred_element_type=F32)


def _dot_nt(a, b):
    return lax.dot_general(a, b, (((1,), (1,)), ((), ())), preferred_element_type=F32)


def _split_bf16(x):
    hi = x.astype(BF16)
    return hi, (x - hi.astype(F32)).astype(BF16)


def _modnorm(x, gain, scale, shift):
    ms = jnp.mean(x * x, axis=-1, keepdims=True)
    return x * lax.rsqrt(ms + EPS) * gain * (1.0 + scale) + shift


def _mod_kernel(cond_ref, w_ref, b_ref, out_ref):
    s_hi, s_lo = _split_bf16(_silu(cond_ref[...]))
    w_hi, w_lo = _split_bf16(w_ref[...])
    s_both = jnp.concatenate([s_hi, s_lo], axis=0)
    p = _dot(s_both, w_hi) + _dot(s_both, w_lo)
    out_ref[...] = p[:COND_ROWS] + p[COND_ROWS:] + b_ref[...]


def _modulation(cond, ada_w, ada_b):
    depth, d_model, n_out = ada_w.shape
    out = pl.pallas_call(
        _mod_kernel,
        grid=(depth, n_out // MOD_TILE_N),
        in_specs=[
            pl.BlockSpec((COND_ROWS, d_model), lambda l, j: (0, 0)),
            pl.BlockSpec((None, d_model, MOD_TILE_N), lambda l, j: (l, 0, j)),
            pl.BlockSpec((None, 1, MOD_TILE_N), lambda l, j: (l, 0, j)),
        ],
        out_specs=pl.BlockSpec((None, COND_ROWS, MOD_TILE_N), lambda l, j: (l, 0, j)),
        out_shape=jax.ShapeDtypeStruct((depth, COND_ROWS, n_out), F32),
        compiler_params=pltpu.CompilerParams(vmem_limit_bytes=VMEM_LIMIT),
        name="adaln_mod",
    )(cond, ada_w, ada_b.reshape(depth, 1, n_out))
    return out.reshape(depth, COND_ROWS, N_MOD, d_model)


def _rope(x, cos, sin_signed):
    lane = lax.broadcasted_iota(jnp.int32, x.shape, 1)
    quarter = HEAD_DIM // 4
    partner = jnp.where(lane % (2 * quarter) < quarter, pltpu.roll(x, HEAD_DIM - quarter, 1), pltpu.roll(x, quarter, 1))
    return x * cos + partner * sin_signed


def _hgrn_lower_bounds(logits, layer):
    e = jnp.exp(logits - jnp.max(logits, axis=0, keepdims=True))
    p = e / jnp.sum(e, axis=0, keepdims=True)
    cum = p[0]
    for i in range(1, layer + 1):
        cum = cum + p[i]
    return jnp.clip(cum - p[0], 0.0, 1.0)


def _in_proj_kernel(*refs, layer, use_rope):
    x_ref, mod_ref, gain_ref, w_ref, lbl_ref = refs[:5]
    refs = refs[5:]
    if use_rope:
        cos_ref, sin_ref, refs = refs[0], refs[1], refs[2:]
    p32_ref, p16_ref = refs

    mod = mod_ref[...]
    h = _modnorm(x_ref[...], gain_ref[...], mod[1:2], mod[0:1]).astype(BF16)
    lb = _hgrn_lower_bounds(lbl_ref[...], layer)

    def proj(j):
        return _dot(h, w_ref[:, j * SECTION : (j + 1) * SECTION])

    def put(ref, section, val):
        for hd in range(N_HEADS):
            ref[section * N_HEADS + hd] = val[:, hd * HEAD_DIM : (hd + 1) * HEAD_DIM].astype(ref.dtype)

    put(p32_ref, P32_QA, _silu(proj(0)))
    for d, (sec_hi, sec_lo, sec_k) in enumerate(
        ((P16_GF_HI, P16_GF_LO, P32_KF), (P16_GB_HI, P16_GB_LO, P32_KB))
    ):
        lb_d = lb[d : d + 1]
        c0, c1 = 0.5 + 0.5 * lb_d, 0.5 - 0.5 * lb_d
        ct = c1 * jnp.tanh(0.5 * proj(1 + d))
        g_hi, g_lo = _split_bf16(jnp.log(jnp.maximum(c0 + ct, GATE_FLOOR)))
        put(p16_ref, sec_hi, g_hi)
        put(p16_ref, sec_lo, g_lo)
        put(p32_ref, sec_k, c1 - ct)
    put(p16_ref, P16_VA, proj(3))
    put(p32_ref, P32_GATE_A, _silu(proj(4)))
    q = proj(5)
    k = proj(6) * (HEAD_DIM ** -0.5)
    if use_rope:
        cos, sin = cos_ref[...], sin_ref[...]
        heads = [slice(hd * HEAD_DIM, (hd + 1) * HEAD_DIM) for hd in range(N_HEADS)]
        q = jnp.concatenate([_rope(q[:, sl], cos, sin) for sl in heads], axis=1)
        k = jnp.concatenate([_rope(k[:, sl], cos, sin) for sl in heads], axis=1)
    put(p16_ref, P16_QR, q)
    put(p16_ref, P16_KR, k)
    put(p16_ref, P16_VR, proj(7))
    put(p32_ref, P32_GATE_R, _silu(proj(8)))


def _in_proj(x, mod, gain, w, lb_logits, rope_tables, layer, tile_m, cond_index, tiles_per_seq):
    n_tok, d_model = x.shape
    depth = lb_logits.shape[0]
    use_rope = rope_tables is not None
    in_specs = [
        pl.BlockSpec((tile_m, d_model), lambda i: (i, 0)),
        pl.BlockSpec((None, None, N_MOD, d_model), lambda i: (layer, cond_index(i), 0, 0)),
        pl.BlockSpec((None, 1, d_model), lambda i: (layer, 0, 0)),
        pl.BlockSpec((None, d_model, D_IN), lambda i: (layer, 0, 0), pipeline_mode=pl.Buffered(1)),
        pl.BlockSpec((depth, 2, SECTION), lambda i: (0, 0, 0)),
    ]
    args = [x, mod, gain, w, lb_logits]
    if use_rope:
        in_specs += [pl.BlockSpec((tile_m, HEAD_DIM), lambda i: (i % tiles_per_seq, 0))] * 2
        args += list(rope_tables)
    return pl.pallas_call(
        functools.partial(_in_proj_kernel, layer=layer, use_rope=use_rope),
        grid=(n_tok // tile_m,),
        in_specs=in_specs,
        out_specs=[
            pl.BlockSpec((N_P32, tile_m, HEAD_DIM), lambda i: (0, i, 0)),
            pl.BlockSpec((N_P16, tile_m, HEAD_DIM), lambda i: (0, i, 0)),
        ],
        out_shape=[
            jax.ShapeDtypeStruct((N_P32, n_tok, HEAD_DIM), F32),
            jax.ShapeDtypeStruct((N_P16, n_tok, HEAD_DIM), BF16),
        ],
        compiler_params=pltpu.CompilerParams(vmem_limit_bytes=VMEM_LIMIT),
        name="in_proj",
    )(*args)


def _rope_tables(seq_len):
    quarter = HEAD_DIM // 4
    inv = ROPE_BASE ** (-jnp.arange(quarter, dtype=F32) / quarter)
    pos = jnp.arange(seq_len, dtype=jnp.int32)
    ang_r = (pos // GRID_W).astype(F32)[:, None] * inv
    ang_c = (pos % GRID_W).astype(F32)[:, None] * inv
    cos = jnp.concatenate([jnp.cos(ang_r), jnp.cos(ang_r), jnp.cos(ang_c), jnp.cos(ang_c)], axis=1)
    sin = jnp.concatenate([-jnp.sin(ang_r), jnp.sin(ang_r), -jnp.sin(ang_c), jnp.sin(ang_c)], axis=1)
    return cos, sin


def _rows(start, size):
    return pl.ds(start if isinstance(start, int) else pl.multiple_of(start, size), size)


def _skewed_stages(n_steps, produce, consume, steps_per_trip=2):
    steps_per_trip = min(steps_per_trip, n_steps)
    assert steps_per_trip % 2 == 0 and n_steps % steps_per_trip == 0
    produce(0, 0)
    if steps_per_trip == n_steps:
        for j in range(n_steps):
            if j + 1 < n_steps:
                produce(j + 1, (j + 1) % 2)
            consume(j, j % 2)
        return

    def body(trip, carry):
        first = steps_per_trip * trip
        for j in range(steps_per_trip):
            produce(jnp.minimum(first + j + 1, n_steps - 1), (j + 1) % 2)
            consume(first + j, j % 2)
        return carry

    lax.fori_loop(0, n_steps // steps_per_trip, body, 0)


def _state_increment(v16, k_cat):
    return _dot(v16.astype(F32).T.astype(BF16), k_cat)


def _run_states(n_chunks, first, dec_of, u_scr, st_scr, s_f, s_b):
    def step(i, carry):
        s_f, s_b = carry
        cf, cb = first + i, first + n_chunks - 1 - i
        st_scr[cf, :, :HEAD_DIM] = s_f.astype(BF16)
        st_scr[cb, :, HEAD_DIM:] = s_b.astype(BF16)
        s_f = dec_of(cf)[:, :HEAD_DIM] * s_f + u_scr[cf, :, :HEAD_DIM]
        s_b = dec_of(cb)[:, HEAD_DIM:] * s_b + u_scr[cb, :, HEAD_DIM:]
        return s_f, s_b

    return lax.fori_loop(0, n_chunks, step, (s_f, s_b), unroll=2)


def _scan_states(n_chunks, n_seqs, dec_of, u_scr, st_scr, s0_ref, st_out_ref):
    per_seq = n_chunks // n_seqs
    for s in range(n_seqs):
        if s0_ref is None:
            s_f = s_b = jnp.zeros((HEAD_DIM, HEAD_DIM), F32)
        else:
            s_f, s_b = s0_ref[s, 0].T, s0_ref[s, 1].T
        s_f, s_b = _run_states(per_seq, s * per_seq, dec_of, u_scr, st_scr, s_f, s_b)
        if st_out_ref is not None:
            st_out_ref[s, 0] = s_f.T
            st_out_ref[s, 1] = s_b.T


def _slab_spec(seq_len, section):
    return pl.BlockSpec((None, None, seq_len, HEAD_DIM), lambda b, h: (section * N_HEADS + h, b, 0, 0))


def _state_in_spec(layer, n_seqs):
    return pl.BlockSpec((n_seqs, None, 2, None, HEAD_DIM, HEAD_DIM), lambda b, h: (b, layer, 0, h, 0, 0))


def _state_out_spec(n_seqs):
    return pl.BlockSpec((n_seqs, 2, None, HEAD_DIM, HEAD_DIM), lambda b, h: (b, 0, h, 0, 0))


def _fold_sequences(slabs, n_seqs):
    n, batch, seq_len, d = slabs.shape
    assert batch % n_seqs == 0
    return slabs.reshape(n, batch // n_seqs, n_seqs * seq_len, d)


def _mix_out(batch, seq_len):
    spec = pl.BlockSpec((None, None, seq_len, HEAD_DIM), lambda b, h: (h, b, 0, 0))
    return spec, jax.ShapeDtypeStruct((N_HEADS, batch, seq_len, HEAD_DIM), BF16)


def _cumsum_rows(g, reverse):
    n = g.shape[0]
    pos = lax.broadcasted_iota(jnp.int32, g.shape, 0)
    sh = 1
    while sh < n:
        if reverse:
            g = g + jnp.where(pos < n - sh, pltpu.roll(g, n - sh, 0), 0.0)
        else:
            g = g + jnp.where(pos >= sh, pltpu.roll(g, sh, 0), 0.0)
        sh *= 2
    return g


def _causal_mask(n, lower):
    t_idx = lax.broadcasted_iota(jnp.int32, (n, n), 0)
    s_idx = lax.broadcasted_iota(jnp.int32, (n, n), 1)
    return t_idx >= s_idx if lower else t_idx <= s_idx


def _hgrn_direct_scores(q, k, b, k_scr, b_scr):
    n = q.shape[0]
    k_scr[...] = k
    b_scr[...] = b
    col = lax.broadcasted_iota(jnp.int32, (n, n), 1)

    def body(s, acc):
        z = q * k_scr[pl.ds(s, 1), :] * jnp.exp(jnp.minimum(b - b_scr[pl.ds(s, 1), :], 0.0))
        return jnp.where(col == s, jnp.sum(z, axis=1, keepdims=True), acc)

    return lax.fori_loop(0, n, body, jnp.zeros((n, n), F32))


def _hgrn_kernel(*refs, seq_len, n_seqs, has_s0, emit_state):
    qa_ref, kf_ref, kb_ref, gate_ref, gf_hi_ref, gf_lo_ref, gb_hi_ref, gb_lo_ref, va_ref, gn_ref = refs[:10]
    refs = refs[10:]
    s0_ref = None
    if has_s0:
        s0_ref, refs = refs[0], refs[1:]
    mix_ref, refs = refs[0], refs[1:]
    st_out_ref = None
    if emit_state:
        st_out_ref, refs = refs[0], refs[1:]
    tri_scr, cum_scr, a16_scr, qe_scr, o_scr, u_scr, dec_scr, st_scr, safe_smem, k_scr, b_scr = refs

    c_len = HGRN_CHUNK
    blk_len = min(HGRN_BLOCK, seq_len)
    n_chunks = seq_len // c_len
    n_blocks = seq_len // blk_len
    chunks_per_blk = blk_len // c_len
    directions = ((kf_ref, gf_hi_ref, gf_lo_ref, False), (kb_ref, gb_hi_ref, gb_lo_ref, True))
    causal = (_causal_mask(c_len, True), _causal_mask(c_len, False))

    @pl.when(jnp.logical_and(pl.program_id(0) == 0, pl.program_id(1) == 0))
    def _():
        t_idx = lax.broadcasted_iota(jnp.int32, (blk_len, blk_len), 0)
        s_idx = lax.broadcasted_iota(jnp.int32, (blk_len, blk_len), 1)
        same_chunk = (t_idx // c_len) == (s_idx // c_len)
        tri_scr[0] = jnp.where(jnp.logical_and(same_chunk, s_idx <= t_idx), 1.0, 0.0).astype(BF16)
        tri_scr[1] = jnp.where(jnp.logical_and(same_chunk, s_idx >= t_idx), 1.0, 0.0).astype(BF16)

    safe_smem[n_chunks] = 0

    def cumulate(blk, slot):
        rows = _rows(blk * blk_len, blk_len)
        for d, (_, g_hi_ref, g_lo_ref, _) in enumerate(directions):
            cum = _dot(tri_scr[d], jnp.concatenate([g_hi_ref[rows, :], g_lo_ref[rows, :]], axis=1))
            cum_scr[slot, d] = cum[:, :HEAD_DIM] + cum[:, HEAD_DIM:]

    def prepare(blk, slot):
        r0 = blk * blk_len
        rows = _rows(r0, blk_len)
        q = qa_ref[rows, :]
        v16 = va_ref[rows, :]
        q_ok = jnp.max(jnp.abs(q)) <= SAFE_Q_MAX
        gates = [(k_ref[rows, :], cum_scr[slot, d]) for d, (k_ref, _, _, _) in enumerate(directions)]
        for j in range(chunks_per_blk):
            sl = slice(j * c_len, (j + 1) * c_len)
            rows_j = _rows(r0 + j * c_len, c_len)
            c = blk * chunks_per_blk + j
            q_e, k_e, dec, span, scores = [], [], [], None, None
            for d, (k, b) in enumerate(gates):
                k_j, b_j = k[sl], b[sl]
                tot = b_j[0:1] if directions[d][3] else b_j[c_len - 1 : c_len]
                dist = b_j - b_j[c_len // 2 - 1 : c_len // 2]
                a = _dot_nt((q[sl] * jnp.exp(dist)).astype(BF16), (k_j * jnp.exp(-dist)).astype(BF16))
                a = jnp.where(causal[d], a, 0.0)
                scores = a if scores is None else scores + a
                q_e.append(q[sl] * jnp.exp(b_j))
                k_e.append(k_j * jnp.exp(tot - b_j))
                dec.append(jnp.exp(tot))
                ends = jnp.maximum(jnp.abs(dist[0:1]), jnp.abs(dist[c_len - 1 : c_len]))
                span = ends if span is None else jnp.maximum(span, ends)
            a16_scr[rows_j, :] = scores.astype(BF16)
            qe_scr[rows_j, :] = jnp.concatenate(q_e, axis=1).astype(BF16)
            u_scr[c] = _state_increment(v16[sl], jnp.concatenate(k_e, axis=1).astype(BF16))
            dec_scr[c] = jnp.concatenate(dec, axis=1)
            safe = jnp.logical_and(q_ok, jnp.max(span) <= SAFE_EXP_RANGE).astype(jnp.int32)
            safe_smem[c] = safe
            safe_smem[n_chunks] = safe_smem[n_chunks] + (1 - safe)

    _skewed_stages(n_blocks, cumulate, prepare, steps_per_trip=8)

    _scan_states(n_chunks, n_seqs, lambda c: dec_scr[c], u_scr, st_scr, s0_ref, st_out_ref)

    def redo_scores(c, carry):
        @pl.when(safe_smem[c] == 0)
        def _():
            rows = _rows(c * c_len, c_len)
            q = qa_ref[rows, :]
            scores = jnp.zeros((c_len, c_len), F32)
            for d, (k_ref, g_hi_ref, g_lo_ref, rev) in enumerate(directions):
                b = _cumsum_rows(g_hi_ref[rows, :].astype(F32) + g_lo_ref[rows, :].astype(F32), rev)
                scores = scores + jnp.where(causal[d], _hgrn_direct_scores(q, k_ref[rows, :], b, k_scr, b_scr), 0.0)
            a16_scr[rows, :] = scores.astype(BF16)

        return carry

    @pl.when(safe_smem[n_chunks] > 0)
    def _():
        lax.fori_loop(0, n_chunks, redo_scores, 0)

    def combine(blk, slot):
        for j in range(chunks_per_blk):
            c = blk * chunks_per_blk + j
            rows = _rows(c * c_len, c_len)
            o = _dot(a16_scr[rows, :], va_ref[rows, :]) + _dot_nt(qe_scr[rows, :], st_scr[c])
            o_scr[slot, j * c_len : (j + 1) * c_len, :] = o

    gn = gn_ref[...]

    def head_norm_gate(blk, slot):
        rows = _rows(blk * blk_len, blk_len)
        o = o_scr[slot]
        o = o * lax.rsqrt(jnp.mean(o * o, axis=-1, keepdims=True) + EPS) * gn
        mix_ref[rows, :] = (o * gate_ref[rows, :]).astype(BF16)

    _skewed_stages(n_blocks, combine, head_norm_gate, steps_per_trip=8)


def _hgrn(p32, p16, norm_gain, state, layer, emit_state, n_seqs):
    p32, p16 = _fold_sequences(p32, n_seqs), _fold_sequences(p16, n_seqs)
    _, batch, seq_len, _ = p32.shape
    n_chunks = seq_len // HGRN_CHUNK
    blk_len = min(HGRN_BLOCK, seq_len)
    has_s0 = state is not None
    f32_slabs = (P32_QA, P32_KF, P32_KB, P32_GATE_A)
    bf16_slabs = (P16_GF_HI, P16_GF_LO, P16_GB_HI, P16_GB_LO, P16_VA)
    in_specs = [_slab_spec(seq_len, s) for s in f32_slabs + bf16_slabs]
    in_specs.append(pl.BlockSpec((None, 1, HEAD_DIM), lambda b, h: (layer, 0, h)))
    args = [p32] * len(f32_slabs) + [p16] * len(bf16_slabs) + [norm_gain]
    if has_s0:
        in_specs.append(_state_in_spec(layer, n_seqs))
        args.append(state)
    mix_spec, mix_shape = _mix_out(batch, seq_len)
    out_specs, out_shape = [mix_spec], [mix_shape]
    if emit_state:
        out_specs.append(_state_out_spec(n_seqs))
        out_shape.append(jax.ShapeDtypeStruct((batch * n_seqs, 2, N_HEADS, HEAD_DIM, HEAD_DIM), F32))
    outs = pl.pallas_call(
        functools.partial(_hgrn_kernel, seq_len=seq_len, n_seqs=n_seqs, has_s0=has_s0, emit_state=emit_state),
        grid=(batch, N_HEADS),
        in_specs=in_specs,
        out_specs=out_specs,
        out_shape=out_shape,
        scratch_shapes=[
            pltpu.VMEM((2, blk_len, blk_len), BF16),
            pltpu.VMEM((2, 2, blk_len, HEAD_DIM), F32),
            pltpu.VMEM((seq_len, HGRN_CHUNK), BF16),
            pltpu.VMEM((seq_len, 2 * HEAD_DIM), BF16),
            pltpu.VMEM((2, blk_len, HEAD_DIM), F32),
            pltpu.VMEM((n_chunks, HEAD_DIM, 2 * HEAD_DIM), F32),
            pltpu.VMEM((n_chunks, 1, 2 * HEAD_DIM), F32),
            pltpu.VMEM((n_chunks, HEAD_DIM, 2 * HEAD_DIM), BF16),
            pltpu.SMEM((n_chunks + 1,), jnp.int32),
            pltpu.VMEM((HGRN_CHUNK, HEAD_DIM), F32),
            pltpu.VMEM((HGRN_CHUNK, HEAD_DIM), F32),
        ],
        compiler_params=pltpu.CompilerParams(vmem_limit_bytes=VMEM_LIMIT),
        name="hgrn_scan",
    )(*args)
    return (outs[0], outs[1]) if emit_state else (outs[0], None)


def _log_sigmoid(x):
    return jnp.minimum(x, 0.0) - jnp.log1p(jnp.exp(-jnp.abs(x)))


def _ret_kernel(*refs, layer, seq_len, n_seqs, has_s0, emit_state):
    logit_ref, q_ref, k_ref, v_ref, gate_ref, gn_ref = refs[:6]
    refs = refs[6:]
    s0_ref = None
    if has_s0:
        s0_ref, refs = refs[0], refs[1:]
    mix_ref, refs = refs[0], refs[1:]
    st_out_ref = None
    if emit_state:
        st_out_ref, refs = refs[0], refs[1:]
    a16_scr, o_scr, u_scr, st_scr = refs

    c_len = RET_CHUNK
    n_chunks = seq_len // c_len
    head = pl.program_id(1)
    lg_f = _log_sigmoid(jnp.full((1, HEAD_DIM), logit_ref[layer, 0, head], F32))
    lg_b = _log_sigmoid(jnp.full((1, HEAD_DIM), logit_ref[layer, 1, head], F32))

    t = lax.broadcasted_iota(jnp.int32, (c_len, HEAD_DIM), 0).astype(F32)
    q_dec_f = jnp.exp((t + 1.0) * lg_f)
    q_dec_b = jnp.exp((c_len - t) * lg_b)
    k_dec = jnp.concatenate([jnp.exp((c_len - 1.0 - t) * lg_f), jnp.exp(t * lg_b)], axis=1)
    chunk_dec = jnp.concatenate([jnp.exp(c_len * lg_f), jnp.exp(c_len * lg_b)], axis=1)
    t_idx = lax.broadcasted_iota(jnp.int32, (c_len, c_len), 0)
    s_idx = lax.broadcasted_iota(jnp.int32, (c_len, c_len), 1)
    dist = (t_idx - s_idx).astype(F32)
    decay = jnp.where(t_idx >= s_idx, jnp.exp(jnp.maximum(dist, 0.0) * lg_f), 0.0)
    decay = decay + jnp.where(t_idx <= s_idx, jnp.exp(jnp.maximum(-dist, 0.0) * lg_b), 0.0)

    def prepare(c, carry):
        rows = _rows(c * c_len, c_len)
        k16 = k_ref[rows, :]
        a16_scr[rows, :] = (_dot_nt(q_ref[rows, :], k16) * decay).astype(BF16)
        k = k16.astype(F32)
        u_scr[c] = _state_increment(v_ref[rows, :], (jnp.concatenate([k, k], axis=1) * k_dec).astype(BF16))
        return carry

    lax.fori_loop(0, n_chunks, prepare, 0, unroll=min(16, n_chunks))

    _scan_states(n_chunks, n_seqs, lambda c: chunk_dec, u_scr, st_scr, s0_ref, st_out_ref)

    blk_len = min(RET_BLOCK, seq_len)
    chunks_per_blk = blk_len // c_len

    def combine(blk, slot):
        for j in range(chunks_per_blk):
            c = blk * chunks_per_blk + j
            rows = _rows(c * c_len, c_len)
            q16 = q_ref[rows, :]
            o = _dot(a16_scr[rows, :], v_ref[rows, :])
            o = o + q_dec_f * _dot_nt(q16, st_scr[c, :, :HEAD_DIM]) + q_dec_b * _dot_nt(q16, st_scr[c, :, HEAD_DIM:])
            o_scr[slot, j * c_len : (j + 1) * c_len, :] = o

    gn = gn_ref[...]

    def head_norm_gate(blk, slot):
        rows = _rows(blk * blk_len, blk_len)
        o = o_scr[slot]
        o = o - jnp.mean(o, axis=-1, keepdims=True)
        o = o * lax.rsqrt(jnp.mean(o * o, axis=-1, keepdims=True) + EPS) * gn
        mix_ref[rows, :] = (o * gate_ref[rows, :]).astype(BF16)

    _skewed_stages(seq_len // blk_len, combine, head_norm_gate, steps_per_trip=8)


def _ret(p32, p16, decay_logit, norm_gain, state, layer, emit_state, n_seqs):
    p32, p16 = _fold_sequences(p32, n_seqs), _fold_sequences(p16, n_seqs)
    _, batch, seq_len, _ = p32.shape
    n_chunks = seq_len // RET_CHUNK
    has_s0 = state is not None
    in_specs = [pl.BlockSpec(memory_space=pltpu.SMEM)]
    in_specs += [_slab_spec(seq_len, s) for s in (P16_QR, P16_KR, P16_VR, P32_GATE_R)]
    in_specs.append(pl.BlockSpec((None, 1, HEAD_DIM), lambda b, h: (layer, 0, h)))
    args = [decay_logit, p16, p16, p16, p32, norm_gain]
    if has_s0:
        in_specs.append(_state_in_spec(layer, n_seqs))
        args.append(state)
    mix_spec, mix_shape = _mix_out(batch, seq_len)
    out_specs, out_shape = [mix_spec], [mix_shape]
    if emit_state:
        out_specs.append(_state_out_spec(n_seqs))
        out_shape.append(jax.ShapeDtypeStruct((batch * n_seqs, 2, N_HEADS, HEAD_DIM, HEAD_DIM), F32))
    outs = pl.pallas_call(
        functools.partial(
            _ret_kernel, layer=layer, seq_len=seq_len, n_seqs=n_seqs, has_s0=has_s0, emit_state=emit_state
        ),
        grid=(batch, N_HEADS),
        in_specs=in_specs,
        out_specs=out_specs,
        out_shape=out_shape,
        scratch_shapes=[
            pltpu.VMEM((seq_len, RET_CHUNK), BF16),
            pltpu.VMEM((2, min(RET_BLOCK, seq_len), HEAD_DIM), F32),
            pltpu.VMEM((n_chunks, HEAD_DIM, 2 * HEAD_DIM), F32),
            pltpu.VMEM((n_chunks, HEAD_DIM, 2 * HEAD_DIM), BF16),
        ],
        compiler_params=pltpu.CompilerParams(vmem_limit_bytes=VMEM_LIMIT),
        name="ret_scan",
    )(*args)
    return (outs[0], outs[1]) if emit_state else (outs[0], None)


def _out_ffn_kernel(*refs, d_ff, final):
    ma_ref, mb_ref, x_ref, mod_ref, gain_ref, wo_ref, wu_ref, wd_ref = refs[:8]
    refs = refs[8:]
    if final:
        gfin_ref, refs = refs[0], refs[1:]
    out_ref, act_scr = refs

    mod = mod_ref[...]
    mix = jnp.concatenate([ma_ref[hd] for hd in range(N_HEADS)] + [mb_ref[hd] for hd in range(N_HEADS)], axis=1)
    x1 = x_ref[...] + mod[2:3] * _dot(mix, wo_ref[...])
    h = _modnorm(x1, gain_ref[...], mod[4:5], mod[3:4]).astype(BF16)
    for c in range(d_ff // FFN_CHUNK):
        lo = c * FFN_CHUNK
        gate = _dot(h, wu_ref[:, lo : lo + FFN_CHUNK])
        up = _dot(h, wu_ref[:, d_ff + lo : d_ff + lo + FFN_CHUNK])
        act_scr[:, lo : lo + FFN_CHUNK] = (_silu(gate) * up).astype(BF16)
    x2 = x1 + mod[5:6] * _dot(act_scr[...], wd_ref[...])
    if final:
        x2 = x2 * lax.rsqrt(jnp.mean(x2 * x2, axis=-1, keepdims=True) + EPS) * gfin_ref[...]
    out_ref[...] = x2


def _out_ffn(mix_a, mix_b, x, mod, gain, w_out, w_up, w_down, final_gain, layer, tile_m, cond_index):
    n_tok, d_model = x.shape
    d_ff = w_down.shape[1]
    final = final_gain is not None
    const = lambda i: (layer, 0, 0)
    in_specs = [
        pl.BlockSpec((N_HEADS, tile_m, HEAD_DIM), lambda i: (0, i, 0)),
        pl.BlockSpec((N_HEADS, tile_m, HEAD_DIM), lambda i: (0, i, 0)),
        pl.BlockSpec((tile_m, d_model), lambda i: (i, 0)),
        pl.BlockSpec((None, None, N_MOD, d_model), lambda i: (layer, cond_index(i), 0, 0)),
        pl.BlockSpec((None, 1, d_model), const),
        pl.BlockSpec((None, 2 * N_HEADS * HEAD_DIM, d_model), const, pipeline_mode=pl.Buffered(1)),
        pl.BlockSpec((None, d_model, 2 * d_ff), const, pipeline_mode=pl.Buffered(1)),
        pl.BlockSpec((None, d_ff, d_model), const, pipeline_mode=pl.Buffered(1)),
    ]
    args = [mix_a, mix_b, x, mod, gain, w_out, w_up, w_down]
    if final:
        in_specs.append(pl.BlockSpec((1, d_model), lambda i: (0, 0)))
        args.append(final_gain)
    return pl.pallas_call(
        functools.partial(_out_ffn_kernel, d_ff=d_ff, final=final),
        grid=(n_tok // tile_m,),
        in_specs=in_specs,
        out_specs=pl.BlockSpec((tile_m, d_model), lambda i: (i, 0)),
        out_shape=jax.ShapeDtypeStruct((n_tok, d_model), F32),
        scratch_shapes=[pltpu.VMEM((tile_m, d_ff), BF16)],
        compiler_params=pltpu.CompilerParams(vmem_limit_bytes=VMEM_LIMIT),
        name="out_ffn",
    )(*args)


def kernel(x_prompt, x_sample, state_hgrn, state_ret, c, c_ctx, ada_w, ada_b, norm_mix, norm_ffn, w_in,
           hgrn_lb_logits, hgrn_norm, ret_decay_logit, ret_norm, w_out, w_up, w_down, norm_final):
    batch, seq, d_model = x_prompt.shape
    dec_batch, dec_seq, _ = x_sample.shape
    depth = ada_w.shape[0]
    tile_m = 512
    assert w_in.shape[2] == D_IN and state_hgrn.shape[3] == N_HEADS and state_ret.shape[3] == N_HEADS
    assert dec_batch < COND_ROWS and dec_seq % tile_m == 0 and (batch * seq) % tile_m == 0
    assert dec_seq % GRID_W == 0
    for n_l in (seq, dec_seq):
        assert n_l % min(HGRN_BLOCK, n_l) == 0 and min(HGRN_BLOCK, n_l) % HGRN_CHUNK == 0
        assert n_l % min(RET_BLOCK, n_l) == 0 and n_l % RET_CHUNK == 0

    cond = jnp.concatenate(
        [c.astype(F32), c_ctx.astype(F32)[None, :], jnp.zeros((COND_ROWS - dec_batch - 1, d_model), F32)], axis=0
    )
    mod = _modulation(cond, ada_w, ada_b)

    w_in16, w_out16, w_up16, w_down16 = (w.astype(BF16) for w in (w_in, w_out, w_up, w_down))
    gain_mix = norm_mix.reshape(depth, 1, d_model)
    gain_ffn = norm_ffn.reshape(depth, 1, d_model)
    gain_hgrn = hgrn_norm.reshape(depth, 1, -1)
    gain_ret = ret_norm.reshape(depth, 1, -1)
    gain_final = norm_final.reshape(1, d_model)
    rope_tables = _rope_tables(dec_seq)

    tiles_per_seq = dec_seq // tile_m
    groups = [
        [x_prompt.astype(F32).reshape(batch * seq, d_model), batch, seq, lambda i: dec_batch, None, None, None, True],
        [x_sample.astype(F32).reshape(dec_batch * dec_seq, d_model), dec_batch, dec_seq,
         lambda i: i // tiles_per_seq, state_hgrn, state_ret, rope_tables, False],
    ]
    new_h, new_r = [], []
    for layer in range(depth):
        final_gain = gain_final if layer == depth - 1 else None
        for grp in groups:
            x, n_b, n_l, cond_index, s0_h, s0_r, tables, emit = grp
            p32, p16 = _in_proj(x, mod, gain_mix, w_in16, hgrn_lb_logits, tables, layer, tile_m, cond_index, tiles_per_seq)
            p32 = p32.reshape(N_P32, n_b, n_l, HEAD_DIM)
            p16 = p16.reshape(N_P16, n_b, n_l, HEAD_DIM)
            n_seqs = max(1, min(n_b, SCAN_ROWS_PER_STEP // n_l))
            mix_a, st_a = _hgrn(p32, p16, gain_hgrn, s0_h, layer, emit, n_seqs)
            mix_b, st_b = _ret(p32, p16, ret_decay_logit, gain_ret, s0_r, layer, emit, n_seqs)
            grp[0] = _out_ffn(
                mix_a.reshape(N_HEADS, n_b * n_l, HEAD_DIM), mix_b.reshape(N_HEADS, n_b * n_l, HEAD_DIM), x, mod,
                gain_ffn, w_out16, w_up16, w_down16, final_gain, layer, tile_m, cond_index,
            )
            if emit:
                new_h.append(st_a)
                new_r.append(st_b)
    y_prompt = groups[0][0].reshape(batch, seq, d_model).astype(x_prompt.dtype)
    y_sample = groups[1][0].reshape(dec_batch, dec_seq, d_model).astype(x_sample.dtype)
    new_state_hgrn = jnp.stack(new_h, axis=1).astype(state_hgrn.dtype)
    new_state_ret = jnp.stack(new_r, axis=1).astype(state_ret.dtype)
    return (y_prompt, y_sample, new_state_hgrn, new_state_ret)
```

```python
import functools

import jax
import jax.numpy as jnp
from jax import lax
from jax.experimental import pallas as pl
from jax.experimental.pallas import tpu as pltpu

F32 = jnp.float32
BF16 = jnp.bfloat16

N_HEADS = 4
HEAD_DIM = 128
N_SECTIONS = 9
SECTION = N_HEADS * HEAD_DIM
D_IN = N_SECTIONS * SECTION
GRID_W = 64
ROPE_BASE = 10000.0
EPS = 1e-6
GATE_FLOOR = 1e-12

P32_QA, P32_KF, P32_KB, P32_GATE_A, P32_GATE_R = range(5)
P16_GF_HI, P16_GF_LO, P16_GB_HI, P16_GB_LO, P16_VA, P16_QR, P16_KR, P16_VR = range(8)
N_P32 = 5 * N_HEADS
N_P16 = 8 * N_HEADS

N_MOD = 6
COND_ROWS = 16

HGRN_CHUNK = 64
HGRN_BLOCK = 256
RET_CHUNK = 128
RET_BLOCK = 256
SCAN_ROWS_PER_STEP = 2048
SAFE_EXP_RANGE = 80.0
SAFE_Q_MAX = 1e3
FFN_CHUNK = 256
MOD_TILE_N = 1536
VMEM_LIMIT = 56 * 1024 * 1024


def _silu(x):
    h = 0.5 * x
    return h + h * jnp.tanh(h)


def _dot(a, b):
    return jnp.dot(a, b, preferred_element_type=F32)


def _dot_nt(a, b):
    return lax.dot_general(a, b, (((1,), (1,)), ((), ())), preferred_element_type=F32)


def _split_bf16(x):
    hi = x.astype(BF16)
    return hi, (x - hi.astype(F32)).astype(BF16)


def _modnorm(x, gain, scale, shift):
    ms = jnp.mean(x * x, axis=-1, keepdims=True)
    return x * lax.rsqrt(ms + EPS) * gain * (1.0 + scale) + shift


def _mod_kernel(cond_ref, w_ref, b_ref, out_ref):
    s_hi, s_lo = _split_bf16(_silu(cond_ref[...]))
    w_hi, w_lo = _split_bf16(w_ref[...])
    s_both = jnp.concatenate([s_hi, s_lo], axis=0)
    p = _dot(s_both, w_hi) + _dot(s_both, w_lo)
    out_ref[...] = p[:COND_ROWS] + p[COND_ROWS:] + b_ref[...]


def _modulation(cond, ada_w, ada_b):
    depth, d_model, n_out = ada_w.shape
    out = pl.pallas_call(
        _mod_kernel,
        grid=(depth, n_out // MOD_TILE_N),
        in_specs=[
            pl.BlockSpec((COND_ROWS, d_model), lambda l, j: (0, 0)),
            pl.BlockSpec((None, d_model, MOD_TILE_N), lambda l, j: (l, 0, j)),
            pl.BlockSpec((None, 1, MOD_TILE_N), lambda l, j: (l, 0, j)),
        ],
        out_specs=pl.BlockSpec((None, COND_ROWS, MOD_TILE_N), lambda l, j: (l, 0, j)),
        out_shape=jax.ShapeDtypeStruct((depth, COND_ROWS, n_out), F32),
        compiler_params=pltpu.CompilerParams(vmem_limit_bytes=VMEM_LIMIT),
        name="adaln_mod",
    )(cond, ada_w, ada_b.reshape(depth, 1, n_out))
    return out.reshape(depth, COND_ROWS, N_MOD, d_model)


def _rope(x, cos, sin_signed):
    lane = lax.broadcasted_iota(jnp.int32, x.shape, 1)
    quarter = HEAD_DIM // 4
    partner = jnp.where(lane % (2 * quarter) < quarter, pltpu.roll(x, HEAD_DIM - quarter, 1), pltpu.roll(x, quarter, 1))
    return x * cos + partner * sin_signed


def _hgrn_lower_bounds(logits, layer):
    e = jnp.exp(logits - jnp.max(logits, axis=0, keepdims=True))
    p = e / jnp.sum(e, axis=0, keepdims=True)
    cum = p[0]
    for i in range(1, layer + 1):
        cum = cum + p[i]
    return jnp.clip(cum - p[0], 0.0, 1.0)


def _in_proj_kernel(*refs, layer, use_rope):
    x_ref, mod_ref, gain_ref, w_ref, lbl_ref = refs[:5]
    refs = refs[5:]
    if use_rope:
        cos_ref, sin_ref, refs = refs[0], refs[1], refs[2:]
    p32_ref, p16_ref = refs

    mod = mod_ref[...]
    h = _modnorm(x_ref[...], gain_ref[...], mod[1:2], mod[0:1]).astype(BF16)
    lb = _hgrn_lower_bounds(lbl_ref[...], layer)

    def proj(j):
        return _dot(h, w_ref[:, j * SECTION : (j + 1) * SECTION])

    def put(ref, section, val):
        for hd in range(N_HEADS):
            ref[section * N_HEADS + hd] = val[:, hd * HEAD_DIM : (hd + 1) * HEAD_DIM].astype(ref.dtype)

    put(p32_ref, P32_QA, _silu(proj(0)))
    for d, (sec_hi, sec_lo, sec_k) in enumerate(
        ((P16_GF_HI, P16_GF_LO, P32_KF), (P16_GB_HI, P16_GB_LO, P32_KB))
    ):
        lb_d = lb[d : d + 1]
        c0, c1 = 0.5 + 0.5 * lb_d, 0.5 - 0.5 * lb_d
        ct = c1 * jnp.tanh(0.5 * proj(1 + d))
        g_hi, g_lo = _split_bf16(jnp.log(jnp.maximum(c0 + ct, GATE_FLOOR)))
        put(p16_ref, sec_hi, g_hi)
        put(p16_ref, sec_lo, g_lo)
        put(p32_ref, sec_k, c1 - ct)
    put(p16_ref, P16_VA, proj(3))
    put(p32_ref, P32_GATE_A, _silu(proj(4)))
    q = proj(5)
    k = proj(6) * (HEAD_DIM ** -0.5)
    if use_rope:
        cos, sin = cos_ref[...], sin_ref[...]
        heads = [slice(hd * HEAD_DIM, (hd + 1) * HEAD_DIM) for hd in range(N_HEADS)]
        q = jnp.concatenate([_rope(q[:, sl], cos, sin) for sl in heads], axis=1)
        k = jnp.concatenate([_rope(k[:, sl], cos, sin) for sl in heads], axis=1)
    put(p16_ref, P16_QR, q)
    put(p16_ref, P16_KR, k)
    put(p16_ref, P16_VR, proj(7))
    put(p32_ref, P32_GATE_R, _silu(proj(8)))


def _in_proj(x, mod, gain, w, lb_logits, rope_tables, layer, tile_m, cond_index, tiles_per_seq):
    n_tok, d_model = x.shape
    depth = lb_logits.shape[0]
    use_rope = rope_tables is not None
    in_specs = [
        pl.BlockSpec((tile_m, d_model), lambda i: (i, 0)),
        pl.BlockSpec((None, None, N_MOD, d_model), lambda i: (layer, cond_index(i), 0, 0)),
        pl.BlockSpec((None, 1, d_model), lambda i: (layer, 0, 0)),
        pl.BlockSpec((None, d_model, D_IN), lambda i: (layer, 0, 0), pipeline_mode=pl.Buffered(1)),
        pl.BlockSpec((depth, 2, SECTION), lambda i: (0, 0, 0)),
    ]
    args = [x, mod, gain, w, lb_logits]
    if use_rope:
        in_specs += [pl.BlockSpec((tile_m, HEAD_DIM), lambda i: (i % tiles_per_seq, 0))] * 2
        args += list(rope_tables)
    return pl.pallas_call(
        functools.partial(_in_proj_kernel, layer=layer, use_rope=use_rope),
        grid=(n_tok // tile_m,),
        in_specs=in_specs,
        out_specs=[
            pl.BlockSpec((N_P32, tile_m, HEAD_DIM), lambda i: (0, i, 0)),
            pl.BlockSpec((N_P16, tile_m, HEAD_DIM), lambda i: (0, i, 0)),
        ],
        out_shape=[
            jax.ShapeDtypeStruct((N_P32, n_tok, HEAD_DIM), F32),
            jax.ShapeDtypeStruct((N_P16, n_tok, HEAD_DIM), BF16),
        ],
        compiler_params=pltpu.CompilerParams(vmem_limit_bytes=VMEM_LIMIT),
        name="in_proj",
    )(*args)


def _rope_tables(seq_len):
    quarter = HEAD_DIM // 4
    inv = ROPE_BASE ** (-jnp.arange(quarter, dtype=F32) / quarter)
    pos = jnp.arange(seq_len, dtype=jnp.int32)
    ang_r = (pos // GRID_W).astype(F32)[:, None] * inv
    ang_c = (pos % GRID_W).astype(F32)[:, None] * inv
    cos = jnp.concatenate([jnp.cos(ang_r), jnp.cos(ang_r), jnp.cos(ang_c), jnp.cos(ang_c)], axis=1)
    sin = jnp.concatenate([-jnp.sin(ang_r), jnp.sin(ang_r), -jnp.sin(ang_c), jnp.sin(ang_c)], axis=1)
    return cos, sin


def _rows(start, size):
    return pl.ds(start if isinstance(start, int) else pl.multiple_of(start, size), size)


def _skewed_stages(n_steps, produce, consume, steps_per_trip=2):
    steps_per_trip = min(steps_per_trip, n_steps)
    assert steps_per_trip % 2 == 0 and n_steps % steps_per_trip == 0
    produce(0, 0)
    if steps_per_trip == n_steps:
        for j in range(n_steps):
            if j + 1 < n_steps:
                produce(j + 1, (j + 1) % 2)
            consume(j, j % 2)
        return

    def body(trip, carry):
        first = steps_per_trip * trip
        for j in range(steps_per_trip):
            produce(jnp.minimum(first + j + 1, n_steps - 1), (j + 1) % 2)
            consume(first + j, j % 2)
        return carry

    lax.fori_loop(0, n_steps // steps_per_trip, body, 0)


def _state_increment(v16, k_cat):
    return _dot(v16.astype(F32).T.astype(BF16), k_cat)


def _run_states(n_chunks, first, dec_of, u_scr, st_scr, s_f, s_b):
    def step(i, carry):
        s_f, s_b = carry
        cf, cb = first + i, first + n_chunks - 1 - i
        st_scr[cf, :, :HEAD_DIM] = s_f.astype(BF16)
        st_scr[cb, :, HEAD_DIM:] = s_b.astype(BF16)
        s_f = dec_of(cf)[:, :HEAD_DIM] * s_f + u_scr[cf, :, :HEAD_DIM]
        s_b = dec_of(cb)[:, HEAD_DIM:] * s_b + u_scr[cb, :, HEAD_DIM:]
        return s_f, s_b

    return lax.fori_loop(0, n_chunks, step, (s_f, s_b), unroll=2)


def _scan_states(n_chunks, n_seqs, dec_of, u_scr, st_scr, s0_ref, earlier_ref, st_out_ref):
    per_seq = n_chunks // n_seqs
    n_earlier = 0
    if earlier_ref is not None:
        n_earlier = earlier_ref.shape[1]
        st_out_ref[:, :n_earlier] = earlier_ref[...]
    for s in range(n_seqs):
        if s0_ref is None:
            s_f = s_b = jnp.zeros((HEAD_DIM, HEAD_DIM), F32)
        else:
            s_f, s_b = s0_ref[s, 0].T, s0_ref[s, 1].T
        s_f, s_b = _run_states(per_seq, s * per_seq, dec_of, u_scr, st_scr, s_f, s_b)
        if st_out_ref is not None:
            st_out_ref[s, n_earlier, 0] = s_f.T
            st_out_ref[s, n_earlier, 1] = s_b.T


def _slab_spec(seq_len, section):
    return pl.BlockSpec((None, None, seq_len, HEAD_DIM), lambda b, h: (section * N_HEADS + h, b, 0, 0))


def _state_in_spec(layer, n_seqs):
    return pl.BlockSpec((n_seqs, None, 2, None, HEAD_DIM, HEAD_DIM), lambda b, h: (b, layer, 0, h, 0, 0))


def _stacked_state_spec(n_seqs, n_layers):
    return pl.BlockSpec((n_seqs, n_layers, 2, None, HEAD_DIM, HEAD_DIM), lambda b, h: (b, 0, 0, h, 0, 0))


def _state_outputs(earlier, n_seqs, batch, in_specs, args, out_specs, out_shape):
    n_layers = 1
    if earlier is not None:
        n_layers += earlier.shape[1]
        in_specs.append(_stacked_state_spec(n_seqs, n_layers - 1))
        args.append(earlier)
    out_specs.append(_stacked_state_spec(n_seqs, n_layers))
    out_shape.append(jax.ShapeDtypeStruct((batch * n_seqs, n_layers, 2, N_HEADS, HEAD_DIM, HEAD_DIM), F32))


def _fold_sequences(slabs, n_seqs):
    n, batch, seq_len, d = slabs.shape
    assert batch % n_seqs == 0
    return slabs.reshape(n, batch // n_seqs, n_seqs * seq_len, d)


def _mix_out(batch, seq_len):
    spec = pl.BlockSpec((None, None, seq_len, HEAD_DIM), lambda b, h: (h, b, 0, 0))
    return spec, jax.ShapeDtypeStruct((N_HEADS, batch, seq_len, HEAD_DIM), BF16)


def _cumsum_rows(g, reverse):
    n = g.shape[0]
    pos = lax.broadcasted_iota(jnp.int32, g.shape, 0)
    sh = 1
    while sh < n:
        if reverse:
            g = g + jnp.where(pos < n - sh, pltpu.roll(g, n - sh, 0), 0.0)
        else:
            g = g + jnp.where(pos >= sh, pltpu.roll(g, sh, 0), 0.0)
        sh *= 2
    return g


def _causal_mask(n, lower):
    t_idx = lax.broadcasted_iota(jnp.int32, (n, n), 0)
    s_idx = lax.broadcasted_iota(jnp.int32, (n, n), 1)
    return t_idx >= s_idx if lower else t_idx <= s_idx


def _hgrn_direct_scores(q, k, b, k_scr, b_scr):
    n = q.shape[0]
    k_scr[...] = k
    b_scr[...] = b
    col = lax.broadcasted_iota(jnp.int32, (n, n), 1)

    def body(s, acc):
        z = q * k_scr[pl.ds(s, 1), :] * jnp.exp(jnp.minimum(b - b_scr[pl.ds(s, 1), :], 0.0))
        return jnp.where(col == s, jnp.sum(z, axis=1, keepdims=True), acc)

    return lax.fori_loop(0, n, body, jnp.zeros((n, n), F32))


def _hgrn_kernel(*refs, seq_len, n_seqs, has_s0, has_earlier, emit_state):
    qa_ref, kf_ref, kb_ref, gate_ref, gf_hi_ref, gf_lo_ref, gb_hi_ref, gb_lo_ref, va_ref, gn_ref = refs[:10]
    refs = refs[10:]
    s0_ref = earlier_ref = None
    if has_s0:
        s0_ref, refs = refs[0], refs[1:]
    if has_earlier:
        earlier_ref, refs = refs[0], refs[1:]
    mix_ref, refs = refs[0], refs[1:]
    st_out_ref = None
    if emit_state:
        st_out_ref, refs = refs[0], refs[1:]
    tri_scr, cum_scr, a16_scr, qe_scr, o_scr, u_scr, dec_scr, st_scr, safe_smem, k_scr, b_scr = refs

    c_len = HGRN_CHUNK
    blk_len = min(HGRN_BLOCK, seq_len)
    n_chunks = seq_len // c_len
    n_blocks = seq_len // blk_len
    chunks_per_blk = blk_len // c_len
    directions = ((kf_ref, gf_hi_ref, gf_lo_ref, False), (kb_ref, gb_hi_ref, gb_lo_ref, True))
    causal = (_causal_mask(c_len, True), _causal_mask(c_len, False))

    @pl.when(jnp.logical_and(pl.program_id(0) == 0, pl.program_id(1) == 0))
    def _():
        t_idx = lax.broadcasted_iota(jnp.int32, (blk_len, blk_len), 0)
        s_idx = lax.broadcasted_iota(jnp.int32, (blk_len, blk_len), 1)
        same_chunk = (t_idx // c_len) == (s_idx // c_len)
        tri_scr[0] = jnp.where(jnp.logical_and(same_chunk, s_idx <= t_idx), 1.0, 0.0).astype(BF16)
        tri_scr[1] = jnp.where(jnp.logical_and(same_chunk, s_idx >= t_idx), 1.0, 0.0).astype(BF16)

    safe_smem[n_chunks] = 0

    def cumulate(blk, slot):
        rows = _rows(blk * blk_len, blk_len)
        for d, (_, g_hi_ref, g_lo_ref, _) in enumerate(directions):
            cum = _dot(tri_scr[d], jnp.concatenate([g_hi_ref[rows, :], g_lo_ref[rows, :]], axis=1))
            cum_scr[slot, d] = cum[:, :HEAD_DIM] + cum[:, HEAD_DIM:]

    def prepare(blk, slot):
        r0 = blk * blk_len
        rows = _rows(r0, blk_len)
        q = qa_ref[rows, :]
        v16 = va_ref[rows, :]
        q_ok = jnp.max(jnp.abs(q)) <= SAFE_Q_MAX
        gates = [(k_ref[rows, :], cum_scr[slot, d]) for d, (k_ref, _, _, _) in enumerate(directions)]
        for j in range(chunks_per_blk):
            sl = slice(j * c_len, (j + 1) * c_len)
            rows_j = _rows(r0 + j * c_len, c_len)
            c = blk * chunks_per_blk + j
            q_e, k_e, dec, span, scores = [], [], [], None, None
            for d, (k, b) in enumerate(gates):
                k_j, b_j = k[sl], b[sl]
                tot = b_j[0:1] if directions[d][3] else b_j[c_len - 1 : c_len]
                dist = b_j - b_j[c_len // 2 - 1 : c_len // 2]
                a = _dot_nt((q[sl] * jnp.exp(dist)).astype(BF16), (k_j * jnp.exp(-dist)).astype(BF16))
                a = jnp.where(causal[d], a, 0.0)
                scores = a if scores is None else scores + a
                q_e.append(q[sl] * jnp.exp(b_j))
                k_e.append(k_j * jnp.exp(tot - b_j))
                dec.append(jnp.exp(tot))
                ends = jnp.maximum(jnp.abs(dist[0:1]), jnp.abs(dist[c_len - 1 : c_len]))
                span = ends if span is None else jnp.maximum(span, ends)
            a16_scr[rows_j, :] = scores.astype(BF16)
            qe_scr[rows_j, :] = jnp.concatenate(q_e, axis=1).astype(BF16)
            u_scr[c] = _state_increment(v16[sl], jnp.concatenate(k_e, axis=1).astype(BF16))
            dec_scr[c] = jnp.concatenate(dec, axis=1)
            safe = jnp.logical_and(q_ok, jnp.max(span) <= SAFE_EXP_RANGE).astype(jnp.int32)
            safe_smem[c] = safe
            safe_smem[n_chunks] = safe_smem[n_chunks] + (1 - safe)

    _skewed_stages(n_blocks, cumulate, prepare, steps_per_trip=8)

    _scan_states(n_chunks, n_seqs, lambda c: dec_scr[c], u_scr, st_scr, s0_ref, earlier_ref, st_out_ref)

    def redo_scores(c, carry):
        @pl.when(safe_smem[c] == 0)
        def _():
            rows = _rows(c * c_len, c_len)
            q = qa_ref[rows, :]
            scores = jnp.zeros((c_len, c_len), F32)
            for d, (k_ref, g_hi_ref, g_lo_ref, rev) in enumerate(directions):
                b = _cumsum_rows(g_hi_ref[rows, :].astype(F32) + g_lo_ref[rows, :].astype(F32), rev)
                scores = scores + jnp.where(causal[d], _hgrn_direct_scores(q, k_ref[rows, :], b, k_scr, b_scr), 0.0)
            a16_scr[rows, :] = scores.astype(BF16)

        return carry

    @pl.when(safe_smem[n_chunks] > 0)
    def _():
        lax.fori_loop(0, n_chunks, redo_scores, 0)

    def combine(blk, slot):
        for j in range(chunks_per_blk):
            c = blk * chunks_per_blk + j
            rows = _rows(c * c_len, c_len)
            o = _dot(a16_scr[rows, :], va_ref[rows, :]) + _dot_nt(qe_scr[rows, :], st_scr[c])
            o_scr[slot, j * c_len : (j + 1) * c_len, :] = o

    gn = gn_ref[...]

    def head_norm_gate(blk, slot):
        rows = _rows(blk * blk_len, blk_len)
        o = o_scr[slot]
        o = o * lax.rsqrt(jnp.mean(o * o, axis=-1, keepdims=True) + EPS) * gn
        mix_ref[rows, :] = (o * gate_ref[rows, :]).astype(BF16)

    _skewed_stages(n_blocks, combine, head_norm_gate, steps_per_trip=8)


def _hgrn(p32, p16, norm_gain, state, layer, emit_state, earlier, n_seqs):
    p32, p16 = _fold_sequences(p32, n_seqs), _fold_sequences(p16, n_seqs)
    _, batch, seq_len, _ = p32.shape
    n_chunks = seq_len // HGRN_CHUNK
    blk_len = min(HGRN_BLOCK, seq_len)
    has_s0 = state is not None
    f32_slabs = (P32_QA, P32_KF, P32_KB, P32_GATE_A)
    bf16_slabs = (P16_GF_HI, P16_GF_LO, P16_GB_HI, P16_GB_LO, P16_VA)
    in_specs = [_slab_spec(seq_len, s) for s in f32_slabs + bf16_slabs]
    in_specs.append(pl.BlockSpec((None, 1, HEAD_DIM), lambda b, h: (layer, 0, h)))
    args = [p32] * len(f32_slabs) + [p16] * len(bf16_slabs) + [norm_gain]
    if has_s0:
        in_specs.append(_state_in_spec(layer, n_seqs))
        args.append(state)
    mix_spec, mix_shape = _mix_out(batch, seq_len)
    out_specs, out_shape = [mix_spec], [mix_shape]
    if emit_state:
        _state_outputs(earlier, n_seqs, batch, in_specs, args, out_specs, out_shape)
    outs = pl.pallas_call(
        functools.partial(
            _hgrn_kernel, seq_len=seq_len, n_seqs=n_seqs, has_s0=has_s0,
            has_earlier=emit_state and earlier is not None, emit_state=emit_state,
        ),
        grid=(batch, N_HEADS),
        in_specs=in_specs,
        out_specs=out_specs,
        out_shape=out_shape,
        scratch_shapes=[
            pltpu.VMEM((2, blk_len, blk_len), BF16),
            pltpu.VMEM((2, 2, blk_len, HEAD_DIM), F32),
            pltpu.VMEM((seq_len, HGRN_CHUNK), BF16),
            pltpu.VMEM((seq_len, 2 * HEAD_DIM), BF16),
            pltpu.VMEM((2, blk_len, HEAD_DIM), F32),
            pltpu.VMEM((n_chunks, HEAD_DIM, 2 * HEAD_DIM), F32),
            pltpu.VMEM((n_chunks, 1, 2 * HEAD_DIM), F32),
            pltpu.VMEM((n_chunks, HEAD_DIM, 2 * HEAD_DIM), BF16),
            pltpu.SMEM((n_chunks + 1,), jnp.int32),
            pltpu.VMEM((HGRN_CHUNK, HEAD_DIM), F32),
            pltpu.VMEM((HGRN_CHUNK, HEAD_DIM), F32),
        ],
        compiler_params=pltpu.CompilerParams(vmem_limit_bytes=VMEM_LIMIT),
        name="hgrn_scan",
    )(*args)
    return (outs[0], outs[1]) if emit_state else (outs[0], None)


def _log_sigmoid(x):
    return jnp.minimum(x, 0.0) - jnp.log1p(jnp.exp(-jnp.abs(x)))


def _ret_kernel(*refs, layer, seq_len, n_seqs, has_s0, has_earlier, emit_state):
    logit_ref, q_ref, k_ref, v_ref, gate_ref, gn_ref = refs[:6]
    refs = refs[6:]
    s0_ref = earlier_ref = None
    if has_s0:
        s0_ref, refs = refs[0], refs[1:]
    if has_earlier:
        earlier_ref, refs = refs[0], refs[1:]
    mix_ref, refs = refs[0], refs[1:]
    st_out_ref = None
    if emit_state:
        st_out_ref, refs = refs[0], refs[1:]
    a16_scr, o_scr, u_scr, st_scr = refs

    c_len = RET_CHUNK
    n_chunks = seq_len // c_len
    head = pl.program_id(1)
    lg_f = _log_sigmoid(jnp.full((1, HEAD_DIM), logit_ref[layer, 0, head], F32))
    lg_b = _log_sigmoid(jnp.full((1, HEAD_DIM), logit_ref[layer, 1, head], F32))

    t = lax.broadcasted_iota(jnp.int32, (c_len, HEAD_DIM), 0).astype(F32)
    q_dec_f = jnp.exp((t + 1.0) * lg_f)
    q_dec_b = jnp.exp((c_len - t) * lg_b)
    k_dec = jnp.concatenate([jnp.exp((c_len - 1.0 - t) * lg_f), jnp.exp(t * lg_b)], axis=1)
    chunk_dec = jnp.concatenate([jnp.exp(c_len * lg_f), jnp.exp(c_len * lg_b)], axis=1)
    t_idx = lax.broadcasted_iota(jnp.int32, (c_len, c_len), 0)
    s_idx = lax.broadcasted_iota(jnp.int32, (c_len, c_len), 1)
    dist = (t_idx - s_idx).astype(F32)
    decay = jnp.where(t_idx >= s_idx, jnp.exp(jnp.maximum(dist, 0.0) * lg_f), 0.0)
    decay = decay + jnp.where(t_idx <= s_idx, jnp.exp(jnp.maximum(-dist, 0.0) * lg_b), 0.0)

    def prepare(c, carry):
        rows = _rows(c * c_len, c_len)
        k16 = k_ref[rows, :]
        a16_scr[rows, :] = (_dot_nt(q_ref[rows, :], k16) * decay).astype(BF16)
        k = k16.astype(F32)
        u_scr[c] = _state_increment(v_ref[rows, :], (jnp.concatenate([k, k], axis=1) * k_dec).astype(BF16))
        return carry

    lax.fori_loop(0, n_chunks, prepare, 0, unroll=min(16, n_chunks))

    _scan_states(n_chunks, n_seqs, lambda c: chunk_dec, u_scr, st_scr, s0_ref, earlier_ref, st_out_ref)

    blk_len = min(RET_BLOCK, seq_len)
    chunks_per_blk = blk_len // c_len

    def combine(blk, slot):
        for j in range(chunks_per_blk):
            c = blk * chunks_per_blk + j
            rows = _rows(c * c_len, c_len)
            q16 = q_ref[rows, :]
            o = _dot(a16_scr[rows, :], v_ref[rows, :])
            o = o + q_dec_f * _dot_nt(q16, st_scr[c, :, :HEAD_DIM]) + q_dec_b * _dot_nt(q16, st_scr[c, :, HEAD_DIM:])
            o_scr[slot, j * c_len : (j + 1) * c_len, :] = o

    gn = gn_ref[...]

    def head_norm_gate(blk, slot):
        rows = _rows(blk * blk_len, blk_len)
        o = o_scr[slot]
        o = o - jnp.mean(o, axis=-1, keepdims=True)
        o = o * lax.rsqrt(jnp.mean(o * o, axis=-1, keepdims=True) + EPS) * gn
        mix_ref[rows, :] = (o * gate_ref[rows, :]).astype(BF16)

    _skewed_stages(seq_len // blk_len, combine, head_norm_gate, steps_per_trip=8)


def _ret(p32, p16, decay_logit, norm_gain, state, layer, emit_state, earlier, n_seqs):
    p32, p16 = _fold_sequences(p32, n_seqs), _fold_sequences(p16, n_seqs)
    _, batch, seq_len, _ = p32.shape
    n_chunks = seq_len // RET_CHUNK
    has_s0 = state is not None
    in_specs = [pl.BlockSpec(memory_space=pltpu.SMEM)]
    in_specs += [_slab_spec(seq_len, s) for s in (P16_QR, P16_KR, P16_VR, P32_GATE_R)]
    in_specs.append(pl.BlockSpec((None, 1, HEAD_DIM), lambda b, h: (layer, 0, h)))
    args = [decay_logit, p16, p16, p16, p32, norm_gain]
    if has_s0:
        in_specs.append(_state_in_spec(layer, n_seqs))
        args.append(state)
    mix_spec, mix_shape = _mix_out(batch, seq_len)
    out_specs, out_shape = [mix_spec], [mix_shape]
    if emit_state:
        _state_outputs(earlier, n_seqs, batch, in_specs, args, out_specs, out_shape)
    outs = pl.pallas_call(
        functools.partial(
            _ret_kernel, layer=layer, seq_len=seq_len, n_seqs=n_seqs, has_s0=has_s0,
            has_earlier=emit_state and earlier is not None, emit_state=emit_state,
        ),
        grid=(batch, N_HEADS),
        in_specs=in_specs,
        out_specs=out_specs,
        out_shape=out_shape,
        scratch_shapes=[
            pltpu.VMEM((seq_len, RET_CHUNK), BF16),
            pltpu.VMEM((2, min(RET_BLOCK, seq_len), HEAD_DIM), F32),
            pltpu.VMEM((n_chunks, HEAD_DIM, 2 * HEAD_DIM), F32),
            pltpu.VMEM((n_chunks, HEAD_DIM, 2 * HEAD_DIM), BF16),
        ],
        compiler_params=pltpu.CompilerParams(vmem_limit_bytes=VMEM_LIMIT),
        name="ret_scan",
    )(*args)
    return (outs[0], outs[1]) if emit_state else (outs[0], None)


def _out_ffn_kernel(*refs, d_ff, final):
    ma_ref, mb_ref, x_ref, mod_ref, gain_ref, wo_ref, wu_ref, wd_ref = refs[:8]
    refs = refs[8:]
    if final:
        gfin_ref, refs = refs[0], refs[1:]
    out_ref, act_scr = refs

    mod = mod_ref[...]
    mix = jnp.concatenate([ma_ref[hd] for hd in range(N_HEADS)] + [mb_ref[hd] for hd in range(N_HEADS)], axis=1)
    x1 = x_ref[...] + mod[2:3] * _dot(mix, wo_ref[...])
    h = _modnorm(x1, gain_ref[...], mod[4:5], mod[3:4]).astype(BF16)
    for c in range(d_ff // FFN_CHUNK):
        lo = c * FFN_CHUNK
        gate = _dot(h, wu_ref[:, lo : lo + FFN_CHUNK])
        up = _dot(h, wu_ref[:, d_ff + lo : d_ff + lo + FFN_CHUNK])
        act_scr[:, lo : lo + FFN_CHUNK] = (_silu(gate) * up).astype(BF16)
    x2 = x1 + mod[5:6] * _dot(act_scr[...], wd_ref[...])
    if final:
        x2 = x2 * lax.rsqrt(jnp.mean(x2 * x2, axis=-1, keepdims=True) + EPS) * gfin_ref[...]
    out_ref[...] = x2


def _out_ffn(mix_a, mix_b, x, mod, gain, w_out, w_up, w_down, final_gain, layer, tile_m, cond_index):
    n_tok, d_model = x.shape
    d_ff = w_down.shape[1]
    final = final_gain is not None
    const = lambda i: (layer, 0, 0)
    in_specs = [
        pl.BlockSpec((N_HEADS, tile_m, HEAD_DIM), lambda i: (0, i, 0)),
        pl.BlockSpec((N_HEADS, tile_m, HEAD_DIM), lambda i: (0, i, 0)),
        pl.BlockSpec((tile_m, d_model), lambda i: (i, 0)),
        pl.BlockSpec((None, None, N_MOD, d_model), lambda i: (layer, cond_index(i), 0, 0)),
        pl.BlockSpec((None, 1, d_model), const),
        pl.BlockSpec((None, 2 * N_HEADS * HEAD_DIM, d_model), const, pipeline_mode=pl.Buffered(1)),
        pl.BlockSpec((None, d_model, 2 * d_ff), const, pipeline_mode=pl.Buffered(1)),
        pl.BlockSpec((None, d_ff, d_model), const, pipeline_mode=pl.Buffered(1)),
    ]
    args = [mix_a, mix_b, x, mod, gain, w_out, w_up, w_down]
    if final:
        in_specs.append(pl.BlockSpec((1, d_model), lambda i: (0, 0)))
        args.append(final_gain)
    return pl.pallas_call(
        functools.partial(_out_ffn_kernel, d_ff=d_ff, final=final),
        grid=(n_tok // tile_m,),
        in_specs=in_specs,
        out_specs=pl.BlockSpec((tile_m, d_model), lambda i: (i, 0)),
        out_shape=jax.ShapeDtypeStruct((n_tok, d_model), F32),
        scratch_shapes=[pltpu.VMEM((tile_m, d_ff), BF16)],
        compiler_params=pltpu.CompilerParams(vmem_limit_bytes=VMEM_LIMIT),
        name="out_ffn",
    )(*args)


def kernel(x_prompt, x_sample, state_hgrn, state_ret, c, c_ctx, ada_w, ada_b, norm_mix, norm_ffn, w_in,
           hgrn_lb_logits, hgrn_norm, ret_decay_logit, ret_norm, w_out, w_up, w_down, norm_final):
    batch, seq, d_model = x_prompt.shape
    dec_batch, dec_seq, _ = x_sample.shape
    depth = ada_w.shape[0]
    tile_m = 512
    assert w_in.shape[2] == D_IN and state_hgrn.shape[3] == N_HEADS and state_ret.shape[3] == N_HEADS
    assert dec_batch < COND_ROWS and dec_seq % tile_m == 0 and (batch * seq) % tile_m == 0
    assert dec_seq % GRID_W == 0
    for n_l in (seq, dec_seq):
        assert n_l % min(HGRN_BLOCK, n_l) == 0 and min(HGRN_BLOCK, n_l) % HGRN_CHUNK == 0
        assert n_l % min(RET_BLOCK, n_l) == 0 and n_l % RET_CHUNK == 0

    cond = jnp.concatenate(
        [c.astype(F32), c_ctx.astype(F32)[None, :], jnp.zeros((COND_ROWS - dec_batch - 1, d_model), F32)], axis=0
    )
    mod = _modulation(cond, ada_w, ada_b)

    w_in16, w_out16, w_up16, w_down16 = (w.astype(BF16) for w in (w_in, w_out, w_up, w_down))
    gain_mix = norm_mix.reshape(depth, 1, d_model)
    gain_ffn = norm_ffn.reshape(depth, 1, d_model)
    gain_hgrn = hgrn_norm.reshape(depth, 1, -1)
    gain_ret = ret_norm.reshape(depth, 1, -1)
    gain_final = norm_final.reshape(1, d_model)
    rope_tables = _rope_tables(dec_seq)

    tiles_per_seq = dec_seq // tile_m
    groups = [
        [x_prompt.astype(F32).reshape(batch * seq, d_model), batch, seq, lambda i: dec_batch, None, None, None, True],
        [x_sample.astype(F32).reshape(dec_batch * dec_seq, d_model), dec_batch, dec_seq,
         lambda i: i // tiles_per_seq, state_hgrn, state_ret, rope_tables, False],
    ]
    new_h = new_r = None
    for layer in range(depth):
        final_gain = gain_final if layer == depth - 1 else None
        for grp in groups:
            x, n_b, n_l, cond_index, s0_h, s0_r, tables, emit = grp
            p32, p16 = _in_proj(x, mod, gain_mix, w_in16, hgrn_lb_logits, tables, layer, tile_m, cond_index, tiles_per_seq)
            p32 = p32.reshape(N_P32, n_b, n_l, HEAD_DIM)
            p16 = p16.reshape(N_P16, n_b, n_l, HEAD_DIM)
            n_seqs = max(1, min(n_b, SCAN_ROWS_PER_STEP // n_l))
            mix_a, st_a = _hgrn(p32, p16, gain_hgrn, s0_h, layer, emit, new_h, n_seqs)
            mix_b, st_b = _ret(p32, p16, ret_decay_logit, gain_ret, s0_r, layer, emit, new_r, n_seqs)
            grp[0] = _out_ffn(
                mix_a.reshape(N_HEADS, n_b * n_l, HEAD_DIM), mix_b.reshape(N_HEADS, n_b * n_l, HEAD_DIM), x, mod,
                gain_ffn, w_out16, w_up16, w_down16, final_gain, layer, tile_m, cond_index,
            )
            if emit:
                new_h, new_r = st_a, st_b
    y_prompt = groups[0][0].reshape(batch, seq, d_model).astype(x_prompt.dtype)
    y_sample = groups[1][0].reshape(dec_batch, dec_seq, d_model).astype(x_sample.dtype)
    new_state_hgrn = new_h.astype(state_hgrn.dtype)
    new_state_ret = new_r.astype(state_ret.dtype)
    return (y_prompt, y_sample, new_state_hgrn, new_state_ret)
```

```python
import functools

import jax
import jax.numpy as jnp
import numpy as np
from jax import lax
from jax.experimental import pallas as pl
from jax.experimental.pallas import tpu as pltpu

F32 = jnp.float32
BF16 = jnp.bfloat16

N_HEADS = 4
HEAD_DIM = 128
N_SECTIONS = 9
SECTION = N_HEADS * HEAD_DIM
D_IN = N_SECTIONS * SECTION
GRID_W = 64
ROPE_BASE = 10000.0
EPS = 1e-6
GATE_FLOOR = 1e-12

P32_QA, P32_KF, P32_KB, P32_GATE_A, P32_GATE_R = range(5)
P16_GF_HI, P16_GF_LO, P16_GB_HI, P16_GB_LO, P16_VA, P16_QR, P16_KR, P16_VR = range(8)
N_P32 = 5 * N_HEADS
N_P16 = 8 * N_HEADS

N_MOD = 6
COND_ROWS = 16

HGRN_CHUNK = 64
HGRN_BLOCK = 256
RET_CHUNK = 128
RET_BLOCK = 256
SCAN_ROWS_PER_STEP = 2048
SAFE_EXP_RANGE = 80.0
SAFE_Q_MAX = 1e3
FFN_CHUNK = 256
MOD_TILE_N = 1536
VMEM_LIMIT = 56 * 1024 * 1024


def _silu(x):
    h = 0.5 * x
    return h + h * jnp.tanh(h)


def _dot(a, b):
    return jnp.dot(a, b, preferred_element_type=F32)


def _dot_nt(a, b):
    return lax.dot_general(a, b, (((1,), (1,)), ((), ())), preferred_element_type=F32)


def _split_bf16(x):
    hi = x.astype(BF16)
    return hi, (x - hi.astype(F32)).astype(BF16)


def _modnorm(x, gain, scale, shift):
    ms = jnp.mean(x * x, axis=-1, keepdims=True)
    return x * lax.rsqrt(ms + EPS) * gain * (1.0 + scale) + shift


def _mod_kernel(cond_ref, w_ref, b_ref, out_ref):
    s_hi, s_lo = _split_bf16(_silu(cond_ref[...]))
    w_hi, w_lo = _split_bf16(w_ref[...])
    s_both = jnp.concatenate([s_hi, s_lo], axis=0)
    p = _dot(s_both, w_hi) + _dot(s_both, w_lo)
    out_ref[...] = p[:COND_ROWS] + p[COND_ROWS:] + b_ref[...]


def _modulation(cond, ada_w, ada_b):
    depth, d_model, n_out = ada_w.shape
    out = pl.pallas_call(
        _mod_kernel,
        grid=(depth, n_out // MOD_TILE_N),
        in_specs=[
            pl.BlockSpec((COND_ROWS, d_model), lambda l, j: (0, 0)),
            pl.BlockSpec((None, d_model, MOD_TILE_N), lambda l, j: (l, 0, j)),
            pl.BlockSpec((None, 1, MOD_TILE_N), lambda l, j: (l, 0, j)),
        ],
        out_specs=pl.BlockSpec((None, COND_ROWS, MOD_TILE_N), lambda l, j: (l, 0, j)),
        out_shape=jax.ShapeDtypeStruct((depth, COND_ROWS, n_out), F32),
        compiler_params=pltpu.CompilerParams(vmem_limit_bytes=VMEM_LIMIT),
        name="adaln_mod",
    )(cond, ada_w, ada_b.reshape(depth, 1, n_out))
    return out.reshape(depth, COND_ROWS, N_MOD, d_model)


def _rope(x, cos, sin_signed):
    lane = lax.broadcasted_iota(jnp.int32, x.shape, 1)
    quarter = HEAD_DIM // 4
    partner = jnp.where(lane % (2 * quarter) < quarter, pltpu.roll(x, HEAD_DIM - quarter, 1), pltpu.roll(x, quarter, 1))
    return x * cos + partner * sin_signed


def _hgrn_lower_bounds(logits, layer):
    e = jnp.exp(logits - jnp.max(logits, axis=0, keepdims=True))
    p = e / jnp.sum(e, axis=0, keepdims=True)
    cum = p[0]
    for i in range(1, layer + 1):
        cum = cum + p[i]
    return jnp.clip(cum - p[0], 0.0, 1.0)


def _in_proj_kernel(*refs, layer, use_rope):
    x_ref, mod_ref, gain_ref, w_ref, lbl_ref = refs[:5]
    refs = refs[5:]
    if use_rope:
        cos_ref, sin_ref, refs = refs[0], refs[1], refs[2:]
    p32_ref, p16_ref = refs

    mod = mod_ref[...]
    h = _modnorm(x_ref[...], gain_ref[...], mod[1:2], mod[0:1]).astype(BF16)
    lb = _hgrn_lower_bounds(lbl_ref[...], layer)

    def proj(j):
        return _dot(h, w_ref[:, j * SECTION : (j + 1) * SECTION])

    def put(ref, section, val):
        for hd in range(N_HEADS):
            ref[section * N_HEADS + hd] = val[:, hd * HEAD_DIM : (hd + 1) * HEAD_DIM].astype(ref.dtype)

    put(p32_ref, P32_QA, _silu(proj(0)))
    for d, (sec_hi, sec_lo, sec_k) in enumerate(
        ((P16_GF_HI, P16_GF_LO, P32_KF), (P16_GB_HI, P16_GB_LO, P32_KB))
    ):
        lb_d = lb[d : d + 1]
        c0, c1 = 0.5 + 0.5 * lb_d, 0.5 - 0.5 * lb_d
        ct = c1 * jnp.tanh(0.5 * proj(1 + d))
        g_hi, g_lo = _split_bf16(jnp.log(jnp.maximum(c0 + ct, GATE_FLOOR)))
        put(p16_ref, sec_hi, g_hi)
        put(p16_ref, sec_lo, g_lo)
        put(p32_ref, sec_k, c1 - ct)
    put(p16_ref, P16_VA, proj(3))
    put(p32_ref, P32_GATE_A, _silu(proj(4)))
    q = proj(5)
    k = proj(6) * (HEAD_DIM ** -0.5)
    if use_rope:
        cos, sin = cos_ref[...], sin_ref[...]
        heads = [slice(hd * HEAD_DIM, (hd + 1) * HEAD_DIM) for hd in range(N_HEADS)]
        q = jnp.concatenate([_rope(q[:, sl], cos, sin) for sl in heads], axis=1)
        k = jnp.concatenate([_rope(k[:, sl], cos, sin) for sl in heads], axis=1)
    put(p16_ref, P16_QR, q)
    put(p16_ref, P16_KR, k)
    put(p16_ref, P16_VR, proj(7))
    put(p32_ref, P32_GATE_R, _silu(proj(8)))


def _in_proj(x, mod, gain, w, lb_logits, rope_tables, layer, tile_m, cond_index, tiles_per_seq):
    n_tok, d_model = x.shape
    depth = lb_logits.shape[0]
    use_rope = rope_tables is not None
    in_specs = [
        pl.BlockSpec((tile_m, d_model), lambda i: (i, 0)),
        pl.BlockSpec((None, None, N_MOD, d_model), lambda i: (layer, cond_index(i), 0, 0)),
        pl.BlockSpec((None, 1, d_model), lambda i: (layer, 0, 0)),
        pl.BlockSpec((None, d_model, D_IN), lambda i: (layer, 0, 0), pipeline_mode=pl.Buffered(1)),
        pl.BlockSpec((depth, 2, SECTION), lambda i: (0, 0, 0)),
    ]
    args = [x, mod, gain, w, lb_logits]
    if use_rope:
        in_specs += [pl.BlockSpec((tile_m, HEAD_DIM), lambda i: (i % tiles_per_seq, 0))] * 2
        args += list(rope_tables)
    return pl.pallas_call(
        functools.partial(_in_proj_kernel, layer=layer, use_rope=use_rope),
        grid=(n_tok // tile_m,),
        in_specs=in_specs,
        out_specs=[
            pl.BlockSpec((N_P32, tile_m, HEAD_DIM), lambda i: (0, i, 0)),
            pl.BlockSpec((N_P16, tile_m, HEAD_DIM), lambda i: (0, i, 0)),
        ],
        out_shape=[
            jax.ShapeDtypeStruct((N_P32, n_tok, HEAD_DIM), F32),
            jax.ShapeDtypeStruct((N_P16, n_tok, HEAD_DIM), BF16),
        ],
        compiler_params=pltpu.CompilerParams(vmem_limit_bytes=VMEM_LIMIT),
        name="in_proj",
    )(*args)


def _rope_tables(seq_len):
    quarter = HEAD_DIM // 4
    inv = np.float32(ROPE_BASE) ** (-np.arange(quarter, dtype=np.float32) / np.float32(quarter))
    pos = np.arange(seq_len, dtype=np.int32)
    ang_r = (pos // GRID_W).astype(np.float32)[:, None] * inv
    ang_c = (pos % GRID_W).astype(np.float32)[:, None] * inv
    cos = np.concatenate([np.cos(ang_r), np.cos(ang_r), np.cos(ang_c), np.cos(ang_c)], axis=1)
    sin = np.concatenate([-np.sin(ang_r), np.sin(ang_r), -np.sin(ang_c), np.sin(ang_c)], axis=1)
    return jnp.asarray(cos, F32), jnp.asarray(sin, F32)


def _rows(start, size):
    return pl.ds(start if isinstance(start, int) else pl.multiple_of(start, size), size)


def _skewed_stages(n_steps, produce, consume, steps_per_trip=2):
    steps_per_trip = min(steps_per_trip, n_steps)
    assert steps_per_trip % 2 == 0 and n_steps % steps_per_trip == 0
    produce(0, 0)
    if steps_per_trip == n_steps:
        for j in range(n_steps):
            if j + 1 < n_steps:
                produce(j + 1, (j + 1) % 2)
            consume(j, j % 2)
        return

    def body(trip, carry):
        first = steps_per_trip * trip
        for j in range(steps_per_trip):
            produce(jnp.minimum(first + j + 1, n_steps - 1), (j + 1) % 2)
            consume(first + j, j % 2)
        return carry

    lax.fori_loop(0, n_steps // steps_per_trip, body, 0)


def _state_increment(v16, k_cat):
    return _dot(v16.astype(F32).T.astype(BF16), k_cat)


def _run_states(n_chunks, first, dec_of, u_scr, st_scr, s_f, s_b):
    def step(i, carry):
        s_f, s_b = carry
        cf, cb = first + i, first + n_chunks - 1 - i
        st_scr[cf, :, :HEAD_DIM] = s_f.astype(BF16)
        st_scr[cb, :, HEAD_DIM:] = s_b.astype(BF16)
        s_f = dec_of(cf)[:, :HEAD_DIM] * s_f + u_scr[cf, :, :HEAD_DIM]
        s_b = dec_of(cb)[:, HEAD_DIM:] * s_b + u_scr[cb, :, HEAD_DIM:]
        return s_f, s_b

    if n_chunks <= 4:
        carry = (s_f, s_b)
        for i in range(n_chunks):
            carry = step(i, carry)
        return carry
    return lax.fori_loop(0, n_chunks, step, (s_f, s_b), unroll=2)


def _scan_states(n_chunks, n_seqs, dec_of, u_scr, st_scr, s0_ref, earlier_ref, st_out_ref):
    per_seq = n_chunks // n_seqs
    n_earlier = 0
    if earlier_ref is not None:
        n_earlier = earlier_ref.shape[1]
        st_out_ref[:, :n_earlier] = earlier_ref[...]
    for s in range(n_seqs):
        if s0_ref is None:
            s_f = s_b = jnp.zeros((HEAD_DIM, HEAD_DIM), F32)
        else:
            s_f, s_b = s0_ref[s, 0].T, s0_ref[s, 1].T
        s_f, s_b = _run_states(per_seq, s * per_seq, dec_of, u_scr, st_scr, s_f, s_b)
        if st_out_ref is not None:
            st_out_ref[s, n_earlier, 0] = s_f.T
            st_out_ref[s, n_earlier, 1] = s_b.T


def _slab_spec(seq_len, section):
    return pl.BlockSpec((None, None, seq_len, HEAD_DIM), lambda b, h: (section * N_HEADS + h, b, 0, 0))


def _state_in_spec(layer, n_seqs):
    return pl.BlockSpec((n_seqs, None, 2, None, HEAD_DIM, HEAD_DIM), lambda b, h: (b, layer, 0, h, 0, 0))


def _stacked_state_spec(n_seqs, n_layers):
    return pl.BlockSpec((n_seqs, n_layers, 2, None, HEAD_DIM, HEAD_DIM), lambda b, h: (b, 0, 0, h, 0, 0))


def _state_outputs(earlier, n_seqs, batch, in_specs, args, out_specs, out_shape):
    n_layers = 1
    if earlier is not None:
        n_layers += earlier.shape[1]
        in_specs.append(_stacked_state_spec(n_seqs, n_layers - 1))
        args.append(earlier)
    out_specs.append(_stacked_state_spec(n_seqs, n_layers))
    out_shape.append(jax.ShapeDtypeStruct((batch * n_seqs, n_layers, 2, N_HEADS, HEAD_DIM, HEAD_DIM), F32))


def _fold_sequences(slabs, n_seqs):
    n, batch, seq_len, d = slabs.shape
    assert batch % n_seqs == 0
    return slabs.reshape(n, batch // n_seqs, n_seqs * seq_len, d)


def _mix_out(batch, seq_len):
    spec = pl.BlockSpec((None, None, seq_len, HEAD_DIM), lambda b, h: (h, b, 0, 0))
    return spec, jax.ShapeDtypeStruct((N_HEADS, batch, seq_len, HEAD_DIM), BF16)


def _cumsum_rows(g, reverse):
    n = g.shape[0]
    pos = lax.broadcasted_iota(jnp.int32, g.shape, 0)
    sh = 1
    while sh < n:
        if reverse:
            g = g + jnp.where(pos < n - sh, pltpu.roll(g, n - sh, 0), 0.0)
        else:
            g = g + jnp.where(pos >= sh, pltpu.roll(g, sh, 0), 0.0)
        sh *= 2
    return g


def _causal_mask(n, lower):
    t_idx = lax.broadcasted_iota(jnp.int32, (n, n), 0)
    s_idx = lax.broadcasted_iota(jnp.int32, (n, n), 1)
    return t_idx >= s_idx if lower else t_idx <= s_idx


def _hgrn_direct_scores(q, k, b, k_scr, b_scr):
    n = q.shape[0]
    k_scr[...] = k
    b_scr[...] = b
    col = lax.broadcasted_iota(jnp.int32, (n, n), 1)

    def body(s, acc):
        z = q * k_scr[pl.ds(s, 1), :] * jnp.exp(jnp.minimum(b - b_scr[pl.ds(s, 1), :], 0.0))
        return jnp.where(col == s, jnp.sum(z, axis=1, keepdims=True), acc)

    return lax.fori_loop(0, n, body, jnp.zeros((n, n), F32))


def _hgrn_kernel(*refs, seq_len, n_seqs, has_s0, has_earlier, emit_state):
    qa_ref, kf_ref, kb_ref, gate_ref, gf_hi_ref, gf_lo_ref, gb_hi_ref, gb_lo_ref, va_ref, gn_ref = refs[:10]
    refs = refs[10:]
    s0_ref = earlier_ref = None
    if has_s0:
        s0_ref, refs = refs[0], refs[1:]
    if has_earlier:
        earlier_ref, refs = refs[0], refs[1:]
    mix_ref, refs = refs[0], refs[1:]
    st_out_ref = None
    if emit_state:
        st_out_ref, refs = refs[0], refs[1:]
    tri_scr, cum_scr, a16_scr, qe_scr, o_scr, u_scr, dec_scr, st_scr, safe_smem, k_scr, b_scr = refs

    c_len = HGRN_CHUNK
    blk_len = min(HGRN_BLOCK, seq_len)
    n_chunks = seq_len // c_len
    n_blocks = seq_len // blk_len
    chunks_per_blk = blk_len // c_len
    directions = ((kf_ref, gf_hi_ref, gf_lo_ref, False), (kb_ref, gb_hi_ref, gb_lo_ref, True))
    causal = (_causal_mask(c_len, True), _causal_mask(c_len, False))

    @pl.when(jnp.logical_and(pl.program_id(0) == 0, pl.program_id(1) == 0))
    def _():
        t_idx = lax.broadcasted_iota(jnp.int32, (blk_len, blk_len), 0)
        s_idx = lax.broadcasted_iota(jnp.int32, (blk_len, blk_len), 1)
        same_chunk = (t_idx // c_len) == (s_idx // c_len)
        tri_scr[0] = jnp.where(jnp.logical_and(same_chunk, s_idx <= t_idx), 1.0, 0.0).astype(BF16)
        tri_scr[1] = jnp.where(jnp.logical_and(same_chunk, s_idx >= t_idx), 1.0, 0.0).astype(BF16)

    safe_smem[n_chunks] = 0

    def cumulate(blk, slot):
        rows = _rows(blk * blk_len, blk_len)
        for d, (_, g_hi_ref, g_lo_ref, _) in enumerate(directions):
            cum = _dot(tri_scr[d], jnp.concatenate([g_hi_ref[rows, :], g_lo_ref[rows, :]], axis=1))
            cum_scr[slot, d] = cum[:, :HEAD_DIM] + cum[:, HEAD_DIM:]

    def prepare(blk, slot):
        r0 = blk * blk_len
        rows = _rows(r0, blk_len)
        q = qa_ref[rows, :]
        v16 = va_ref[rows, :]
        q_ok = jnp.max(jnp.abs(q)) <= SAFE_Q_MAX
        gates = [(k_ref[rows, :], cum_scr[slot, d]) for d, (k_ref, _, _, _) in enumerate(directions)]
        for j in range(chunks_per_blk):
            sl = slice(j * c_len, (j + 1) * c_len)
            rows_j = _rows(r0 + j * c_len, c_len)
            c = blk * chunks_per_blk + j
            q_e, k_e, dec, span, scores = [], [], [], None, None
            for d, (k, b) in enumerate(gates):
                k_j, b_j = k[sl], b[sl]
                tot = b_j[0:1] if directions[d][3] else b_j[c_len - 1 : c_len]
                dist = b_j - b_j[c_len // 2 - 1 : c_len // 2]
                a = _dot_nt((q[sl] * jnp.exp(dist)).astype(BF16), (k_j * jnp.exp(-dist)).astype(BF16))
                a = jnp.where(causal[d], a, 0.0)
                scores = a if scores is None else scores + a
                q_e.append(q[sl] * jnp.exp(b_j))
                k_e.append(k_j * jnp.exp(tot - b_j))
                dec.append(jnp.exp(tot))
                ends = jnp.maximum(jnp.abs(dist[0:1]), jnp.abs(dist[c_len - 1 : c_len]))
                span = ends if span is None else jnp.maximum(span, ends)
            a16_scr[rows_j, :] = scores.astype(BF16)
            qe_scr[rows_j, :] = jnp.concatenate(q_e, axis=1).astype(BF16)
            u_scr[c] = _state_increment(v16[sl], jnp.concatenate(k_e, axis=1).astype(BF16))
            dec_scr[c] = jnp.concatenate(dec, axis=1)
            safe = jnp.logical_and(q_ok, jnp.max(span) <= SAFE_EXP_RANGE).astype(jnp.int32)
            safe_smem[c] = safe
            safe_smem[n_chunks] = safe_smem[n_chunks] + (1 - safe)

    _skewed_stages(n_blocks, cumulate, prepare, steps_per_trip=8)

    _scan_states(n_chunks, n_seqs, lambda c: dec_scr[c], u_scr, st_scr, s0_ref, earlier_ref, st_out_ref)

    def redo_scores(c, carry):
        @pl.when(safe_smem[c] == 0)
        def _():
            rows = _rows(c * c_len, c_len)
            q = qa_ref[rows, :]
            scores = jnp.zeros((c_len, c_len), F32)
            for d, (k_ref, g_hi_ref, g_lo_ref, rev) in enumerate(directions):
                b = _cumsum_rows(g_hi_ref[rows, :].astype(F32) + g_lo_ref[rows, :].astype(F32), rev)
                scores = scores + jnp.where(causal[d], _hgrn_direct_scores(q, k_ref[rows, :], b, k_scr, b_scr), 0.0)
            a16_scr[rows, :] = scores.astype(BF16)

        return carry

    @pl.when(safe_smem[n_chunks] > 0)
    def _():
        lax.fori_loop(0, n_chunks, redo_scores, 0)

    def combine(blk, slot):
        for j in range(chunks_per_blk):
            c = blk * chunks_per_blk + j
            rows = _rows(c * c_len, c_len)
            o = _dot(a16_scr[rows, :], va_ref[rows, :]) + _dot_nt(qe_scr[rows, :], st_scr[c])
            o_scr[slot, j * c_len : (j + 1) * c_len, :] = o

    gn = gn_ref[...]

    def head_norm_gate(blk, slot):
        rows = _rows(blk * blk_len, blk_len)
        o = o_scr[slot]
        o = o * lax.rsqrt(jnp.mean(o * o, axis=-1, keepdims=True) + EPS) * gn
        mix_ref[rows, :] = (o * gate_ref[rows, :]).astype(BF16)

    _skewed_stages(n_blocks, combine, head_norm_gate, steps_per_trip=8)


def _hgrn(p32, p16, norm_gain, state, layer, emit_state, earlier, n_seqs):
    p32, p16 = _fold_sequences(p32, n_seqs), _fold_sequences(p16, n_seqs)
    _, batch, seq_len, _ = p32.shape
    n_chunks = seq_len // HGRN_CHUNK
    blk_len = min(HGRN_BLOCK, seq_len)
    has_s0 = state is not None
    f32_slabs = (P32_QA, P32_KF, P32_KB, P32_GATE_A)
    bf16_slabs = (P16_GF_HI, P16_GF_LO, P16_GB_HI, P16_GB_LO, P16_VA)
    in_specs = [_slab_spec(seq_len, s) for s in f32_slabs + bf16_slabs]
    in_specs.append(pl.BlockSpec((None, 1, HEAD_DIM), lambda b, h: (layer, 0, h)))
    args = [p32] * len(f32_slabs) + [p16] * len(bf16_slabs) + [norm_gain]
    if has_s0:
        in_specs.append(_state_in_spec(layer, n_seqs))
        args.append(state)
    mix_spec, mix_shape = _mix_out(batch, seq_len)
    out_specs, out_shape = [mix_spec], [mix_shape]
    if emit_state:
        _state_outputs(earlier, n_seqs, batch, in_specs, args, out_specs, out_shape)
    outs = pl.pallas_call(
        functools.partial(
            _hgrn_kernel, seq_len=seq_len, n_seqs=n_seqs, has_s0=has_s0,
            has_earlier=emit_state and earlier is not None, emit_state=emit_state,
        ),
        grid=(batch, N_HEADS),
        in_specs=in_specs,
        out_specs=out_specs,
        out_shape=out_shape,
        scratch_shapes=[
            pltpu.VMEM((2, blk_len, blk_len), BF16),
            pltpu.VMEM((2, 2, blk_len, HEAD_DIM), F32),
            pltpu.VMEM((seq_len, HGRN_CHUNK), BF16),
            pltpu.VMEM((seq_len, 2 * HEAD_DIM), BF16),
            pltpu.VMEM((2, blk_len, HEAD_DIM), F32),
            pltpu.VMEM((n_chunks, HEAD_DIM, 2 * HEAD_DIM), F32),
            pltpu.VMEM((n_chunks, 1, 2 * HEAD_DIM), F32),
            pltpu.VMEM((n_chunks, HEAD_DIM, 2 * HEAD_DIM), BF16),
            pltpu.SMEM((n_chunks + 1,), jnp.int32),
            pltpu.VMEM((HGRN_CHUNK, HEAD_DIM), F32),
            pltpu.VMEM((HGRN_CHUNK, HEAD_DIM), F32),
        ],
        compiler_params=pltpu.CompilerParams(vmem_limit_bytes=VMEM_LIMIT),
        name="hgrn_scan",
    )(*args)
    return (outs[0], outs[1]) if emit_state else (outs[0], None)


def _log_sigmoid(x):
    return jnp.minimum(x, 0.0) - jnp.log1p(jnp.exp(-jnp.abs(x)))


def _ret_kernel(*refs, layer, seq_len, n_seqs, has_s0, has_earlier, emit_state):
    logit_ref, q_ref, k_ref, v_ref, gate_ref, gn_ref = refs[:6]
    refs = refs[6:]
    s0_ref = earlier_ref = None
    if has_s0:
        s0_ref, refs = refs[0], refs[1:]
    if has_earlier:
        earlier_ref, refs = refs[0], refs[1:]
    mix_ref, refs = refs[0], refs[1:]
    st_out_ref = None
    if emit_state:
        st_out_ref, refs = refs[0], refs[1:]
    a16_scr, o_scr, u_scr, st_scr = refs

    c_len = RET_CHUNK
    n_chunks = seq_len // c_len
    head = pl.program_id(1)
    lg_f = _log_sigmoid(jnp.full((1, HEAD_DIM), logit_ref[layer, 0, head], F32))
    lg_b = _log_sigmoid(jnp.full((1, HEAD_DIM), logit_ref[layer, 1, head], F32))

    t = lax.broadcasted_iota(jnp.int32, (c_len, HEAD_DIM), 0).astype(F32)
    q_dec_f = jnp.exp((t + 1.0) * lg_f)
    q_dec_b = jnp.exp((c_len - t) * lg_b)
    k_dec = jnp.concatenate([jnp.exp((c_len - 1.0 - t) * lg_f), jnp.exp(t * lg_b)], axis=1)
    chunk_dec = jnp.concatenate([jnp.exp(c_len * lg_f), jnp.exp(c_len * lg_b)], axis=1)
    t_idx = lax.broadcasted_iota(jnp.int32, (c_len, c_len), 0)
    s_idx = lax.broadcasted_iota(jnp.int32, (c_len, c_len), 1)
    dist = (t_idx - s_idx).astype(F32)
    decay = jnp.where(t_idx >= s_idx, jnp.exp(jnp.maximum(dist, 0.0) * lg_f), 0.0)
    decay = decay + jnp.where(t_idx <= s_idx, jnp.exp(jnp.maximum(-dist, 0.0) * lg_b), 0.0)

    def prepare(c, carry):
        rows = _rows(c * c_len, c_len)
        k16 = k_ref[rows, :]
        a16_scr[rows, :] = (_dot_nt(q_ref[rows, :], k16) * decay).astype(BF16)
        k = k16.astype(F32)
        u_scr[c] = _state_increment(v_ref[rows, :], (jnp.concatenate([k, k], axis=1) * k_dec).astype(BF16))
        return carry

    lax.fori_loop(0, n_chunks, prepare, 0, unroll=min(16, n_chunks))

    _scan_states(n_chunks, n_seqs, lambda c: chunk_dec, u_scr, st_scr, s0_ref, earlier_ref, st_out_ref)

    blk_len = min(RET_BLOCK, seq_len)
    chunks_per_blk = blk_len // c_len

    def combine(blk, slot):
        for j in range(chunks_per_blk):
            c = blk * chunks_per_blk + j
            rows = _rows(c * c_len, c_len)
            q16 = q_ref[rows, :]
            o = _dot(a16_scr[rows, :], v_ref[rows, :])
            o = o + q_dec_f * _dot_nt(q16, st_scr[c, :, :HEAD_DIM]) + q_dec_b * _dot_nt(q16, st_scr[c, :, HEAD_DIM:])
            o_scr[slot, j * c_len : (j + 1) * c_len, :] = o

    gn = gn_ref[...]

    def head_norm_gate(blk, slot):
        rows = _rows(blk * blk_len, blk_len)
        o = o_scr[slot]
        o = o - jnp.mean(o, axis=-1, keepdims=True)
        o = o * lax.rsqrt(jnp.mean(o * o, axis=-1, keepdims=True) + EPS) * gn
        mix_ref[rows, :] = (o * gate_ref[rows, :]).astype(BF16)

    _skewed_stages(seq_len // blk_len, combine, head_norm_gate, steps_per_trip=8)


def _ret(p32, p16, decay_logit, norm_gain, state, layer, emit_state, earlier, n_seqs):
    p32, p16 = _fold_sequences(p32, n_seqs), _fold_sequences(p16, n_seqs)
    _, batch, seq_len, _ = p32.shape
    n_chunks = seq_len // RET_CHUNK
    has_s0 = state is not None
    in_specs = [pl.BlockSpec(memory_space=pltpu.SMEM)]
    in_specs += [_slab_spec(seq_len, s) for s in (P16_QR, P16_KR, P16_VR, P32_GATE_R)]
    in_specs.append(pl.BlockSpec((None, 1, HEAD_DIM), lambda b, h: (layer, 0, h)))
    args = [decay_logit, p16, p16, p16, p32, norm_gain]
    if has_s0:
        in_specs.append(_state_in_spec(layer, n_seqs))
        args.append(state)
    mix_spec, mix_shape = _mix_out(batch, seq_len)
    out_specs, out_shape = [mix_spec], [mix_shape]
    if emit_state:
        _state_outputs(earlier, n_seqs, batch, in_specs, args, out_specs, out_shape)
    outs = pl.pallas_call(
        functools.partial(
            _ret_kernel, layer=layer, seq_len=seq_len, n_seqs=n_seqs, has_s0=has_s0,
            has_earlier=emit_state and earlier is not None, emit_state=emit_state,
        ),
        grid=(batch, N_HEADS),
        in_specs=in_specs,
        out_specs=out_specs,
        out_shape=out_shape,
        scratch_shapes=[
            pltpu.VMEM((seq_len, RET_CHUNK), BF16),
            pltpu.VMEM((2, min(RET_BLOCK, seq_len), HEAD_DIM), F32),
            pltpu.VMEM((n_chunks, HEAD_DIM, 2 * HEAD_DIM), F32),
            pltpu.VMEM((n_chunks, HEAD_DIM, 2 * HEAD_DIM), BF16),
        ],
        compiler_params=pltpu.CompilerParams(vmem_limit_bytes=VMEM_LIMIT),
        name="ret_scan",
    )(*args)
    return (outs[0], outs[1]) if emit_state else (outs[0], None)


def _out_ffn_kernel(*refs, d_ff, final):
    ma_ref, mb_ref, x_ref, mod_ref, gain_ref, wo_ref, wu_ref, wd_ref = refs[:8]
    refs = refs[8:]
    if final:
        gfin_ref, refs = refs[0], refs[1:]
    out_ref, act_scr = refs

    mod = mod_ref[...]
    mix = jnp.concatenate([ma_ref[hd] for hd in range(N_HEADS)] + [mb_ref[hd] for hd in range(N_HEADS)], axis=1)
    x1 = x_ref[...] + mod[2:3] * _dot(mix, wo_ref[...])
    h = _modnorm(x1, gain_ref[...], mod[4:5], mod[3:4]).astype(BF16)
    for c in range(d_ff // FFN_CHUNK):
        lo = c * FFN_CHUNK
        gate = _dot(h, wu_ref[:, lo : lo + FFN_CHUNK])
        up = _dot(h, wu_ref[:, d_ff + lo : d_ff + lo + FFN_CHUNK])
        act_scr[:, lo : lo + FFN_CHUNK] = (_silu(gate) * up).astype(BF16)
    x2 = x1 + mod[5:6] * _dot(act_scr[...], wd_ref[...])
    if final:
        x2 = x2 * lax.rsqrt(jnp.mean(x2 * x2, axis=-1, keepdims=True) + EPS) * gfin_ref[...]
    out_ref[...] = x2


def _out_ffn(mix_a, mix_b, x, mod, gain, w_out, w_up, w_down, final_gain, layer, tile_m, cond_index):
    n_tok, d_model = x.shape
    d_ff = w_down.shape[1]
    final = final_gain is not None
    const = lambda i: (layer, 0, 0)
    in_specs = [
        pl.BlockSpec((N_HEADS, tile_m, HEAD_DIM), lambda i: (0, i, 0)),
        pl.BlockSpec((N_HEADS, tile_m, HEAD_DIM), lambda i: (0, i, 0)),
        pl.BlockSpec((tile_m, d_model), lambda i: (i, 0)),
        pl.BlockSpec((None, None, N_MOD, d_model), lambda i: (layer, cond_index(i), 0, 0)),
        pl.BlockSpec((None, 1, d_model), const),
        pl.BlockSpec((None, 2 * N_HEADS * HEAD_DIM, d_model), const, pipeline_mode=pl.Buffered(1)),
        pl.BlockSpec((None, d_model, 2 * d_ff), const, pipeline_mode=pl.Buffered(1)),
        pl.BlockSpec((None, d_ff, d_model), const, pipeline_mode=pl.Buffered(1)),
    ]
    args = [mix_a, mix_b, x, mod, gain, w_out, w_up, w_down]
    if final:
        in_specs.append(pl.BlockSpec((1, d_model), lambda i: (0, 0)))
        args.append(final_gain)
    return pl.pallas_call(
        functools.partial(_out_ffn_kernel, d_ff=d_ff, final=final),
        grid=(n_tok // tile_m,),
        in_specs=in_specs,
        out_specs=pl.BlockSpec((tile_m, d_model), lambda i: (i, 0)),
        out_shape=jax.ShapeDtypeStruct((n_tok, d_model), F32),
        scratch_shapes=[pltpu.VMEM((tile_m, d_ff), BF16)],
        compiler_params=pltpu.CompilerParams(vmem_limit_bytes=VMEM_LIMIT),
        name="out_ffn",
    )(*args)


def kernel(x_prompt, x_sample, state_hgrn, state_ret, c, c_ctx, ada_w, ada_b, norm_mix, norm_ffn, w_in,
           hgrn_lb_logits, hgrn_norm, ret_decay_logit, ret_norm, w_out, w_up, w_down, norm_final):
    batch, seq, d_model = x_prompt.shape
    dec_batch, dec_seq, _ = x_sample.shape
    depth = ada_w.shape[0]
    tile_m = 512
    assert w_in.shape[2] == D_IN and state_hgrn.shape[3] == N_HEADS and state_ret.shape[3] == N_HEADS
    assert dec_batch < COND_ROWS and dec_seq % tile_m == 0 and (batch * seq) % tile_m == 0
    assert dec_seq % GRID_W == 0
    for n_l in (seq, dec_seq):
        assert n_l % min(HGRN_BLOCK, n_l) == 0 and min(HGRN_BLOCK, n_l) % HGRN_CHUNK == 0
        assert n_l % min(RET_BLOCK, n_l) == 0 and n_l % RET_CHUNK == 0

    cond = jnp.concatenate(
        [c.astype(F32), c_ctx.astype(F32)[None, :], jnp.zeros((COND_ROWS - dec_batch - 1, d_model), F32)], axis=0
    )
    mod = _modulation(cond, ada_w, ada_b)

    w_in16, w_out16, w_up16, w_down16 = (w.astype(BF16) for w in (w_in, w_out, w_up, w_down))
    gain_mix = norm_mix.reshape(depth, 1, d_model)
    gain_ffn = norm_ffn.reshape(depth, 1, d_model)
    gain_hgrn = hgrn_norm.reshape(depth, 1, -1)
    gain_ret = ret_norm.reshape(depth, 1, -1)
    gain_final = norm_final.reshape(1, d_model)
    rope_tables = _rope_tables(dec_seq)

    tiles_per_seq = dec_seq // tile_m
    groups = [
        [x_prompt.astype(F32).reshape(batch * seq, d_model), batch, seq, lambda i: dec_batch, None, None, None, True],
        [x_sample.astype(F32).reshape(dec_batch * dec_seq, d_model), dec_batch, dec_seq,
         lambda i: i // tiles_per_seq, state_hgrn, state_ret, rope_tables, False],
    ]
    new_h = new_r = None
    for layer in range(depth):
        final_gain = gain_final if layer == depth - 1 else None
        for grp in groups:
            x, n_b, n_l, cond_index, s0_h, s0_r, tables, emit = grp
            p32, p16 = _in_proj(x, mod, gain_mix, w_in16, hgrn_lb_logits, tables, layer, tile_m, cond_index, tiles_per_seq)
            p32 = p32.reshape(N_P32, n_b, n_l, HEAD_DIM)
            p16 = p16.reshape(N_P16, n_b, n_l, HEAD_DIM)
            n_seqs = max(1, min(n_b, SCAN_ROWS_PER_STEP // n_l))
            mix_a, st_a = _hgrn(p32, p16, gain_hgrn, s0_h, layer, emit, new_h, n_seqs)
            mix_b, st_b = _ret(p32, p16, ret_decay_logit, gain_ret, s0_r, layer, emit, new_r, n_seqs)
            grp[0] = _out_ffn(
                mix_a.reshape(N_HEADS, n_b * n_l, HEAD_DIM), mix_b.reshape(N_HEADS, n_b * n_l, HEAD_DIM), x, mod,
                gain_ffn, w_out16, w_up16, w_down16, final_gain, layer, tile_m, cond_index,
            )
            if emit:
                new_h, new_r = st_a, st_b
    y_prompt = groups[0][0].reshape(batch, seq, d_model).astype(x_prompt.dtype)
    y_sample = groups[1][0].reshape(dec_batch, dec_seq, d_model).astype(x_sample.dtype)
    new_state_hgrn = new_h.astype(state_hgrn.dtype)
    new_state_ret = new_r.astype(state_ret.dtype)
    return (y_prompt, y_sample, new_state_hgrn, new_state_ret)
```

```python
import functools

import jax
import jax.numpy as jnp
import numpy as np
from jax import lax
from jax.experimental import pallas as pl
from jax.experimental.pallas import tpu as pltpu

F32 = jnp.float32
BF16 = jnp.bfloat16

N_HEADS = 4
HEAD_DIM = 128
N_SECTIONS = 9
SECTION = N_HEADS * HEAD_DIM
D_IN = N_SECTIONS * SECTION
GRID_W = 64
ROPE_BASE = 10000.0
EPS = 1e-6
GATE_FLOOR = 1e-12

P32_QA, P32_KF, P32_KB, P32_GATE_A, P32_GATE_R = range(5)
P16_GF_HI, P16_GF_LO, P16_GB_HI, P16_GB_LO, P16_VA, P16_QR, P16_KR, P16_VR = range(8)
N_P32 = 5 * N_HEADS
N_P16 = 8 * N_HEADS

N_MOD = 6
COND_ROWS = 16

HGRN_CHUNK = 64
HGRN_BLOCK = 256
RET_CHUNK = 128
RET_BLOCK = 256
SCAN_ROWS_PER_STEP = 2048
BLOCKS_PER_TRIP = 8
RET_CHUNKS_PER_TRIP = 16
STATE_STEPS_INLINE = 4
STATE_STEPS_PER_TRIP = 2
SAFE_EXP_RANGE = 80.0
SAFE_Q_MAX = 1e3
FFN_CHUNK = 256
MOD_TILE_N = 1536
DENSE_TILE_M = 512
V7X_VMEM_BYTES = 64 * 1024 * 1024
VMEM_LIMIT = V7X_VMEM_BYTES - 8 * 1024 * 1024


def _silu(x):
    h = 0.5 * x
    return h + h * jnp.tanh(h)


def _dot(a, b):
    return jnp.dot(a, b, preferred_element_type=F32)


def _dot_nt(a, b):
    return lax.dot_general(a, b, (((1,), (1,)), ((), ())), preferred_element_type=F32)


def _split_bf16(x):
    hi = x.astype(BF16)
    return hi, (x - hi.astype(F32)).astype(BF16)


def _modnorm(x, gain, scale, shift):
    ms = jnp.mean(x * x, axis=-1, keepdims=True)
    return x * lax.rsqrt(ms + EPS) * gain * (1.0 + scale) + shift


def _mod_kernel(cond_ref, w_ref, b_ref, out_ref):
    s_hi, s_lo = _split_bf16(_silu(cond_ref[...]))
    w_hi, w_lo = _split_bf16(w_ref[...])
    s_both = jnp.concatenate([s_hi, s_lo], axis=0)
    p = _dot(s_both, w_hi) + _dot(s_both, w_lo)
    out_ref[...] = p[:COND_ROWS] + p[COND_ROWS:] + b_ref[...]


def _modulation(cond, ada_w, ada_b):
    depth, d_model, n_out = ada_w.shape
    out = pl.pallas_call(
        _mod_kernel,
        grid=(depth, n_out // MOD_TILE_N),
        in_specs=[
            pl.BlockSpec((COND_ROWS, d_model), lambda l, j: (0, 0)),
            pl.BlockSpec((None, d_model, MOD_TILE_N), lambda l, j: (l, 0, j)),
            pl.BlockSpec((None, 1, MOD_TILE_N), lambda l, j: (l, 0, j)),
        ],
        out_specs=pl.BlockSpec((None, COND_ROWS, MOD_TILE_N), lambda l, j: (l, 0, j)),
        out_shape=jax.ShapeDtypeStruct((depth, COND_ROWS, n_out), F32),
        compiler_params=pltpu.CompilerParams(vmem_limit_bytes=VMEM_LIMIT),
        name="adaln_mod",
    )(cond, ada_w, ada_b.reshape(depth, 1, n_out))
    return out.reshape(depth, COND_ROWS, N_MOD, d_model)


def _rope(x, cos, sin_signed):
    lane = lax.broadcasted_iota(jnp.int32, x.shape, 1)
    quarter = HEAD_DIM // 4
    partner = jnp.where(lane % (2 * quarter) < quarter, pltpu.roll(x, HEAD_DIM - quarter, 1), pltpu.roll(x, quarter, 1))
    return x * cos + partner * sin_signed


def _hgrn_lower_bounds(logits, layer):
    e = jnp.exp(logits - jnp.max(logits, axis=0, keepdims=True))
    p = e / jnp.sum(e, axis=0, keepdims=True)
    cum = p[0]
    for i in range(1, layer + 1):
        cum = cum + p[i]
    return jnp.clip(cum - p[0], 0.0, 1.0)


def _in_proj_kernel(*refs, layer, use_rope):
    x_ref, mod_ref, gain_ref, w_ref, lbl_ref = refs[:5]
    refs = refs[5:]
    if use_rope:
        cos_ref, sin_ref, refs = refs[0], refs[1], refs[2:]
    p32_ref, p16_ref = refs

    mod = mod_ref[...]
    h = _modnorm(x_ref[...], gain_ref[...], mod[1:2], mod[0:1]).astype(BF16)
    lb = _hgrn_lower_bounds(lbl_ref[...], layer)

    def proj(j):
        return _dot(h, w_ref[:, j * SECTION : (j + 1) * SECTION])

    def put(ref, section, val):
        for hd in range(N_HEADS):
            ref[section * N_HEADS + hd] = val[:, hd * HEAD_DIM : (hd + 1) * HEAD_DIM].astype(ref.dtype)

    put(p32_ref, P32_QA, _silu(proj(0)))
    for d, (sec_hi, sec_lo, sec_k) in enumerate(
        ((P16_GF_HI, P16_GF_LO, P32_KF), (P16_GB_HI, P16_GB_LO, P32_KB))
    ):
        lb_d = lb[d : d + 1]
        c0, c1 = 0.5 + 0.5 * lb_d, 0.5 - 0.5 * lb_d
        ct = c1 * jnp.tanh(0.5 * proj(1 + d))
        g_hi, g_lo = _split_bf16(jnp.log(jnp.maximum(c0 + ct, GATE_FLOOR)))
        put(p16_ref, sec_hi, g_hi)
        put(p16_ref, sec_lo, g_lo)
        put(p32_ref, sec_k, c1 - ct)
    put(p16_ref, P16_VA, proj(3))
    put(p32_ref, P32_GATE_A, _silu(proj(4)))
    q = proj(5)
    k = proj(6) * (HEAD_DIM ** -0.5)
    if use_rope:
        cos, sin = cos_ref[...], sin_ref[...]
        heads = [slice(hd * HEAD_DIM, (hd + 1) * HEAD_DIM) for hd in range(N_HEADS)]
        q = jnp.concatenate([_rope(q[:, sl], cos, sin) for sl in heads], axis=1)
        k = jnp.concatenate([_rope(k[:, sl], cos, sin) for sl in heads], axis=1)
    put(p16_ref, P16_QR, q)
    put(p16_ref, P16_KR, k)
    put(p16_ref, P16_VR, proj(7))
    put(p32_ref, P32_GATE_R, _silu(proj(8)))


def _in_proj(x, mod, gain, w, lb_logits, rope_tables, layer, tile_m, cond_index, tiles_per_seq):
    n_tok, d_model = x.shape
    depth = lb_logits.shape[0]
    use_rope = rope_tables is not None
    in_specs = [
        pl.BlockSpec((tile_m, d_model), lambda i: (i, 0)),
        pl.BlockSpec((None, None, N_MOD, d_model), lambda i: (layer, cond_index(i), 0, 0)),
        pl.BlockSpec((None, 1, d_model), lambda i: (layer, 0, 0)),
        pl.BlockSpec((None, d_model, D_IN), lambda i: (layer, 0, 0), pipeline_mode=pl.Buffered(1)),
        pl.BlockSpec((depth, 2, SECTION), lambda i: (0, 0, 0)),
    ]
    args = [x, mod, gain, w, lb_logits]
    if use_rope:
        in_specs += [pl.BlockSpec((tile_m, HEAD_DIM), lambda i: (i % tiles_per_seq, 0))] * 2
        args += list(rope_tables)
    return pl.pallas_call(
        functools.partial(_in_proj_kernel, layer=layer, use_rope=use_rope),
        grid=(n_tok // tile_m,),
        in_specs=in_specs,
        out_specs=[
            pl.BlockSpec((N_P32, tile_m, HEAD_DIM), lambda i: (0, i, 0)),
            pl.BlockSpec((N_P16, tile_m, HEAD_DIM), lambda i: (0, i, 0)),
        ],
        out_shape=[
            jax.ShapeDtypeStruct((N_P32, n_tok, HEAD_DIM), F32),
            jax.ShapeDtypeStruct((N_P16, n_tok, HEAD_DIM), BF16),
        ],
        compiler_params=pltpu.CompilerParams(vmem_limit_bytes=VMEM_LIMIT),
        name="in_proj",
    )(*args)


def _rope_tables(seq_len):
    quarter = HEAD_DIM // 4
    inv = np.float32(ROPE_BASE) ** (-np.arange(quarter, dtype=np.float32) / np.float32(quarter))
    pos = np.arange(seq_len, dtype=np.int32)
    ang_r = (pos // GRID_W).astype(np.float32)[:, None] * inv
    ang_c = (pos % GRID_W).astype(np.float32)[:, None] * inv
    cos = np.concatenate([np.cos(ang_r), np.cos(ang_r), np.cos(ang_c), np.cos(ang_c)], axis=1)
    sin = np.concatenate([-np.sin(ang_r), np.sin(ang_r), -np.sin(ang_c), np.sin(ang_c)], axis=1)
    return jnp.asarray(cos, F32), jnp.asarray(sin, F32)


def _rows(start, size):
    return pl.ds(start if isinstance(start, int) else pl.multiple_of(start, size), size)


def _skewed_stages(n_steps, produce, consume, steps_per_trip):
    steps_per_trip = min(steps_per_trip, n_steps)
    assert steps_per_trip % 2 == 0 and n_steps % steps_per_trip == 0
    produce(0, 0)
    if steps_per_trip == n_steps:
        for j in range(n_steps):
            if j + 1 < n_steps:
                produce(j + 1, (j + 1) % 2)
            consume(j, j % 2)
        return

    def body(trip, carry):
        first = steps_per_trip * trip
        for j in range(steps_per_trip):
            produce(jnp.minimum(first + j + 1, n_steps - 1), (j + 1) % 2)
            consume(first + j, j % 2)
        return carry

    lax.fori_loop(0, n_steps // steps_per_trip, body, 0)


def _state_increment(v16, k_cat):
    return _dot(v16.astype(F32).T.astype(BF16), k_cat)


def _run_states(n_chunks, first, dec_of, u_scr, st_scr, s_f, s_b):
    def step(i, carry):
        s_f, s_b = carry
        cf, cb = first + i, first + n_chunks - 1 - i
        st_scr[cf, :, :HEAD_DIM] = s_f.astype(BF16)
        st_scr[cb, :, HEAD_DIM:] = s_b.astype(BF16)
        s_f = dec_of(cf)[:, :HEAD_DIM] * s_f + u_scr[cf, :, :HEAD_DIM]
        s_b = dec_of(cb)[:, HEAD_DIM:] * s_b + u_scr[cb, :, HEAD_DIM:]
        return s_f, s_b

    if n_chunks <= STATE_STEPS_INLINE:
        carry = (s_f, s_b)
        for i in range(n_chunks):
            carry = step(i, carry)
        return carry
    return lax.fori_loop(0, n_chunks, step, (s_f, s_b), unroll=STATE_STEPS_PER_TRIP)


def _scan_states(n_chunks, n_seqs, dec_of, u_scr, st_scr, s0_ref, earlier_ref, st_out_ref):
    per_seq = n_chunks // n_seqs
    n_earlier = 0
    if earlier_ref is not None:
        n_earlier = earlier_ref.shape[1]
        st_out_ref[:, :n_earlier] = earlier_ref[...]
    for s in range(n_seqs):
        if s0_ref is None:
            s_f = s_b = jnp.zeros((HEAD_DIM, HEAD_DIM), F32)
        else:
            s_f, s_b = s0_ref[s, 0].T, s0_ref[s, 1].T
        s_f, s_b = _run_states(per_seq, s * per_seq, dec_of, u_scr, st_scr, s_f, s_b)
        if st_out_ref is not None:
            st_out_ref[s, n_earlier, 0] = s_f.T
            st_out_ref[s, n_earlier, 1] = s_b.T


def _slab_spec(seq_len, section):
    return pl.BlockSpec((None, None, seq_len, HEAD_DIM), lambda b, h: (section * N_HEADS + h, b, 0, 0))


def _state_in_spec(layer, n_seqs):
    return pl.BlockSpec((n_seqs, None, 2, None, HEAD_DIM, HEAD_DIM), lambda b, h: (b, layer, 0, h, 0, 0))


def _stacked_state_spec(n_seqs, n_layers):
    return pl.BlockSpec((n_seqs, n_layers, 2, None, HEAD_DIM, HEAD_DIM), lambda b, h: (b, 0, 0, h, 0, 0))


def _state_outputs(earlier, n_seqs, batch, in_specs, args, out_specs, out_shape):
    n_layers = 1
    if earlier is not None:
        n_layers += earlier.shape[1]
        in_specs.append(_stacked_state_spec(n_seqs, n_layers - 1))
        args.append(earlier)
    out_specs.append(_stacked_state_spec(n_seqs, n_layers))
    out_shape.append(jax.ShapeDtypeStruct((batch * n_seqs, n_layers, 2, N_HEADS, HEAD_DIM, HEAD_DIM), F32))


def _fold_sequences(slabs, n_seqs):
    n, batch, seq_len, d = slabs.shape
    assert batch % n_seqs == 0
    return slabs.reshape(n, batch // n_seqs, n_seqs * seq_len, d)


def _mix_out(batch, seq_len):
    spec = pl.BlockSpec((None, None, seq_len, HEAD_DIM), lambda b, h: (h, b, 0, 0))
    return spec, jax.ShapeDtypeStruct((N_HEADS, batch, seq_len, HEAD_DIM), BF16)


def _cumsum_rows(g, reverse):
    n = g.shape[0]
    pos = lax.broadcasted_iota(jnp.int32, g.shape, 0)
    sh = 1
    while sh < n:
        if reverse:
            g = g + jnp.where(pos < n - sh, pltpu.roll(g, n - sh, 0), 0.0)
        else:
            g = g + jnp.where(pos >= sh, pltpu.roll(g, sh, 0), 0.0)
        sh *= 2
    return g


def _causal_mask(n, lower):
    t_idx = lax.broadcasted_iota(jnp.int32, (n, n), 0)
    s_idx = lax.broadcasted_iota(jnp.int32, (n, n), 1)
    return t_idx >= s_idx if lower else t_idx <= s_idx


def _hgrn_direct_scores(q, k, b, k_scr, b_scr):
    n = q.shape[0]
    k_scr[...] = k
    b_scr[...] = b
    col = lax.broadcasted_iota(jnp.int32, (n, n), 1)

    def body(s, acc):
        z = q * k_scr[pl.ds(s, 1), :] * jnp.exp(jnp.minimum(b - b_scr[pl.ds(s, 1), :], 0.0))
        return jnp.where(col == s, jnp.sum(z, axis=1, keepdims=True), acc)

    return lax.fori_loop(0, n, body, jnp.zeros((n, n), F32))


def _hgrn_kernel(*refs, seq_len, n_seqs, has_s0, has_earlier, emit_state):
    qa_ref, kf_ref, kb_ref, gate_ref, gf_hi_ref, gf_lo_ref, gb_hi_ref, gb_lo_ref, va_ref, gn_ref = refs[:10]
    refs = refs[10:]
    s0_ref = earlier_ref = None
    if has_s0:
        s0_ref, refs = refs[0], refs[1:]
    if has_earlier:
        earlier_ref, refs = refs[0], refs[1:]
    mix_ref, refs = refs[0], refs[1:]
    st_out_ref = None
    if emit_state:
        st_out_ref, refs = refs[0], refs[1:]
    tri_scr, cum_scr, a16_scr, qe_scr, o_scr, u_scr, dec_scr, st_scr, safe_smem, k_scr, b_scr = refs

    c_len = HGRN_CHUNK
    blk_len = min(HGRN_BLOCK, seq_len)
    n_chunks = seq_len // c_len
    n_blocks = seq_len // blk_len
    chunks_per_blk = blk_len // c_len
    directions = ((kf_ref, gf_hi_ref, gf_lo_ref, False), (kb_ref, gb_hi_ref, gb_lo_ref, True))
    causal = (_causal_mask(c_len, True), _causal_mask(c_len, False))

    @pl.when(jnp.logical_and(pl.program_id(0) == 0, pl.program_id(1) == 0))
    def _():
        t_idx = lax.broadcasted_iota(jnp.int32, (blk_len, blk_len), 0)
        s_idx = lax.broadcasted_iota(jnp.int32, (blk_len, blk_len), 1)
        same_chunk = (t_idx // c_len) == (s_idx // c_len)
        tri_scr[0] = jnp.where(jnp.logical_and(same_chunk, s_idx <= t_idx), 1.0, 0.0).astype(BF16)
        tri_scr[1] = jnp.where(jnp.logical_and(same_chunk, s_idx >= t_idx), 1.0, 0.0).astype(BF16)

    safe_smem[n_chunks] = 0

    def cumulate(blk, slot):
        rows = _rows(blk * blk_len, blk_len)
        for d, (_, g_hi_ref, g_lo_ref, _) in enumerate(directions):
            cum = _dot(tri_scr[d], jnp.concatenate([g_hi_ref[rows, :], g_lo_ref[rows, :]], axis=1))
            cum_scr[slot, d] = cum[:, :HEAD_DIM] + cum[:, HEAD_DIM:]

    def prepare(blk, slot):
        r0 = blk * blk_len
        rows = _rows(r0, blk_len)
        q = qa_ref[rows, :]
        v16 = va_ref[rows, :]
        q_ok = jnp.max(jnp.abs(q)) <= SAFE_Q_MAX
        gates = [(k_ref[rows, :], cum_scr[slot, d]) for d, (k_ref, _, _, _) in enumerate(directions)]
        for j in range(chunks_per_blk):
            sl = slice(j * c_len, (j + 1) * c_len)
            rows_j = _rows(r0 + j * c_len, c_len)
            c = blk * chunks_per_blk + j
            q_e, k_e, dec, span, scores = [], [], [], None, None
            for d, (k, b) in enumerate(gates):
                k_j, b_j = k[sl], b[sl]
                tot = b_j[0:1] if directions[d][3] else b_j[c_len - 1 : c_len]
                dist = b_j - b_j[c_len // 2 - 1 : c_len // 2]
                a = _dot_nt((q[sl] * jnp.exp(dist)).astype(BF16), (k_j * jnp.exp(-dist)).astype(BF16))
                a = jnp.where(causal[d], a, 0.0)
                scores = a if scores is None else scores + a
                q_e.append(q[sl] * jnp.exp(b_j))
                k_e.append(k_j * jnp.exp(tot - b_j))
                dec.append(jnp.exp(tot))
                ends = jnp.maximum(jnp.abs(dist[0:1]), jnp.abs(dist[c_len - 1 : c_len]))
                span = ends if span is None else jnp.maximum(span, ends)
            a16_scr[rows_j, :] = scores.astype(BF16)
            qe_scr[rows_j, :] = jnp.concatenate(q_e, axis=1).astype(BF16)
            u_scr[c] = _state_increment(v16[sl], jnp.concatenate(k_e, axis=1).astype(BF16))
            dec_scr[c] = jnp.concatenate(dec, axis=1)
            safe = jnp.logical_and(q_ok, jnp.max(span) <= SAFE_EXP_RANGE).astype(jnp.int32)
            safe_smem[c] = safe
            safe_smem[n_chunks] = safe_smem[n_chunks] + (1 - safe)

    _skewed_stages(n_blocks, cumulate, prepare, steps_per_trip=BLOCKS_PER_TRIP)

    _scan_states(n_chunks, n_seqs, lambda c: dec_scr[c], u_scr, st_scr, s0_ref, earlier_ref, st_out_ref)

    def redo_scores(c, carry):
        @pl.when(safe_smem[c] == 0)
        def _():
            rows = _rows(c * c_len, c_len)
            q = qa_ref[rows, :]
            scores = jnp.zeros((c_len, c_len), F32)
            for d, (k_ref, g_hi_ref, g_lo_ref, rev) in enumerate(directions):
                b = _cumsum_rows(g_hi_ref[rows, :].astype(F32) + g_lo_ref[rows, :].astype(F32), rev)
                scores = scores + jnp.where(causal[d], _hgrn_direct_scores(q, k_ref[rows, :], b, k_scr, b_scr), 0.0)
            a16_scr[rows, :] = scores.astype(BF16)

        return carry

    @pl.when(safe_smem[n_chunks] > 0)
    def _():
        lax.fori_loop(0, n_chunks, redo_scores, 0)

    def combine(blk, slot):
        for j in range(chunks_per_blk):
            c = blk * chunks_per_blk + j
            rows = _rows(c * c_len, c_len)
            o = _dot(a16_scr[rows, :], va_ref[rows, :]) + _dot_nt(qe_scr[rows, :], st_scr[c])
            o_scr[slot, j * c_len : (j + 1) * c_len, :] = o

    gn = gn_ref[...]

    def head_norm_gate(blk, slot):
        rows = _rows(blk * blk_len, blk_len)
        o = o_scr[slot]
        o = o * lax.rsqrt(jnp.mean(o * o, axis=-1, keepdims=True) + EPS) * gn
        mix_ref[rows, :] = (o * gate_ref[rows, :]).astype(BF16)

    _skewed_stages(n_blocks, combine, head_norm_gate, steps_per_trip=BLOCKS_PER_TRIP)


def _hgrn(p32, p16, norm_gain, state, layer, emit_state, earlier, n_seqs):
    p32, p16 = _fold_sequences(p32, n_seqs), _fold_sequences(p16, n_seqs)
    _, batch, seq_len, _ = p32.shape
    n_chunks = seq_len // HGRN_CHUNK
    blk_len = min(HGRN_BLOCK, seq_len)
    has_s0 = state is not None
    f32_slabs = (P32_QA, P32_KF, P32_KB, P32_GATE_A)
    bf16_slabs = (P16_GF_HI, P16_GF_LO, P16_GB_HI, P16_GB_LO, P16_VA)
    in_specs = [_slab_spec(seq_len, s) for s in f32_slabs + bf16_slabs]
    in_specs.append(pl.BlockSpec((None, 1, HEAD_DIM), lambda b, h: (layer, 0, h)))
    args = [p32] * len(f32_slabs) + [p16] * len(bf16_slabs) + [norm_gain]
    if has_s0:
        in_specs.append(_state_in_spec(layer, n_seqs))
        args.append(state)
    mix_spec, mix_shape = _mix_out(batch, seq_len)
    out_specs, out_shape = [mix_spec], [mix_shape]
    if emit_state:
        _state_outputs(earlier, n_seqs, batch, in_specs, args, out_specs, out_shape)
    outs = pl.pallas_call(
        functools.partial(
            _hgrn_kernel, seq_len=seq_len, n_seqs=n_seqs, has_s0=has_s0,
            has_earlier=emit_state and earlier is not None, emit_state=emit_state,
        ),
        grid=(batch, N_HEADS),
        in_specs=in_specs,
        out_specs=out_specs,
        out_shape=out_shape,
        scratch_shapes=[
            pltpu.VMEM((2, blk_len, blk_len), BF16),
            pltpu.VMEM((2, 2, blk_len, HEAD_DIM), F32),
            pltpu.VMEM((seq_len, HGRN_CHUNK), BF16),
            pltpu.VMEM((seq_len, 2 * HEAD_DIM), BF16),
            pltpu.VMEM((2, blk_len, HEAD_DIM), F32),
            pltpu.VMEM((n_chunks, HEAD_DIM, 2 * HEAD_DIM), F32),
            pltpu.VMEM((n_chunks, 1, 2 * HEAD_DIM), F32),
            pltpu.VMEM((n_chunks, HEAD_DIM, 2 * HEAD_DIM), BF16),
            pltpu.SMEM((n_chunks + 1,), jnp.int32),
            pltpu.VMEM((HGRN_CHUNK, HEAD_DIM), F32),
            pltpu.VMEM((HGRN_CHUNK, HEAD_DIM), F32),
        ],
        compiler_params=pltpu.CompilerParams(vmem_limit_bytes=VMEM_LIMIT),
        name="hgrn_scan",
    )(*args)
    return (outs[0], outs[1]) if emit_state else (outs[0], None)


def _log_sigmoid(x):
    return jnp.minimum(x, 0.0) - jnp.log1p(jnp.exp(-jnp.abs(x)))


def _ret_kernel(*refs, layer, seq_len, n_seqs, has_s0, has_earlier, emit_state):
    logit_ref, q_ref, k_ref, v_ref, gate_ref, gn_ref = refs[:6]
    refs = refs[6:]
    s0_ref = earlier_ref = None
    if has_s0:
        s0_ref, refs = refs[0], refs[1:]
    if has_earlier:
        earlier_ref, refs = refs[0], refs[1:]
    mix_ref, refs = refs[0], refs[1:]
    st_out_ref = None
    if emit_state:
        st_out_ref, refs = refs[0], refs[1:]
    a16_scr, o_scr, u_scr, st_scr = refs

    c_len = RET_CHUNK
    n_chunks = seq_len // c_len
    head = pl.program_id(1)
    lg_f = _log_sigmoid(jnp.full((1, HEAD_DIM), logit_ref[layer, 0, head], F32))
    lg_b = _log_sigmoid(jnp.full((1, HEAD_DIM), logit_ref[layer, 1, head], F32))

    t = lax.broadcasted_iota(jnp.int32, (c_len, HEAD_DIM), 0).astype(F32)
    q_dec_f = jnp.exp((t + 1.0) * lg_f)
    q_dec_b = jnp.exp((c_len - t) * lg_b)
    k_dec = jnp.concatenate([jnp.exp((c_len - 1.0 - t) * lg_f), jnp.exp(t * lg_b)], axis=1)
    chunk_dec = jnp.concatenate([jnp.exp(c_len * lg_f), jnp.exp(c_len * lg_b)], axis=1)
    t_idx = lax.broadcasted_iota(jnp.int32, (c_len, c_len), 0)
    s_idx = lax.broadcasted_iota(jnp.int32, (c_len, c_len), 1)
    dist = (t_idx - s_idx).astype(F32)
    decay = jnp.where(t_idx >= s_idx, jnp.exp(jnp.maximum(dist, 0.0) * lg_f), 0.0)
    decay = decay + jnp.where(t_idx <= s_idx, jnp.exp(jnp.maximum(-dist, 0.0) * lg_b), 0.0)

    def prepare(c, carry):
        rows = _rows(c * c_len, c_len)
        k16 = k_ref[rows, :]
        a16_scr[rows, :] = (_dot_nt(q_ref[rows, :], k16) * decay).astype(BF16)
        k = k16.astype(F32)
        u_scr[c] = _state_increment(v_ref[rows, :], (jnp.concatenate([k, k], axis=1) * k_dec).astype(BF16))
        return carry

    lax.fori_loop(0, n_chunks, prepare, 0, unroll=min(RET_CHUNKS_PER_TRIP, n_chunks))

    _scan_states(n_chunks, n_seqs, lambda c: chunk_dec, u_scr, st_scr, s0_ref, earlier_ref, st_out_ref)

    blk_len = min(RET_BLOCK, seq_len)
    chunks_per_blk = blk_len // c_len

    def combine(blk, slot):
        for j in range(chunks_per_blk):
            c = blk * chunks_per_blk + j
            rows = _rows(c * c_len, c_len)
            q16 = q_ref[rows, :]
            o = _dot(a16_scr[rows, :], v_ref[rows, :])
            o = o + q_dec_f * _dot_nt(q16, st_scr[c, :, :HEAD_DIM]) + q_dec_b * _dot_nt(q16, st_scr[c, :, HEAD_DIM:])
            o_scr[slot, j * c_len : (j + 1) * c_len, :] = o

    gn = gn_ref[...]

    def head_norm_gate(blk, slot):
        rows = _rows(blk * blk_len, blk_len)
        o = o_scr[slot]
        o = o - jnp.mean(o, axis=-1, keepdims=True)
        o = o * lax.rsqrt(jnp.mean(o * o, axis=-1, keepdims=True) + EPS) * gn
        mix_ref[rows, :] = (o * gate_ref[rows, :]).astype(BF16)

    _skewed_stages(seq_len // blk_len, combine, head_norm_gate, steps_per_trip=BLOCKS_PER_TRIP)


def _ret(p32, p16, decay_logit, norm_gain, state, layer, emit_state, earlier, n_seqs):
    p32, p16 = _fold_sequences(p32, n_seqs), _fold_sequences(p16, n_seqs)
    _, batch, seq_len, _ = p32.shape
    n_chunks = seq_len // RET_CHUNK
    has_s0 = state is not None
    in_specs = [pl.BlockSpec(memory_space=pltpu.SMEM)]
    in_specs += [_slab_spec(seq_len, s) for s in (P16_QR, P16_KR, P16_VR, P32_GATE_R)]
    in_specs.append(pl.BlockSpec((None, 1, HEAD_DIM), lambda b, h: (layer, 0, h)))
    args = [decay_logit, p16, p16, p16, p32, norm_gain]
    if has_s0:
        in_specs.append(_state_in_spec(layer, n_seqs))
        args.append(state)
    mix_spec, mix_shape = _mix_out(batch, seq_len)
    out_specs, out_shape = [mix_spec], [mix_shape]
    if emit_state:
        _state_outputs(earlier, n_seqs, batch, in_specs, args, out_specs, out_shape)
    outs = pl.pallas_call(
        functools.partial(
            _ret_kernel, layer=layer, seq_len=seq_len, n_seqs=n_seqs, has_s0=has_s0,
            has_earlier=emit_state and earlier is not None, emit_state=emit_state,
        ),
        grid=(batch, N_HEADS),
        in_specs=in_specs,
        out_specs=out_specs,
        out_shape=out_shape,
        scratch_shapes=[
            pltpu.VMEM((seq_len, RET_CHUNK), BF16),
            pltpu.VMEM((2, min(RET_BLOCK, seq_len), HEAD_DIM), F32),
            pltpu.VMEM((n_chunks, HEAD_DIM, 2 * HEAD_DIM), F32),
            pltpu.VMEM((n_chunks, HEAD_DIM, 2 * HEAD_DIM), BF16),
        ],
        compiler_params=pltpu.CompilerParams(vmem_limit_bytes=VMEM_LIMIT),
        name="ret_scan",
    )(*args)
    return (outs[0], outs[1]) if emit_state else (outs[0], None)


def _out_ffn_kernel(*refs, d_ff, final):
    ma_ref, mb_ref, x_ref, mod_ref, gain_ref, wo_ref, wu_ref, wd_ref = refs[:8]
    refs = refs[8:]
    if final:
        gfin_ref, refs = refs[0], refs[1:]
    out_ref, act_scr = refs

    mod = mod_ref[...]
    mix = jnp.concatenate([ma_ref[hd] for hd in range(N_HEADS)] + [mb_ref[hd] for hd in range(N_HEADS)], axis=1)
    x1 = x_ref[...] + mod[2:3] * _dot(mix, wo_ref[...])
    h = _modnorm(x1, gain_ref[...], mod[4:5], mod[3:4]).astype(BF16)
    for c in range(d_ff // FFN_CHUNK):
        lo = c * FFN_CHUNK
        gate = _dot(h, wu_ref[:, lo : lo + FFN_CHUNK])
        up = _dot(h, wu_ref[:, d_ff + lo : d_ff + lo + FFN_CHUNK])
        act_scr[:, lo : lo + FFN_CHUNK] = (_silu(gate) * up).astype(BF16)
    x2 = x1 + mod[5:6] * _dot(act_scr[...], wd_ref[...])
    if final:
        x2 = x2 * lax.rsqrt(jnp.mean(x2 * x2, axis=-1, keepdims=True) + EPS) * gfin_ref[...]
    out_ref[...] = x2


def _out_ffn(mix_a, mix_b, x, mod, gain, w_out, w_up, w_down, final_gain, layer, tile_m, cond_index):
    n_tok, d_model = x.shape
    d_ff = w_down.shape[1]
    final = final_gain is not None
    const = lambda i: (layer, 0, 0)
    in_specs = [
        pl.BlockSpec((N_HEADS, tile_m, HEAD_DIM), lambda i: (0, i, 0)),
        pl.BlockSpec((N_HEADS, tile_m, HEAD_DIM), lambda i: (0, i, 0)),
        pl.BlockSpec((tile_m, d_model), lambda i: (i, 0)),
        pl.BlockSpec((None, None, N_MOD, d_model), lambda i: (layer, cond_index(i), 0, 0)),
        pl.BlockSpec((None, 1, d_model), const),
        pl.BlockSpec((None, 2 * N_HEADS * HEAD_DIM, d_model), const, pipeline_mode=pl.Buffered(1)),
        pl.BlockSpec((None, d_model, 2 * d_ff), const, pipeline_mode=pl.Buffered(1)),
        pl.BlockSpec((None, d_ff, d_model), const, pipeline_mode=pl.Buffered(1)),
    ]
    args = [mix_a, mix_b, x, mod, gain, w_out, w_up, w_down]
    if final:
        in_specs.append(pl.BlockSpec((1, d_model), lambda i: (0, 0)))
        args.append(final_gain)
    return pl.pallas_call(
        functools.partial(_out_ffn_kernel, d_ff=d_ff, final=final),
        grid=(n_tok // tile_m,),
        in_specs=in_specs,
        out_specs=pl.BlockSpec((tile_m, d_model), lambda i: (i, 0)),
        out_shape=jax.ShapeDtypeStruct((n_tok, d_model), F32),
        scratch_shapes=[pltpu.VMEM((tile_m, d_ff), BF16)],
        compiler_params=pltpu.CompilerParams(vmem_limit_bytes=VMEM_LIMIT),
        name="out_ffn",
    )(*args)


def kernel(x_prompt, x_sample, state_hgrn, state_ret, c, c_ctx, ada_w, ada_b, norm_mix, norm_ffn, w_in,
           hgrn_lb_logits, hgrn_norm, ret_decay_logit, ret_norm, w_out, w_up, w_down, norm_final):
    batch, seq, d_model = x_prompt.shape
    dec_batch, dec_seq, _ = x_sample.shape
    depth = ada_w.shape[0]
    tile_m = DENSE_TILE_M
    assert w_in.shape[2] == D_IN and state_hgrn.shape[3] == N_HEADS and state_ret.shape[3] == N_HEADS
    assert dec_batch < COND_ROWS and dec_seq % tile_m == 0 and (batch * seq) % tile_m == 0
    assert dec_seq % GRID_W == 0
    for n_l in (seq, dec_seq):
        assert n_l % min(HGRN_BLOCK, n_l) == 0 and min(HGRN_BLOCK, n_l) % HGRN_CHUNK == 0
        assert n_l % min(RET_BLOCK, n_l) == 0 and n_l % RET_CHUNK == 0

    cond = jnp.concatenate(
        [c.astype(F32), c_ctx.astype(F32)[None, :], jnp.zeros((COND_ROWS - dec_batch - 1, d_model), F32)], axis=0
    )
    mod = _modulation(cond, ada_w, ada_b)

    w_in16, w_out16, w_up16, w_down16 = (w.astype(BF16) for w in (w_in, w_out, w_up, w_down))
    gain_mix = norm_mix.reshape(depth, 1, d_model)
    gain_ffn = norm_ffn.reshape(depth, 1, d_model)
    gain_hgrn = hgrn_norm.reshape(depth, 1, -1)
    gain_ret = ret_norm.reshape(depth, 1, -1)
    gain_final = norm_final.reshape(1, d_model)
    rope_tables = _rope_tables(dec_seq)

    tiles_per_seq = dec_seq // tile_m
    groups = [
        [x_prompt.astype(F32).reshape(batch * seq, d_model), batch, seq, lambda i: dec_batch, None, None, None, True],
        [x_sample.astype(F32).reshape(dec_batch * dec_seq, d_model), dec_batch, dec_seq,
         lambda i: i // tiles_per_seq, state_hgrn, state_ret, rope_tables, False],
    ]
    new_h = new_r = None
    for layer in range(depth):
        final_gain = gain_final if layer == depth - 1 else None
        for grp in groups:
            x, n_b, n_l, cond_index, s0_h, s0_r, tables, emit = grp
            p32, p16 = _in_proj(x, mod, gain_mix, w_in16, hgrn_lb_logits, tables, layer, tile_m, cond_index, tiles_per_seq)
            p32 = p32.reshape(N_P32, n_b, n_l, HEAD_DIM)
            p16 = p16.reshape(N_P16, n_b, n_l, HEAD_DIM)
            n_seqs = max(1, min(n_b, SCAN_ROWS_PER_STEP // n_l))
            mix_a, st_a = _hgrn(p32, p16, gain_hgrn, s0_h, layer, emit, new_h, n_seqs)
            mix_b, st_b = _ret(p32, p16, ret_decay_logit, gain_ret, s0_r, layer, emit, new_r, n_seqs)
            grp[0] = _out_ffn(
                mix_a.reshape(N_HEADS, n_b * n_l, HEAD_DIM), mix_b.reshape(N_HEADS, n_b * n_l, HEAD_DIM), x, mod,
                gain_ffn, w_out16, w_up16, w_down16, final_gain, layer, tile_m, cond_index,
            )
            if emit:
                new_h, new_r = st_a, st_b
    y_prompt = groups[0][0].reshape(batch, seq, d_model).astype(x_prompt.dtype)
    y_sample = groups[1][0].reshape(dec_batch, dec_seq, d_model).astype(x_sample.dtype)
    new_state_hgrn = new_h.astype(state_hgrn.dtype)
    new_state_ret = new_r.astype(state_ret.dtype)
    return (y_prompt, y_sample, new_state_hgrn, new_state_ret)
```

```python
import functools

import jax
import jax.numpy as jnp
import numpy as np
from jax import lax
from jax.experimental import pallas as pl
from jax.experimental.pallas import tpu as pltpu

F32 = jnp.float32
BF16 = jnp.bfloat16

N_HEADS = 4
HEAD_DIM = 128
N_SECTIONS = 9
SECTION = N_HEADS * HEAD_DIM
D_IN = N_SECTIONS * SECTION
GRID_W = 64
ROPE_BASE = 10000.0
EPS = 1e-6
GATE_FLOOR = 1e-12

P32_QA, P32_KF, P32_KB, P32_GATE_A, P32_GATE_R = range(5)
P16_GF_HI, P16_GF_LO, P16_GB_HI, P16_GB_LO, P16_VA, P16_QR, P16_KR, P16_VR = range(8)
N_P32 = 5 * N_HEADS
N_P16 = 8 * N_HEADS

N_MOD = 6
COND_ROWS = 16

HGRN_CHUNK = 64
HGRN_BLOCK = 256
RET_CHUNK = 128
RET_BLOCK = 256
SCAN_ROWS_PER_STEP = 2048
BLOCKS_PER_TRIP = 8
RET_CHUNKS_PER_TRIP = 16
STATE_STEPS_INLINE = 4
STATE_STEPS_PER_TRIP = 2
SAFE_EXP_RANGE = 80.0
SAFE_Q_MAX = 1e3
FFN_CHUNK = 256
MOD_TILE_N = 1536
DENSE_TILE_M = 512
FFN_TILE_M = 1024
V7X_VMEM_BYTES = 64 * 1024 * 1024
VMEM_LIMIT = V7X_VMEM_BYTES - 8 * 1024 * 1024


def _silu(x):
    h = 0.5 * x
    return h + h * jnp.tanh(h)


def _dot(a, b):
    return jnp.dot(a, b, preferred_element_type=F32)


def _dot_nt(a, b):
    return lax.dot_general(a, b, (((1,), (1,)), ((), ())), preferred_element_type=F32)


def _split_bf16(x):
    hi = x.astype(BF16)
    return hi, (x - hi.astype(F32)).astype(BF16)


def _modnorm(x, gain, scale, shift):
    ms = jnp.mean(x * x, axis=-1, keepdims=True)
    return x * lax.rsqrt(ms + EPS) * gain * (1.0 + scale) + shift


def _mod_kernel(cond_ref, w_ref, b_ref, out_ref):
    s_hi, s_lo = _split_bf16(_silu(cond_ref[...]))
    w_hi, w_lo = _split_bf16(w_ref[...])
    s_both = jnp.concatenate([s_hi, s_lo], axis=0)
    p = _dot(s_both, w_hi) + _dot(s_both, w_lo)
    out_ref[...] = p[:COND_ROWS] + p[COND_ROWS:] + b_ref[...]


def _modulation(cond, ada_w, ada_b):
    depth, d_model, n_out = ada_w.shape
    out = pl.pallas_call(
        _mod_kernel,
        grid=(depth, n_out // MOD_TILE_N),
        in_specs=[
            pl.BlockSpec((COND_ROWS, d_model), lambda l, j: (0, 0)),
            pl.BlockSpec((None, d_model, MOD_TILE_N), lambda l, j: (l, 0, j)),
            pl.BlockSpec((None, 1, MOD_TILE_N), lambda l, j: (l, 0, j)),
        ],
        out_specs=pl.BlockSpec((None, COND_ROWS, MOD_TILE_N), lambda l, j: (l, 0, j)),
        out_shape=jax.ShapeDtypeStruct((depth, COND_ROWS, n_out), F32),
        compiler_params=pltpu.CompilerParams(vmem_limit_bytes=VMEM_LIMIT),
        name="adaln_mod",
    )(cond, ada_w, ada_b.reshape(depth, 1, n_out))
    return out.reshape(depth, COND_ROWS, N_MOD, d_model)


def _rope(x, cos, sin_signed):
    lane = lax.broadcasted_iota(jnp.int32, x.shape, 1)
    quarter = HEAD_DIM // 4
    partner = jnp.where(lane % (2 * quarter) < quarter, pltpu.roll(x, HEAD_DIM - quarter, 1), pltpu.roll(x, quarter, 1))
    return x * cos + partner * sin_signed


def _hgrn_lower_bounds(logits, layer):
    e = jnp.exp(logits - jnp.max(logits, axis=0, keepdims=True))
    p = e / jnp.sum(e, axis=0, keepdims=True)
    cum = p[0]
    for i in range(1, layer + 1):
        cum = cum + p[i]
    return jnp.clip(cum - p[0], 0.0, 1.0)


def _in_proj_kernel(*refs, layer, use_rope):
    x_ref, mod_ref, gain_ref, w_ref, lbl_ref = refs[:5]
    refs = refs[5:]
    if use_rope:
        cos_ref, sin_ref, refs = refs[0], refs[1], refs[2:]
    p32_ref, p16_ref = refs

    mod = mod_ref[...]
    h = _modnorm(x_ref[...], gain_ref[...], mod[1:2], mod[0:1]).astype(BF16)
    lb = _hgrn_lower_bounds(lbl_ref[...], layer)

    def proj(j):
        return _dot(h, w_ref[:, j * SECTION : (j + 1) * SECTION])

    def put(ref, section, val):
        for hd in range(N_HEADS):
            ref[section * N_HEADS + hd] = val[:, hd * HEAD_DIM : (hd + 1) * HEAD_DIM].astype(ref.dtype)

    put(p32_ref, P32_QA, _silu(proj(0)))
    for d, (sec_hi, sec_lo, sec_k) in enumerate(
        ((P16_GF_HI, P16_GF_LO, P32_KF), (P16_GB_HI, P16_GB_LO, P32_KB))
    ):
        lb_d = lb[d : d + 1]
        c0, c1 = 0.5 + 0.5 * lb_d, 0.5 - 0.5 * lb_d
        ct = c1 * jnp.tanh(0.5 * proj(1 + d))
        g_hi, g_lo = _split_bf16(jnp.log(jnp.maximum(c0 + ct, GATE_FLOOR)))
        put(p16_ref, sec_hi, g_hi)
        put(p16_ref, sec_lo, g_lo)
        put(p32_ref, sec_k, c1 - ct)
    put(p16_ref, P16_VA, proj(3))
    put(p32_ref, P32_GATE_A, _silu(proj(4)))
    q = proj(5)
    k = proj(6) * (HEAD_DIM ** -0.5)
    if use_rope:
        cos, sin = cos_ref[...], sin_ref[...]
        heads = [slice(hd * HEAD_DIM, (hd + 1) * HEAD_DIM) for hd in range(N_HEADS)]
        q = jnp.concatenate([_rope(q[:, sl], cos, sin) for sl in heads], axis=1)
        k = jnp.concatenate([_rope(k[:, sl], cos, sin) for sl in heads], axis=1)
    put(p16_ref, P16_QR, q)
    put(p16_ref, P16_KR, k)
    put(p16_ref, P16_VR, proj(7))
    put(p32_ref, P32_GATE_R, _silu(proj(8)))


def _in_proj(x, mod, gain, w, lb_logits, rope_tables, layer, tile_m, cond_index, tiles_per_seq):
    n_tok, d_model = x.shape
    depth = lb_logits.shape[0]
    use_rope = rope_tables is not None
    in_specs = [
        pl.BlockSpec((tile_m, d_model), lambda i: (i, 0)),
        pl.BlockSpec((None, None, N_MOD, d_model), lambda i: (layer, cond_index(i), 0, 0)),
        pl.BlockSpec((None, 1, d_model), lambda i: (layer, 0, 0)),
        pl.BlockSpec((None, d_model, D_IN), lambda i: (layer, 0, 0), pipeline_mode=pl.Buffered(1)),
        pl.BlockSpec((depth, 2, SECTION), lambda i: (0, 0, 0)),
    ]
    args = [x, mod, gain, w, lb_logits]
    if use_rope:
        in_specs += [pl.BlockSpec((tile_m, HEAD_DIM), lambda i: (i % tiles_per_seq, 0))] * 2
        args += list(rope_tables)
    return pl.pallas_call(
        functools.partial(_in_proj_kernel, layer=layer, use_rope=use_rope),
        grid=(n_tok // tile_m,),
        in_specs=in_specs,
        out_specs=[
            pl.BlockSpec((N_P32, tile_m, HEAD_DIM), lambda i: (0, i, 0)),
            pl.BlockSpec((N_P16, tile_m, HEAD_DIM), lambda i: (0, i, 0)),
        ],
        out_shape=[
            jax.ShapeDtypeStruct((N_P32, n_tok, HEAD_DIM), F32),
            jax.ShapeDtypeStruct((N_P16, n_tok, HEAD_DIM), BF16),
        ],
        compiler_params=pltpu.CompilerParams(vmem_limit_bytes=VMEM_LIMIT),
        name="in_proj",
    )(*args)


def _rope_tables(seq_len):
    quarter = HEAD_DIM // 4
    inv = np.float32(ROPE_BASE) ** (-np.arange(quarter, dtype=np.float32) / np.float32(quarter))
    pos = np.arange(seq_len, dtype=np.int32)
    ang_r = (pos // GRID_W).astype(np.float32)[:, None] * inv
    ang_c = (pos % GRID_W).astype(np.float32)[:, None] * inv
    cos = np.concatenate([np.cos(ang_r), np.cos(ang_r), np.cos(ang_c), np.cos(ang_c)], axis=1)
    sin = np.concatenate([-np.sin(ang_r), np.sin(ang_r), -np.sin(ang_c), np.sin(ang_c)], axis=1)
    return jnp.asarray(cos, F32), jnp.asarray(sin, F32)


def _rows(start, size):
    return pl.ds(start if isinstance(start, int) else pl.multiple_of(start, size), size)


def _skewed_stages(n_steps, produce, consume, steps_per_trip):
    steps_per_trip = min(steps_per_trip, n_steps)
    assert steps_per_trip % 2 == 0 and n_steps % steps_per_trip == 0
    produce(0, 0)
    if steps_per_trip == n_steps:
        for j in range(n_steps):
            if j + 1 < n_steps:
                produce(j + 1, (j + 1) % 2)
            consume(j, j % 2)
        return

    def body(trip, carry):
        first = steps_per_trip * trip
        for j in range(steps_per_trip):
            produce(jnp.minimum(first + j + 1, n_steps - 1), (j + 1) % 2)
            consume(first + j, j % 2)
        return carry

    lax.fori_loop(0, n_steps // steps_per_trip, body, 0)


def _state_increment(v16, k_cat):
    return _dot(v16.astype(F32).T.astype(BF16), k_cat)


def _run_states(n_chunks, first, dec_of, u_scr, st_scr, s_f, s_b):
    def step(i, carry):
        s_f, s_b = carry
        cf, cb = first + i, first + n_chunks - 1 - i
        st_scr[cf, :, :HEAD_DIM] = s_f.astype(BF16)
        st_scr[cb, :, HEAD_DIM:] = s_b.astype(BF16)
        s_f = dec_of(cf)[:, :HEAD_DIM] * s_f + u_scr[cf, :, :HEAD_DIM]
        s_b = dec_of(cb)[:, HEAD_DIM:] * s_b + u_scr[cb, :, HEAD_DIM:]
        return s_f, s_b

    if n_chunks <= STATE_STEPS_INLINE:
        carry = (s_f, s_b)
        for i in range(n_chunks):
            carry = step(i, carry)
        return carry
    return lax.fori_loop(0, n_chunks, step, (s_f, s_b), unroll=STATE_STEPS_PER_TRIP)


def _scan_states(n_chunks, n_seqs, dec_of, u_scr, st_scr, s0_ref, earlier_ref, st_out_ref):
    per_seq = n_chunks // n_seqs
    n_earlier = 0
    if earlier_ref is not None:
        n_earlier = earlier_ref.shape[1]
        st_out_ref[:, :n_earlier] = earlier_ref[...]
    for s in range(n_seqs):
        if s0_ref is None:
            s_f = s_b = jnp.zeros((HEAD_DIM, HEAD_DIM), F32)
        else:
            s_f, s_b = s0_ref[s, 0].T, s0_ref[s, 1].T
        s_f, s_b = _run_states(per_seq, s * per_seq, dec_of, u_scr, st_scr, s_f, s_b)
        if st_out_ref is not None:
            st_out_ref[s, n_earlier, 0] = s_f.T
            st_out_ref[s, n_earlier, 1] = s_b.T


def _slab_spec(seq_len, section):
    return pl.BlockSpec((None, None, seq_len, HEAD_DIM), lambda b, h: (section * N_HEADS + h, b, 0, 0))


def _state_in_spec(layer, n_seqs):
    return pl.BlockSpec((n_seqs, None, 2, None, HEAD_DIM, HEAD_DIM), lambda b, h: (b, layer, 0, h, 0, 0))


def _stacked_state_spec(n_seqs, n_layers):
    return pl.BlockSpec((n_seqs, n_layers, 2, None, HEAD_DIM, HEAD_DIM), lambda b, h: (b, 0, 0, h, 0, 0))


def _state_outputs(earlier, n_seqs, batch, in_specs, args, out_specs, out_shape):
    n_layers = 1
    if earlier is not None:
        n_layers += earlier.shape[1]
        in_specs.append(_stacked_state_spec(n_seqs, n_layers - 1))
        args.append(earlier)
    out_specs.append(_stacked_state_spec(n_seqs, n_layers))
    out_shape.append(jax.ShapeDtypeStruct((batch * n_seqs, n_layers, 2, N_HEADS, HEAD_DIM, HEAD_DIM), F32))


def _fold_sequences(slabs, n_seqs):
    n, batch, seq_len, d = slabs.shape
    assert batch % n_seqs == 0
    return slabs.reshape(n, batch // n_seqs, n_seqs * seq_len, d)


def _mix_out(batch, seq_len):
    spec = pl.BlockSpec((None, None, seq_len, HEAD_DIM), lambda b, h: (h, b, 0, 0))
    return spec, jax.ShapeDtypeStruct((N_HEADS, batch, seq_len, HEAD_DIM), BF16)


def _cumsum_rows(g, reverse):
    n = g.shape[0]
    pos = lax.broadcasted_iota(jnp.int32, g.shape, 0)
    sh = 1
    while sh < n:
        if reverse:
            g = g + jnp.where(pos < n - sh, pltpu.roll(g, n - sh, 0), 0.0)
        else:
            g = g + jnp.where(pos >= sh, pltpu.roll(g, sh, 0), 0.0)
        sh *= 2
    return g


def _causal_mask(n, lower):
    t_idx = lax.broadcasted_iota(jnp.int32, (n, n), 0)
    s_idx = lax.broadcasted_iota(jnp.int32, (n, n), 1)
    return t_idx >= s_idx if lower else t_idx <= s_idx


def _hgrn_direct_scores(q, k, b, k_scr, b_scr):
    n = q.shape[0]
    k_scr[...] = k
    b_scr[...] = b
    col = lax.broadcasted_iota(jnp.int32, (n, n), 1)

    def body(s, acc):
        z = q * k_scr[pl.ds(s, 1), :] * jnp.exp(jnp.minimum(b - b_scr[pl.ds(s, 1), :], 0.0))
        return jnp.where(col == s, jnp.sum(z, axis=1, keepdims=True), acc)

    return lax.fori_loop(0, n, body, jnp.zeros((n, n), F32))


def _hgrn_kernel(*refs, seq_len, n_seqs, has_s0, has_earlier, emit_state):
    qa_ref, kf_ref, kb_ref, gate_ref, gf_hi_ref, gf_lo_ref, gb_hi_ref, gb_lo_ref, va_ref, gn_ref = refs[:10]
    refs = refs[10:]
    s0_ref = earlier_ref = None
    if has_s0:
        s0_ref, refs = refs[0], refs[1:]
    if has_earlier:
        earlier_ref, refs = refs[0], refs[1:]
    mix_ref, refs = refs[0], refs[1:]
    st_out_ref = None
    if emit_state:
        st_out_ref, refs = refs[0], refs[1:]
    tri_scr, cum_scr, a16_scr, qe_scr, o_scr, u_scr, dec_scr, st_scr, safe_smem, k_scr, b_scr = refs

    c_len = HGRN_CHUNK
    blk_len = min(HGRN_BLOCK, seq_len)
    n_chunks = seq_len // c_len
    n_blocks = seq_len // blk_len
    chunks_per_blk = blk_len // c_len
    directions = ((kf_ref, gf_hi_ref, gf_lo_ref, False), (kb_ref, gb_hi_ref, gb_lo_ref, True))
    causal = (_causal_mask(c_len, True), _causal_mask(c_len, False))

    @pl.when(jnp.logical_and(pl.program_id(0) == 0, pl.program_id(1) == 0))
    def _():
        t_idx = lax.broadcasted_iota(jnp.int32, (blk_len, blk_len), 0)
        s_idx = lax.broadcasted_iota(jnp.int32, (blk_len, blk_len), 1)
        same_chunk = (t_idx // c_len) == (s_idx // c_len)
        tri_scr[0] = jnp.where(jnp.logical_and(same_chunk, s_idx <= t_idx), 1.0, 0.0).astype(BF16)
        tri_scr[1] = jnp.where(jnp.logical_and(same_chunk, s_idx >= t_idx), 1.0, 0.0).astype(BF16)

    safe_smem[n_chunks] = 0

    def cumulate(blk, slot):
        rows = _rows(blk * blk_len, blk_len)
        for d, (_, g_hi_ref, g_lo_ref, _) in enumerate(directions):
            cum = _dot(tri_scr[d], jnp.concatenate([g_hi_ref[rows, :], g_lo_ref[rows, :]], axis=1))
            cum_scr[slot, d] = cum[:, :HEAD_DIM] + cum[:, HEAD_DIM:]

    def prepare(blk, slot):
        r0 = blk * blk_len
        rows = _rows(r0, blk_len)
        q = qa_ref[rows, :]
        v16 = va_ref[rows, :]
        q_ok = jnp.max(jnp.abs(q)) <= SAFE_Q_MAX
        gates = [(k_ref[rows, :], cum_scr[slot, d]) for d, (k_ref, _, _, _) in enumerate(directions)]
        for j in range(chunks_per_blk):
            sl = slice(j * c_len, (j + 1) * c_len)
            rows_j = _rows(r0 + j * c_len, c_len)
            c = blk * chunks_per_blk + j
            q_e, k_e, dec, span, scores = [], [], [], None, None
            for d, (k, b) in enumerate(gates):
                k_j, b_j = k[sl], b[sl]
                tot = b_j[0:1] if directions[d][3] else b_j[c_len - 1 : c_len]
                dist = b_j - b_j[c_len // 2 - 1 : c_len // 2]
                a = _dot_nt((q[sl] * jnp.exp(dist)).astype(BF16), (k_j * jnp.exp(-dist)).astype(BF16))
                a = jnp.where(causal[d], a, 0.0)
                scores = a if scores is None else scores + a
                q_e.append(q[sl] * jnp.exp(b_j))
                k_e.append(k_j * jnp.exp(tot - b_j))
                dec.append(jnp.exp(tot))
                ends = jnp.maximum(jnp.abs(dist[0:1]), jnp.abs(dist[c_len - 1 : c_len]))
                span = ends if span is None else jnp.maximum(span, ends)
            a16_scr[rows_j, :] = scores.astype(BF16)
            qe_scr[rows_j, :] = jnp.concatenate(q_e, axis=1).astype(BF16)
            u_scr[c] = _state_increment(v16[sl], jnp.concatenate(k_e, axis=1).astype(BF16))
            dec_scr[c] = jnp.concatenate(dec, axis=1)
            safe = jnp.logical_and(q_ok, jnp.max(span) <= SAFE_EXP_RANGE).astype(jnp.int32)
            safe_smem[c] = safe
            safe_smem[n_chunks] = safe_smem[n_chunks] + (1 - safe)

    _skewed_stages(n_blocks, cumulate, prepare, steps_per_trip=BLOCKS_PER_TRIP)

    _scan_states(n_chunks, n_seqs, lambda c: dec_scr[c], u_scr, st_scr, s0_ref, earlier_ref, st_out_ref)

    def redo_scores(c, carry):
        @pl.when(safe_smem[c] == 0)
        def _():
            rows = _rows(c * c_len, c_len)
            q = qa_ref[rows, :]
            scores = jnp.zeros((c_len, c_len), F32)
            for d, (k_ref, g_hi_ref, g_lo_ref, rev) in enumerate(directions):
                b = _cumsum_rows(g_hi_ref[rows, :].astype(F32) + g_lo_ref[rows, :].astype(F32), rev)
                scores = scores + jnp.where(causal[d], _hgrn_direct_scores(q, k_ref[rows, :], b, k_scr, b_scr), 0.0)
            a16_scr[rows, :] = scores.astype(BF16)

        return carry

    @pl.when(safe_smem[n_chunks] > 0)
    def _():
        lax.fori_loop(0, n_chunks, redo_scores, 0)

    def combine(blk, slot):
        for j in range(chunks_per_blk):
            c = blk * chunks_per_blk + j
            rows = _rows(c * c_len, c_len)
            o = _dot(a16_scr[rows, :], va_ref[rows, :]) + _dot_nt(qe_scr[rows, :], st_scr[c])
            o_scr[slot, j * c_len : (j + 1) * c_len, :] = o

    gn = gn_ref[...]

    def head_norm_gate(blk, slot):
        rows = _rows(blk * blk_len, blk_len)
        o = o_scr[slot]
        o = o * lax.rsqrt(jnp.mean(o * o, axis=-1, keepdims=True) + EPS) * gn
        mix_ref[rows, :] = (o * gate_ref[rows, :]).astype(BF16)

    _skewed_stages(n_blocks, combine, head_norm_gate, steps_per_trip=BLOCKS_PER_TRIP)


def _hgrn(p32, p16, norm_gain, state, layer, emit_state, earlier, n_seqs):
    p32, p16 = _fold_sequences(p32, n_seqs), _fold_sequences(p16, n_seqs)
    _, batch, seq_len, _ = p32.shape
    n_chunks = seq_len // HGRN_CHUNK
    blk_len = min(HGRN_BLOCK, seq_len)
    has_s0 = state is not None
    f32_slabs = (P32_QA, P32_KF, P32_KB, P32_GATE_A)
    bf16_slabs = (P16_GF_HI, P16_GF_LO, P16_GB_HI, P16_GB_LO, P16_VA)
    in_specs = [_slab_spec(seq_len, s) for s in f32_slabs + bf16_slabs]
    in_specs.append(pl.BlockSpec((None, 1, HEAD_DIM), lambda b, h: (layer, 0, h)))
    args = [p32] * len(f32_slabs) + [p16] * len(bf16_slabs) + [norm_gain]
    if has_s0:
        in_specs.append(_state_in_spec(layer, n_seqs))
        args.append(state)
    mix_spec, mix_shape = _mix_out(batch, seq_len)
    out_specs, out_shape = [mix_spec], [mix_shape]
    if emit_state:
        _state_outputs(earlier, n_seqs, batch, in_specs, args, out_specs, out_shape)
    outs = pl.pallas_call(
        functools.partial(
            _hgrn_kernel, seq_len=seq_len, n_seqs=n_seqs, has_s0=has_s0,
            has_earlier=emit_state and earlier is not None, emit_state=emit_state,
        ),
        grid=(batch, N_HEADS),
        in_specs=in_specs,
        out_specs=out_specs,
        out_shape=out_shape,
        scratch_shapes=[
            pltpu.VMEM((2, blk_len, blk_len), BF16),
            pltpu.VMEM((2, 2, blk_len, HEAD_DIM), F32),
            pltpu.VMEM((seq_len, HGRN_CHUNK), BF16),
            pltpu.VMEM((seq_len, 2 * HEAD_DIM), BF16),
            pltpu.VMEM((2, blk_len, HEAD_DIM), F32),
            pltpu.VMEM((n_chunks, HEAD_DIM, 2 * HEAD_DIM), F32),
            pltpu.VMEM((n_chunks, 1, 2 * HEAD_DIM), F32),
            pltpu.VMEM((n_chunks, HEAD_DIM, 2 * HEAD_DIM), BF16),
            pltpu.SMEM((n_chunks + 1,), jnp.int32),
            pltpu.VMEM((HGRN_CHUNK, HEAD_DIM), F32),
            pltpu.VMEM((HGRN_CHUNK, HEAD_DIM), F32),
        ],
        compiler_params=pltpu.CompilerParams(vmem_limit_bytes=VMEM_LIMIT),
        name="hgrn_scan",
    )(*args)
    return (outs[0], outs[1]) if emit_state else (outs[0], None)


def _log_sigmoid(x):
    return jnp.minimum(x, 0.0) - jnp.log1p(jnp.exp(-jnp.abs(x)))


def _ret_kernel(*refs, layer, seq_len, n_seqs, has_s0, has_earlier, emit_state):
    logit_ref, q_ref, k_ref, v_ref, gate_ref, gn_ref = refs[:6]
    refs = refs[6:]
    s0_ref = earlier_ref = None
    if has_s0:
        s0_ref, refs = refs[0], refs[1:]
    if has_earlier:
        earlier_ref, refs = refs[0], refs[1:]
    mix_ref, refs = refs[0], refs[1:]
    st_out_ref = None
    if emit_state:
        st_out_ref, refs = refs[0], refs[1:]
    a16_scr, o_scr, u_scr, st_scr = refs

    c_len = RET_CHUNK
    n_chunks = seq_len // c_len
    head = pl.program_id(1)
    lg_f = _log_sigmoid(jnp.full((1, HEAD_DIM), logit_ref[layer, 0, head], F32))
    lg_b = _log_sigmoid(jnp.full((1, HEAD_DIM), logit_ref[layer, 1, head], F32))

    t = lax.broadcasted_iota(jnp.int32, (c_len, HEAD_DIM), 0).astype(F32)
    q_dec_f = jnp.exp((t + 1.0) * lg_f)
    q_dec_b = jnp.exp((c_len - t) * lg_b)
    k_dec = jnp.concatenate([jnp.exp((c_len - 1.0 - t) * lg_f), jnp.exp(t * lg_b)], axis=1)
    chunk_dec = jnp.concatenate([jnp.exp(c_len * lg_f), jnp.exp(c_len * lg_b)], axis=1)
    t_idx = lax.broadcasted_iota(jnp.int32, (c_len, c_len), 0)
    s_idx = lax.broadcasted_iota(jnp.int32, (c_len, c_len), 1)
    dist = (t_idx - s_idx).astype(F32)
    decay = jnp.where(t_idx >= s_idx, jnp.exp(jnp.maximum(dist, 0.0) * lg_f), 0.0)
    decay = decay + jnp.where(t_idx <= s_idx, jnp.exp(jnp.maximum(-dist, 0.0) * lg_b), 0.0)

    def prepare(c, carry):
        rows = _rows(c * c_len, c_len)
        k16 = k_ref[rows, :]
        a16_scr[rows, :] = (_dot_nt(q_ref[rows, :], k16) * decay).astype(BF16)
        k = k16.astype(F32)
        u_scr[c] = _state_increment(v_ref[rows, :], (jnp.concatenate([k, k], axis=1) * k_dec).astype(BF16))
        return carry

    lax.fori_loop(0, n_chunks, prepare, 0, unroll=min(RET_CHUNKS_PER_TRIP, n_chunks))

    _scan_states(n_chunks, n_seqs, lambda c: chunk_dec, u_scr, st_scr, s0_ref, earlier_ref, st_out_ref)

    blk_len = min(RET_BLOCK, seq_len)
    chunks_per_blk = blk_len // c_len

    def combine(blk, slot):
        for j in range(chunks_per_blk):
            c = blk * chunks_per_blk + j
            rows = _rows(c * c_len, c_len)
            q16 = q_ref[rows, :]
            o = _dot(a16_scr[rows, :], v_ref[rows, :])
            o = o + q_dec_f * _dot_nt(q16, st_scr[c, :, :HEAD_DIM]) + q_dec_b * _dot_nt(q16, st_scr[c, :, HEAD_DIM:])
            o_scr[slot, j * c_len : (j + 1) * c_len, :] = o

    gn = gn_ref[...]

    def head_norm_gate(blk, slot):
        rows = _rows(blk * blk_len, blk_len)
        o = o_scr[slot]
        o = o - jnp.mean(o, axis=-1, keepdims=True)
        o = o * lax.rsqrt(jnp.mean(o * o, axis=-1, keepdims=True) + EPS) * gn
        mix_ref[rows, :] = (o * gate_ref[rows, :]).astype(BF16)

    _skewed_stages(seq_len // blk_len, combine, head_norm_gate, steps_per_trip=BLOCKS_PER_TRIP)


def _ret(p32, p16, decay_logit, norm_gain, state, layer, emit_state, earlier, n_seqs):
    p32, p16 = _fold_sequences(p32, n_seqs), _fold_sequences(p16, n_seqs)
    _, batch, seq_len, _ = p32.shape
    n_chunks = seq_len // RET_CHUNK
    has_s0 = state is not None
    in_specs = [pl.BlockSpec(memory_space=pltpu.SMEM)]
    in_specs += [_slab_spec(seq_len, s) for s in (P16_QR, P16_KR, P16_VR, P32_GATE_R)]
    in_specs.append(pl.BlockSpec((None, 1, HEAD_DIM), lambda b, h: (layer, 0, h)))
    args = [decay_logit, p16, p16, p16, p32, norm_gain]
    if has_s0:
        in_specs.append(_state_in_spec(layer, n_seqs))
        args.append(state)
    mix_spec, mix_shape = _mix_out(batch, seq_len)
    out_specs, out_shape = [mix_spec], [mix_shape]
    if emit_state:
        _state_outputs(earlier, n_seqs, batch, in_specs, args, out_specs, out_shape)
    outs = pl.pallas_call(
        functools.partial(
            _ret_kernel, layer=layer, seq_len=seq_len, n_seqs=n_seqs, has_s0=has_s0,
            has_earlier=emit_state and earlier is not None, emit_state=emit_state,
        ),
        grid=(batch, N_HEADS),
        in_specs=in_specs,
        out_specs=out_specs,
        out_shape=out_shape,
        scratch_shapes=[
            pltpu.VMEM((seq_len, RET_CHUNK), BF16),
            pltpu.VMEM((2, min(RET_BLOCK, seq_len), HEAD_DIM), F32),
            pltpu.VMEM((n_chunks, HEAD_DIM, 2 * HEAD_DIM), F32),
            pltpu.VMEM((n_chunks, HEAD_DIM, 2 * HEAD_DIM), BF16),
        ],
        compiler_params=pltpu.CompilerParams(vmem_limit_bytes=VMEM_LIMIT),
        name="ret_scan",
    )(*args)
    return (outs[0], outs[1]) if emit_state else (outs[0], None)


def _out_ffn_kernel(*refs, d_ff, final):
    ma_ref, mb_ref, x_ref, mod_ref, gain_ref, wo_ref, wu_ref, wd_ref = refs[:8]
    refs = refs[8:]
    if final:
        gfin_ref, refs = refs[0], refs[1:]
    out_ref, act_scr = refs

    mod = mod_ref[...]
    mix = jnp.concatenate([ma_ref[hd] for hd in range(N_HEADS)] + [mb_ref[hd] for hd in range(N_HEADS)], axis=1)
    x1 = x_ref[...] + mod[2:3] * _dot(mix, wo_ref[...])
    h = _modnorm(x1, gain_ref[...], mod[4:5], mod[3:4]).astype(BF16)
    for c in range(d_ff // FFN_CHUNK):
        lo = c * FFN_CHUNK
        gate = _dot(h, wu_ref[:, lo : lo + FFN_CHUNK])
        up = _dot(h, wu_ref[:, d_ff + lo : d_ff + lo + FFN_CHUNK])
        act_scr[:, lo : lo + FFN_CHUNK] = (_silu(gate) * up).astype(BF16)
    x2 = x1 + mod[5:6] * _dot(act_scr[...], wd_ref[...])
    if final:
        x2 = x2 * lax.rsqrt(jnp.mean(x2 * x2, axis=-1, keepdims=True) + EPS) * gfin_ref[...]
    out_ref[...] = x2


def _out_ffn(mix_a, mix_b, x, mod, gain, w_out, w_up, w_down, final_gain, layer, tile_m, cond_index):
    n_tok, d_model = x.shape
    d_ff = w_down.shape[1]
    final = final_gain is not None
    const = lambda i: (layer, 0, 0)
    in_specs = [
        pl.BlockSpec((N_HEADS, tile_m, HEAD_DIM), lambda i: (0, i, 0)),
        pl.BlockSpec((N_HEADS, tile_m, HEAD_DIM), lambda i: (0, i, 0)),
        pl.BlockSpec((tile_m, d_model), lambda i: (i, 0)),
        pl.BlockSpec((None, None, N_MOD, d_model), lambda i: (layer, cond_index(i), 0, 0)),
        pl.BlockSpec((None, 1, d_model), const),
        pl.BlockSpec((None, 2 * N_HEADS * HEAD_DIM, d_model), const, pipeline_mode=pl.Buffered(1)),
        pl.BlockSpec((None, d_model, 2 * d_ff), const, pipeline_mode=pl.Buffered(1)),
        pl.BlockSpec((None, d_ff, d_model), const, pipeline_mode=pl.Buffered(1)),
    ]
    args = [mix_a, mix_b, x, mod, gain, w_out, w_up, w_down]
    if final:
        in_specs.append(pl.BlockSpec((1, d_model), lambda i: (0, 0)))
        args.append(final_gain)
    return pl.pallas_call(
        functools.partial(_out_ffn_kernel, d_ff=d_ff, final=final),
        grid=(n_tok // tile_m,),
        in_specs=in_specs,
        out_specs=pl.BlockSpec((tile_m, d_model), lambda i: (i, 0)),
        out_shape=jax.ShapeDtypeStruct((n_tok, d_model), F32),
        scratch_shapes=[pltpu.VMEM((tile_m, d_ff), BF16)],
        compiler_params=pltpu.CompilerParams(vmem_limit_bytes=VMEM_LIMIT),
        name="out_ffn",
    )(*args)


def kernel(x_prompt, x_sample, state_hgrn, state_ret, c, c_ctx, ada_w, ada_b, norm_mix, norm_ffn, w_in,
           hgrn_lb_logits, hgrn_norm, ret_decay_logit, ret_norm, w_out, w_up, w_down, norm_final):
    batch, seq, d_model = x_prompt.shape
    dec_batch, dec_seq, _ = x_sample.shape
    depth = ada_w.shape[0]
    tile_m = DENSE_TILE_M
    assert w_in.shape[2] == D_IN and state_hgrn.shape[3] == N_HEADS and state_ret.shape[3] == N_HEADS
    assert dec_batch < COND_ROWS
    for rows in (tile_m, FFN_TILE_M):
        assert dec_seq % rows == 0 and (batch * seq) % rows == 0
    assert dec_seq % GRID_W == 0
    for n_l in (seq, dec_seq):
        assert n_l % min(HGRN_BLOCK, n_l) == 0 and min(HGRN_BLOCK, n_l) % HGRN_CHUNK == 0
        assert n_l % min(RET_BLOCK, n_l) == 0 and n_l % RET_CHUNK == 0

    cond = jnp.concatenate(
        [c.astype(F32), c_ctx.astype(F32)[None, :], jnp.zeros((COND_ROWS - dec_batch - 1, d_model), F32)], axis=0
    )
    mod = _modulation(cond, ada_w, ada_b)

    w_in16, w_out16, w_up16, w_down16 = (w.astype(BF16) for w in (w_in, w_out, w_up, w_down))
    gain_mix = norm_mix.reshape(depth, 1, d_model)
    gain_ffn = norm_ffn.reshape(depth, 1, d_model)
    gain_hgrn = hgrn_norm.reshape(depth, 1, -1)
    gain_ret = ret_norm.reshape(depth, 1, -1)
    gain_final = norm_final.reshape(1, d_model)
    rope_tables = _rope_tables(dec_seq)

    tiles_per_seq = dec_seq // tile_m
    groups = [
        [x_prompt.astype(F32).reshape(batch * seq, d_model), batch, seq, lambda rows: (lambda i: dec_batch),
         None, None, None, True],
        [x_sample.astype(F32).reshape(dec_batch * dec_seq, d_model), dec_batch, dec_seq,
         lambda rows: (lambda i: i // (dec_seq // rows)), state_hgrn, state_ret, rope_tables, False],
    ]
    new_h = new_r = None
    for layer in range(depth):
        final_gain = gain_final if layer == depth - 1 else None
        for grp in groups:
            x, n_b, n_l, cond_of, s0_h, s0_r, tables, emit = grp
            p32, p16 = _in_proj(
                x, mod, gain_mix, w_in16, hgrn_lb_logits, tables, layer, tile_m, cond_of(tile_m), tiles_per_seq
            )
            p32 = p32.reshape(N_P32, n_b, n_l, HEAD_DIM)
            p16 = p16.reshape(N_P16, n_b, n_l, HEAD_DIM)
            n_seqs = max(1, min(n_b, SCAN_ROWS_PER_STEP // n_l))
            mix_a, st_a = _hgrn(p32, p16, gain_hgrn, s0_h, layer, emit, new_h, n_seqs)
            mix_b, st_b = _ret(p32, p16, ret_decay_logit, gain_ret, s0_r, layer, emit, new_r, n_seqs)
            grp[0] = _out_ffn(
                mix_a.reshape(N_HEADS, n_b * n_l, HEAD_DIM), mix_b.reshape(N_HEADS, n_b * n_l, HEAD_DIM), x, mod,
                gain_ffn, w_out16, w_up16, w_down16, final_gain, layer, FFN_TILE_M, cond_of(FFN_TILE_M),
            )
            if emit:
                new_h, new_r = st_a, st_b
    y_prompt = groups[0][0].reshape(batch, seq, d_model).astype(x_prompt.dtype)
    y_sample = groups[1][0].reshape(dec_batch, dec_seq, d_model).astype(x_sample.dtype)
    new_state_hgrn = new_h.astype(state_hgrn.dtype)
    new_state_ret = new_r.astype(state_ret.dtype)
    return (y_prompt, y_sample, new_state_hgrn, new_state_ret)
```

```python
import functools

import jax
import jax.numpy as jnp
import numpy as np
from jax import lax
from jax.experimental import pallas as pl
from jax.experimental.pallas import tpu as pltpu

F32 = jnp.float32
BF16 = jnp.bfloat16

N_HEADS = 4
HEAD_DIM = 128
N_SECTIONS = 9
SECTION = N_HEADS * HEAD_DIM
D_IN = N_SECTIONS * SECTION
GRID_W = 64
ROPE_BASE = 10000.0
EPS = 1e-6
GATE_FLOOR = 1e-12

P32_QA, P32_KF, P32_KB, P32_GATE_A, P32_GATE_R = range(5)
P16_GF_HI, P16_GF_LO, P16_GB_HI, P16_GB_LO, P16_VA, P16_QR, P16_KR, P16_VR = range(8)
N_P32 = 5 * N_HEADS
N_P16 = 8 * N_HEADS

N_MOD = 6
COND_ROWS = 16

HGRN_CHUNK = 64
HGRN_BLOCK = 256
RET_CHUNK = 128
RET_BLOCK = 256
SCAN_ROWS_PER_STEP = 4096
SCAN_SEQS_PER_STEP = 8
BLOCKS_PER_TRIP = 8
RET_CHUNKS_PER_TRIP = 16
STATE_STEPS_INLINE = 4
STATE_STEPS_PER_TRIP = 2
SAFE_EXP_RANGE = 80.0
SAFE_Q_MAX = 1e3
FFN_CHUNK = 256
MOD_TILE_N = 1536
DENSE_TILE_M = 512
FFN_TILE_M = 1024
V7X_VMEM_BYTES = 64 * 1024 * 1024
VMEM_LIMIT = V7X_VMEM_BYTES - 8 * 1024 * 1024


def _silu(x):
    h = 0.5 * x
    return h + h * jnp.tanh(h)


def _dot(a, b):
    return jnp.dot(a, b, preferred_element_type=F32)


def _dot_nt(a, b):
    return lax.dot_general(a, b, (((1,), (1,)), ((), ())), preferred_element_type=F32)


def _split_bf16(x):
    hi = x.astype(BF16)
    return hi, (x - hi.astype(F32)).astype(BF16)


def _modnorm(x, gain, scale, shift):
    ms = jnp.mean(x * x, axis=-1, keepdims=True)
    return x * lax.rsqrt(ms + EPS) * gain * (1.0 + scale) + shift


def _mod_kernel(cond_ref, w_ref, b_ref, out_ref):
    s_hi, s_lo = _split_bf16(_silu(cond_ref[...]))
    w_hi, w_lo = _split_bf16(w_ref[...])
    s_both = jnp.concatenate([s_hi, s_lo], axis=0)
    p = _dot(s_both, w_hi) + _dot(s_both, w_lo)
    out_ref[...] = p[:COND_ROWS] + p[COND_ROWS:] + b_ref[...]


def _modulation(cond, ada_w, ada_b):
    depth, d_model, n_out = ada_w.shape
    out = pl.pallas_call(
        _mod_kernel,
        grid=(depth, n_out // MOD_TILE_N),
        in_specs=[
            pl.BlockSpec((COND_ROWS, d_model), lambda l, j: (0, 0)),
            pl.BlockSpec((None, d_model, MOD_TILE_N), lambda l, j: (l, 0, j)),
            pl.BlockSpec((None, 1, MOD_TILE_N), lambda l, j: (l, 0, j)),
        ],
        out_specs=pl.BlockSpec((None, COND_ROWS, MOD_TILE_N), lambda l, j: (l, 0, j)),
        out_shape=jax.ShapeDtypeStruct((depth, COND_ROWS, n_out), F32),
        compiler_params=pltpu.CompilerParams(vmem_limit_bytes=VMEM_LIMIT),
        name="adaln_mod",
    )(cond, ada_w, ada_b.reshape(depth, 1, n_out))
    return out.reshape(depth, COND_ROWS, N_MOD, d_model)


def _rope(x, cos, sin_signed):
    lane = lax.broadcasted_iota(jnp.int32, x.shape, 1)
    quarter = HEAD_DIM // 4
    partner = jnp.where(lane % (2 * quarter) < quarter, pltpu.roll(x, HEAD_DIM - quarter, 1), pltpu.roll(x, quarter, 1))
    return x * cos + partner * sin_signed


def _hgrn_lower_bounds(logits, layer):
    e = jnp.exp(logits - jnp.max(logits, axis=0, keepdims=True))
    p = e / jnp.sum(e, axis=0, keepdims=True)
    cum = p[0]
    for i in range(1, layer + 1):
        cum = cum + p[i]
    return jnp.clip(cum - p[0], 0.0, 1.0)


def _in_proj_kernel(*refs, layer, use_rope):
    x_ref, mod_ref, gain_ref, w_ref, lbl_ref = refs[:5]
    refs = refs[5:]
    if use_rope:
        cos_ref, sin_ref, refs = refs[0], refs[1], refs[2:]
    p32_ref, p16_ref = refs

    mod = mod_ref[...]
    h = _modnorm(x_ref[...], gain_ref[...], mod[1:2], mod[0:1]).astype(BF16)
    lb = _hgrn_lower_bounds(lbl_ref[...], layer)

    def proj(j):
        return _dot(h, w_ref[:, j * SECTION : (j + 1) * SECTION])

    def put(ref, section, val):
        for hd in range(N_HEADS):
            ref[section * N_HEADS + hd] = val[:, hd * HEAD_DIM : (hd + 1) * HEAD_DIM].astype(ref.dtype)

    put(p32_ref, P32_QA, _silu(proj(0)))
    for d, (sec_hi, sec_lo, sec_k) in enumerate(
        ((P16_GF_HI, P16_GF_LO, P32_KF), (P16_GB_HI, P16_GB_LO, P32_KB))
    ):
        lb_d = lb[d : d + 1]
        c0, c1 = 0.5 + 0.5 * lb_d, 0.5 - 0.5 * lb_d
        ct = c1 * jnp.tanh(0.5 * proj(1 + d))
        g_hi, g_lo = _split_bf16(jnp.log(jnp.maximum(c0 + ct, GATE_FLOOR)))
        put(p16_ref, sec_hi, g_hi)
        put(p16_ref, sec_lo, g_lo)
        put(p32_ref, sec_k, c1 - ct)
    put(p16_ref, P16_VA, proj(3))
    put(p32_ref, P32_GATE_A, _silu(proj(4)))
    q = proj(5)
    k = proj(6) * (HEAD_DIM ** -0.5)
    if use_rope:
        cos, sin = cos_ref[...], sin_ref[...]
        heads = [slice(hd * HEAD_DIM, (hd + 1) * HEAD_DIM) for hd in range(N_HEADS)]
        q = jnp.concatenate([_rope(q[:, sl], cos, sin) for sl in heads], axis=1)
        k = jnp.concatenate([_rope(k[:, sl], cos, sin) for sl in heads], axis=1)
    put(p16_ref, P16_QR, q)
    put(p16_ref, P16_KR, k)
    put(p16_ref, P16_VR, proj(7))
    put(p32_ref, P32_GATE_R, _silu(proj(8)))


def _in_proj(x, mod, gain, w, lb_logits, rope_tables, layer, tile_m, cond_index, tiles_per_seq):
    n_tok, d_model = x.shape
    depth = lb_logits.shape[0]
    use_rope = rope_tables is not None
    in_specs = [
        pl.BlockSpec((tile_m, d_model), lambda i: (i, 0)),
        pl.BlockSpec((None, None, N_MOD, d_model), lambda i: (layer, cond_index(i), 0, 0)),
        pl.BlockSpec((None, 1, d_model), lambda i: (layer, 0, 0)),
        pl.BlockSpec((None, d_model, D_IN), lambda i: (layer, 0, 0), pipeline_mode=pl.Buffered(1)),
        pl.BlockSpec((depth, 2, SECTION), lambda i: (0, 0, 0)),
    ]
    args = [x, mod, gain, w, lb_logits]
    if use_rope:
        in_specs += [pl.BlockSpec((tile_m, HEAD_DIM), lambda i: (i % tiles_per_seq, 0))] * 2
        args += list(rope_tables)
    return pl.pallas_call(
        functools.partial(_in_proj_kernel, layer=layer, use_rope=use_rope),
        grid=(n_tok // tile_m,),
        in_specs=in_specs,
        out_specs=[
            pl.BlockSpec((N_P32, tile_m, HEAD_DIM), lambda i: (0, i, 0)),
            pl.BlockSpec((N_P16, tile_m, HEAD_DIM), lambda i: (0, i, 0)),
        ],
        out_shape=[
            jax.ShapeDtypeStruct((N_P32, n_tok, HEAD_DIM), F32),
            jax.ShapeDtypeStruct((N_P16, n_tok, HEAD_DIM), BF16),
        ],
        compiler_params=pltpu.CompilerParams(vmem_limit_bytes=VMEM_LIMIT),
        name="in_proj",
    )(*args)


def _rope_tables(seq_len):
    quarter = HEAD_DIM // 4
    inv = np.float32(ROPE_BASE) ** (-np.arange(quarter, dtype=np.float32) / np.float32(quarter))
    pos = np.arange(seq_len, dtype=np.int32)
    ang_r = (pos // GRID_W).astype(np.float32)[:, None] * inv
    ang_c = (pos % GRID_W).astype(np.float32)[:, None] * inv
    cos = np.concatenate([np.cos(ang_r), np.cos(ang_r), np.cos(ang_c), np.cos(ang_c)], axis=1)
    sin = np.concatenate([-np.sin(ang_r), np.sin(ang_r), -np.sin(ang_c), np.sin(ang_c)], axis=1)
    return jnp.asarray(cos, F32), jnp.asarray(sin, F32)


def _rows(start, size):
    return pl.ds(start if isinstance(start, int) else pl.multiple_of(start, size), size)


def _skewed_stages(n_steps, produce, consume, steps_per_trip):
    steps_per_trip = min(steps_per_trip, n_steps)
    assert steps_per_trip % 2 == 0 and n_steps % steps_per_trip == 0
    produce(0, 0)
    if steps_per_trip == n_steps:
        for j in range(n_steps):
            if j + 1 < n_steps:
                produce(j + 1, (j + 1) % 2)
            consume(j, j % 2)
        return

    def body(trip, carry):
        first = steps_per_trip * trip
        for j in range(steps_per_trip):
            produce(jnp.minimum(first + j + 1, n_steps - 1), (j + 1) % 2)
            consume(first + j, j % 2)
        return carry

    lax.fori_loop(0, n_steps // steps_per_trip, body, 0)


def _state_increment(v16, k_cat):
    return _dot(v16.astype(F32).T.astype(BF16), k_cat)


def _run_states(n_chunks, first, dec_of, u_scr, st_scr, s_f, s_b):
    def step(i, carry):
        s_f, s_b = carry
        cf, cb = first + i, first + n_chunks - 1 - i
        st_scr[cf, :, :HEAD_DIM] = s_f.astype(BF16)
        st_scr[cb, :, HEAD_DIM:] = s_b.astype(BF16)
        s_f = dec_of(cf)[:, :HEAD_DIM] * s_f + u_scr[cf, :, :HEAD_DIM]
        s_b = dec_of(cb)[:, HEAD_DIM:] * s_b + u_scr[cb, :, HEAD_DIM:]
        return s_f, s_b

    if n_chunks <= STATE_STEPS_INLINE:
        carry = (s_f, s_b)
        for i in range(n_chunks):
            carry = step(i, carry)
        return carry
    return lax.fori_loop(0, n_chunks, step, (s_f, s_b), unroll=STATE_STEPS_PER_TRIP)


def _scan_states(n_chunks, n_seqs, dec_of, u_scr, st_scr, s0_ref, earlier_ref, st_out_ref):
    per_seq = n_chunks // n_seqs
    n_earlier = 0
    if earlier_ref is not None:
        n_earlier = earlier_ref.shape[1]
        st_out_ref[:, :n_earlier] = earlier_ref[...]
    for s in range(n_seqs):
        if s0_ref is None:
            s_f = s_b = jnp.zeros((HEAD_DIM, HEAD_DIM), F32)
        else:
            s_f, s_b = s0_ref[s, 0].T, s0_ref[s, 1].T
        s_f, s_b = _run_states(per_seq, s * per_seq, dec_of, u_scr, st_scr, s_f, s_b)
        if st_out_ref is not None:
            st_out_ref[s, n_earlier, 0] = s_f.T
            st_out_ref[s, n_earlier, 1] = s_b.T


def _slab_spec(seq_len, section):
    return pl.BlockSpec((None, None, seq_len, HEAD_DIM), lambda b, h: (section * N_HEADS + h, b, 0, 0))


def _state_in_spec(layer, n_seqs):
    return pl.BlockSpec((n_seqs, None, 2, None, HEAD_DIM, HEAD_DIM), lambda b, h: (b, layer, 0, h, 0, 0))


def _stacked_state_spec(n_seqs, n_layers):
    return pl.BlockSpec((n_seqs, n_layers, 2, None, HEAD_DIM, HEAD_DIM), lambda b, h: (b, 0, 0, h, 0, 0))


def _state_outputs(earlier, n_seqs, batch, in_specs, args, out_specs, out_shape):
    n_layers = 1
    if earlier is not None:
        n_layers += earlier.shape[1]
        in_specs.append(_stacked_state_spec(n_seqs, n_layers - 1))
        args.append(earlier)
    out_specs.append(_stacked_state_spec(n_seqs, n_layers))
    out_shape.append(jax.ShapeDtypeStruct((batch * n_seqs, n_layers, 2, N_HEADS, HEAD_DIM, HEAD_DIM), F32))


def _fold_sequences(slabs, n_seqs):
    n, batch, seq_len, d = slabs.shape
    assert batch % n_seqs == 0
    return slabs.reshape(n, batch // n_seqs, n_seqs * seq_len, d)


def _mix_out(batch, seq_len):
    spec = pl.BlockSpec((None, None, seq_len, HEAD_DIM), lambda b, h: (h, b, 0, 0))
    return spec, jax.ShapeDtypeStruct((N_HEADS, batch, seq_len, HEAD_DIM), BF16)


def _cumsum_rows(g, reverse):
    n = g.shape[0]
    pos = lax.broadcasted_iota(jnp.int32, g.shape, 0)
    sh = 1
    while sh < n:
        if reverse:
            g = g + jnp.where(pos < n - sh, pltpu.roll(g, n - sh, 0), 0.0)
        else:
            g = g + jnp.where(pos >= sh, pltpu.roll(g, sh, 0), 0.0)
        sh *= 2
    return g


def _causal_mask(n, lower):
    t_idx = lax.broadcasted_iota(jnp.int32, (n, n), 0)
    s_idx = lax.broadcasted_iota(jnp.int32, (n, n), 1)
    return t_idx >= s_idx if lower else t_idx <= s_idx


def _hgrn_direct_scores(q, k, b, k_scr, b_scr):
    n = q.shape[0]
    k_scr[...] = k
    b_scr[...] = b
    col = lax.broadcasted_iota(jnp.int32, (n, n), 1)

    def body(s, acc):
        z = q * k_scr[pl.ds(s, 1), :] * jnp.exp(jnp.minimum(b - b_scr[pl.ds(s, 1), :], 0.0))
        return jnp.where(col == s, jnp.sum(z, axis=1, keepdims=True), acc)

    return lax.fori_loop(0, n, body, jnp.zeros((n, n), F32))


def _hgrn_kernel(*refs, seq_len, n_seqs, has_s0, has_earlier, emit_state):
    qa_ref, kf_ref, kb_ref, gate_ref, gf_hi_ref, gf_lo_ref, gb_hi_ref, gb_lo_ref, va_ref, gn_ref = refs[:10]
    refs = refs[10:]
    s0_ref = earlier_ref = None
    if has_s0:
        s0_ref, refs = refs[0], refs[1:]
    if has_earlier:
        earlier_ref, refs = refs[0], refs[1:]
    mix_ref, refs = refs[0], refs[1:]
    st_out_ref = None
    if emit_state:
        st_out_ref, refs = refs[0], refs[1:]
    tri_scr, cum_scr, a16_scr, qe_scr, o_scr, u_scr, dec_scr, st_scr, safe_smem, k_scr, b_scr = refs

    c_len = HGRN_CHUNK
    blk_len = min(HGRN_BLOCK, seq_len)
    n_chunks = seq_len // c_len
    n_blocks = seq_len // blk_len
    chunks_per_blk = blk_len // c_len
    directions = ((kf_ref, gf_hi_ref, gf_lo_ref, False), (kb_ref, gb_hi_ref, gb_lo_ref, True))
    causal = (_causal_mask(c_len, True), _causal_mask(c_len, False))

    @pl.when(jnp.logical_and(pl.program_id(0) == 0, pl.program_id(1) == 0))
    def _():
        t_idx = lax.broadcasted_iota(jnp.int32, (blk_len, blk_len), 0)
        s_idx = lax.broadcasted_iota(jnp.int32, (blk_len, blk_len), 1)
        same_chunk = (t_idx // c_len) == (s_idx // c_len)
        tri_scr[0] = jnp.where(jnp.logical_and(same_chunk, s_idx <= t_idx), 1.0, 0.0).astype(BF16)
        tri_scr[1] = jnp.where(jnp.logical_and(same_chunk, s_idx >= t_idx), 1.0, 0.0).astype(BF16)

    safe_smem[n_chunks] = 0

    def cumulate(blk, slot):
        rows = _rows(blk * blk_len, blk_len)
        for d, (_, g_hi_ref, g_lo_ref, _) in enumerate(directions):
            cum = _dot(tri_scr[d], jnp.concatenate([g_hi_ref[rows, :], g_lo_ref[rows, :]], axis=1))
            cum_scr[slot, d] = cum[:, :HEAD_DIM] + cum[:, HEAD_DIM:]

    def prepare(blk, slot):
        r0 = blk * blk_len
        rows = _rows(r0, blk_len)
        q = qa_ref[rows, :]
        v16 = va_ref[rows, :]
        q_ok = jnp.max(jnp.abs(q)) <= SAFE_Q_MAX
        gates = [(k_ref[rows, :], cum_scr[slot, d]) for d, (k_ref, _, _, _) in enumerate(directions)]
        for j in range(chunks_per_blk):
            sl = slice(j * c_len, (j + 1) * c_len)
            rows_j = _rows(r0 + j * c_len, c_len)
            c = blk * chunks_per_blk + j
            q_e, k_e, dec, span, scores = [], [], [], None, None
            for d, (k, b) in enumerate(gates):
                k_j, b_j = k[sl], b[sl]
                tot = b_j[0:1] if directions[d][3] else b_j[c_len - 1 : c_len]
                dist = b_j - b_j[c_len // 2 - 1 : c_len // 2]
                a = _dot_nt((q[sl] * jnp.exp(dist)).astype(BF16), (k_j * jnp.exp(-dist)).astype(BF16))
                a = jnp.where(causal[d], a, 0.0)
                scores = a if scores is None else scores + a
                q_e.append(q[sl] * jnp.exp(b_j))
                k_e.append(k_j * jnp.exp(tot - b_j))
                dec.append(jnp.exp(tot))
                ends = jnp.maximum(jnp.abs(dist[0:1]), jnp.abs(dist[c_len - 1 : c_len]))
                span = ends if span is None else jnp.maximum(span, ends)
            a16_scr[rows_j, :] = scores.astype(BF16)
            qe_scr[rows_j, :] = jnp.concatenate(q_e, axis=1).astype(BF16)
            u_scr[c] = _state_increment(v16[sl], jnp.concatenate(k_e, axis=1).astype(BF16))
            dec_scr[c] = jnp.concatenate(dec, axis=1)
            safe = jnp.logical_and(q_ok, jnp.max(span) <= SAFE_EXP_RANGE).astype(jnp.int32)
            safe_smem[c] = safe
            safe_smem[n_chunks] = safe_smem[n_chunks] + (1 - safe)

    _skewed_stages(n_blocks, cumulate, prepare, steps_per_trip=BLOCKS_PER_TRIP)

    _scan_states(n_chunks, n_seqs, lambda c: dec_scr[c], u_scr, st_scr, s0_ref, earlier_ref, st_out_ref)

    def redo_scores(c, carry):
        @pl.when(safe_smem[c] == 0)
        def _():
            rows = _rows(c * c_len, c_len)
            q = qa_ref[rows, :]
            scores = jnp.zeros((c_len, c_len), F32)
            for d, (k_ref, g_hi_ref, g_lo_ref, rev) in enumerate(directions):
                b = _cumsum_rows(g_hi_ref[rows, :].astype(F32) + g_lo_ref[rows, :].astype(F32), rev)
                scores = scores + jnp.where(causal[d], _hgrn_direct_scores(q, k_ref[rows, :], b, k_scr, b_scr), 0.0)
            a16_scr[rows, :] = scores.astype(BF16)

        return carry

    @pl.when(safe_smem[n_chunks] > 0)
    def _():
        lax.fori_loop(0, n_chunks, redo_scores, 0)

    def combine(blk, slot):
        for j in range(chunks_per_blk):
            c = blk * chunks_per_blk + j
            rows = _rows(c * c_len, c_len)
            o = _dot(a16_scr[rows, :], va_ref[rows, :]) + _dot_nt(qe_scr[rows, :], st_scr[c])
            o_scr[slot, j * c_len : (j + 1) * c_len, :] = o

    gn = gn_ref[...]

    def head_norm_gate(blk, slot):
        rows = _rows(blk * blk_len, blk_len)
        o = o_scr[slot]
        o = o * lax.rsqrt(jnp.mean(o * o, axis=-1, keepdims=True) + EPS) * gn
        mix_ref[rows, :] = (o * gate_ref[rows, :]).astype(BF16)

    _skewed_stages(n_blocks, combine, head_norm_gate, steps_per_trip=BLOCKS_PER_TRIP)


def _hgrn(p32, p16, norm_gain, state, layer, emit_state, earlier, n_seqs):
    p32, p16 = _fold_sequences(p32, n_seqs), _fold_sequences(p16, n_seqs)
    _, batch, seq_len, _ = p32.shape
    n_chunks = seq_len // HGRN_CHUNK
    blk_len = min(HGRN_BLOCK, seq_len)
    has_s0 = state is not None
    f32_slabs = (P32_QA, P32_KF, P32_KB, P32_GATE_A)
    bf16_slabs = (P16_GF_HI, P16_GF_LO, P16_GB_HI, P16_GB_LO, P16_VA)
    in_specs = [_slab_spec(seq_len, s) for s in f32_slabs + bf16_slabs]
    in_specs.append(pl.BlockSpec((None, 1, HEAD_DIM), lambda b, h: (layer, 0, h)))
    args = [p32] * len(f32_slabs) + [p16] * len(bf16_slabs) + [norm_gain]
    if has_s0:
        in_specs.append(_state_in_spec(layer, n_seqs))
        args.append(state)
    mix_spec, mix_shape = _mix_out(batch, seq_len)
    out_specs, out_shape = [mix_spec], [mix_shape]
    if emit_state:
        _state_outputs(earlier, n_seqs, batch, in_specs, args, out_specs, out_shape)
    outs = pl.pallas_call(
        functools.partial(
            _hgrn_kernel, seq_len=seq_len, n_seqs=n_seqs, has_s0=has_s0,
            has_earlier=emit_state and earlier is not None, emit_state=emit_state,
        ),
        grid=(batch, N_HEADS),
        in_specs=in_specs,
        out_specs=out_specs,
        out_shape=out_shape,
        scratch_shapes=[
            pltpu.VMEM((2, blk_len, blk_len), BF16),
            pltpu.VMEM((2, 2, blk_len, HEAD_DIM), F32),
            pltpu.VMEM((seq_len, HGRN_CHUNK), BF16),
            pltpu.VMEM((seq_len, 2 * HEAD_DIM), BF16),
            pltpu.VMEM((2, blk_len, HEAD_DIM), F32),
            pltpu.VMEM((n_chunks, HEAD_DIM, 2 * HEAD_DIM), F32),
            pltpu.VMEM((n_chunks, 1, 2 * HEAD_DIM), F32),
            pltpu.VMEM((n_chunks, HEAD_DIM, 2 * HEAD_DIM), BF16),
            pltpu.SMEM((n_chunks + 1,), jnp.int32),
            pltpu.VMEM((HGRN_CHUNK, HEAD_DIM), F32),
            pltpu.VMEM((HGRN_CHUNK, HEAD_DIM), F32),
        ],
        compiler_params=pltpu.CompilerParams(vmem_limit_bytes=VMEM_LIMIT),
        name="hgrn_scan",
    )(*args)
    return (outs[0], outs[1]) if emit_state else (outs[0], None)


def _log_sigmoid(x):
    return jnp.minimum(x, 0.0) - jnp.log1p(jnp.exp(-jnp.abs(x)))


def _ret_kernel(*refs, layer, seq_len, n_seqs, has_s0, has_earlier, emit_state):
    logit_ref, q_ref, k_ref, v_ref, gate_ref, gn_ref = refs[:6]
    refs = refs[6:]
    s0_ref = earlier_ref = None
    if has_s0:
        s0_ref, refs = refs[0], refs[1:]
    if has_earlier:
        earlier_ref, refs = refs[0], refs[1:]
    mix_ref, refs = refs[0], refs[1:]
    st_out_ref = None
    if emit_state:
        st_out_ref, refs = refs[0], refs[1:]
    a16_scr, o_scr, u_scr, st_scr = refs

    c_len = RET_CHUNK
    n_chunks = seq_len // c_len
    head = pl.program_id(1)
    lg_f = _log_sigmoid(jnp.full((1, HEAD_DIM), logit_ref[layer, 0, head], F32))
    lg_b = _log_sigmoid(jnp.full((1, HEAD_DIM), logit_ref[layer, 1, head], F32))

    t = lax.broadcasted_iota(jnp.int32, (c_len, HEAD_DIM), 0).astype(F32)
    q_dec_f = jnp.exp((t + 1.0) * lg_f)
    q_dec_b = jnp.exp((c_len - t) * lg_b)
    k_dec = jnp.concatenate([jnp.exp((c_len - 1.0 - t) * lg_f), jnp.exp(t * lg_b)], axis=1)
    chunk_dec = jnp.concatenate([jnp.exp(c_len * lg_f), jnp.exp(c_len * lg_b)], axis=1)
    t_idx = lax.broadcasted_iota(jnp.int32, (c_len, c_len), 0)
    s_idx = lax.broadcasted_iota(jnp.int32, (c_len, c_len), 1)
    dist = (t_idx - s_idx).astype(F32)
    decay = jnp.where(t_idx >= s_idx, jnp.exp(jnp.maximum(dist, 0.0) * lg_f), 0.0)
    decay = decay + jnp.where(t_idx <= s_idx, jnp.exp(jnp.maximum(-dist, 0.0) * lg_b), 0.0)

    def prepare(c, carry):
        rows = _rows(c * c_len, c_len)
        k16 = k_ref[rows, :]
        a16_scr[rows, :] = (_dot_nt(q_ref[rows, :], k16) * decay).astype(BF16)
        k = k16.astype(F32)
        u_scr[c] = _state_increment(v_ref[rows, :], (jnp.concatenate([k, k], axis=1) * k_dec).astype(BF16))
        return carry

    lax.fori_loop(0, n_chunks, prepare, 0, unroll=min(RET_CHUNKS_PER_TRIP, n_chunks))

    _scan_states(n_chunks, n_seqs, lambda c: chunk_dec, u_scr, st_scr, s0_ref, earlier_ref, st_out_ref)

    blk_len = min(RET_BLOCK, seq_len)
    chunks_per_blk = blk_len // c_len

    def combine(blk, slot):
        for j in range(chunks_per_blk):
            c = blk * chunks_per_blk + j
            rows = _rows(c * c_len, c_len)
            q16 = q_ref[rows, :]
            o = _dot(a16_scr[rows, :], v_ref[rows, :])
            o = o + q_dec_f * _dot_nt(q16, st_scr[c, :, :HEAD_DIM]) + q_dec_b * _dot_nt(q16, st_scr[c, :, HEAD_DIM:])
            o_scr[slot, j * c_len : (j + 1) * c_len, :] = o

    gn = gn_ref[...]

    def head_norm_gate(blk, slot):
        rows = _rows(blk * blk_len, blk_len)
        o = o_scr[slot]
        o = o - jnp.mean(o, axis=-1, keepdims=True)
        o = o * lax.rsqrt(jnp.mean(o * o, axis=-1, keepdims=True) + EPS) * gn
        mix_ref[rows, :] = (o * gate_ref[rows, :]).astype(BF16)

    _skewed_stages(seq_len // blk_len, combine, head_norm_gate, steps_per_trip=BLOCKS_PER_TRIP)


def _ret(p32, p16, decay_logit, norm_gain, state, layer, emit_state, earlier, n_seqs):
    p32, p16 = _fold_sequences(p32, n_seqs), _fold_sequences(p16, n_seqs)
    _, batch, seq_len, _ = p32.shape
    n_chunks = seq_len // RET_CHUNK
    has_s0 = state is not None
    in_specs = [pl.BlockSpec(memory_space=pltpu.SMEM)]
    in_specs += [_slab_spec(seq_len, s) for s in (P16_QR, P16_KR, P16_VR, P32_GATE_R)]
    in_specs.append(pl.BlockSpec((None, 1, HEAD_DIM), lambda b, h: (layer, 0, h)))
    args = [decay_logit, p16, p16, p16, p32, norm_gain]
    if has_s0:
        in_specs.append(_state_in_spec(layer, n_seqs))
        args.append(state)
    mix_spec, mix_shape = _mix_out(batch, seq_len)
    out_specs, out_shape = [mix_spec], [mix_shape]
    if emit_state:
        _state_outputs(earlier, n_seqs, batch, in_specs, args, out_specs, out_shape)
    outs = pl.pallas_call(
        functools.partial(
            _ret_kernel, layer=layer, seq_len=seq_len, n_seqs=n_seqs, has_s0=has_s0,
            has_earlier=emit_state and earlier is not None, emit_state=emit_state,
        ),
        grid=(batch, N_HEADS),
        in_specs=in_specs,
        out_specs=out_specs,
        out_shape=out_shape,
        scratch_shapes=[
            pltpu.VMEM((seq_len, RET_CHUNK), BF16),
            pltpu.VMEM((2, min(RET_BLOCK, seq_len), HEAD_DIM), F32),
            pltpu.VMEM((n_chunks, HEAD_DIM, 2 * HEAD_DIM), F32),
            pltpu.VMEM((n_chunks, HEAD_DIM, 2 * HEAD_DIM), BF16),
        ],
        compiler_params=pltpu.CompilerParams(vmem_limit_bytes=VMEM_LIMIT),
        name="ret_scan",
    )(*args)
    return (outs[0], outs[1]) if emit_state else (outs[0], None)


def _out_ffn_kernel(*refs, d_ff, final):
    ma_ref, mb_ref, x_ref, mod_ref, gain_ref, wo_ref, wu_ref, wd_ref = refs[:8]
    refs = refs[8:]
    if final:
        gfin_ref, refs = refs[0], refs[1:]
    out_ref, act_scr = refs

    mod = mod_ref[...]
    mix = jnp.concatenate([ma_ref[hd] for hd in range(N_HEADS)] + [mb_ref[hd] for hd in range(N_HEADS)], axis=1)
    x1 = x_ref[...] + mod[2:3] * _dot(mix, wo_ref[...])
    h = _modnorm(x1, gain_ref[...], mod[4:5], mod[3:4]).astype(BF16)
    for c in range(d_ff // FFN_CHUNK):
        lo = c * FFN_CHUNK
        gate = _dot(h, wu_ref[:, lo : lo + FFN_CHUNK])
        up = _dot(h, wu_ref[:, d_ff + lo : d_ff + lo + FFN_CHUNK])
        act_scr[:, lo : lo + FFN_CHUNK] = (_silu(gate) * up).astype(BF16)
    x2 = x1 + mod[5:6] * _dot(act_scr[...], wd_ref[...])
    if final:
        x2 = x2 * lax.rsqrt(jnp.mean(x2 * x2, axis=-1, keepdims=True) + EPS) * gfin_ref[...]
    out_ref[...] = x2


def _out_ffn(mix_a, mix_b, x, mod, gain, w_out, w_up, w_down, final_gain, layer, tile_m, cond_index):
    n_tok, d_model = x.shape
    d_ff = w_down.shape[1]
    final = final_gain is not None
    const = lambda i: (layer, 0, 0)
    in_specs = [
        pl.BlockSpec((N_HEADS, tile_m, HEAD_DIM), lambda i: (0, i, 0)),
        pl.BlockSpec((N_HEADS, tile_m, HEAD_DIM), lambda i: (0, i, 0)),
        pl.BlockSpec((tile_m, d_model), lambda i: (i, 0)),
        pl.BlockSpec((None, None, N_MOD, d_model), lambda i: (layer, cond_index(i), 0, 0)),
        pl.BlockSpec((None, 1, d_model), const),
        pl.BlockSpec((None, 2 * N_HEADS * HEAD_DIM, d_model), const, pipeline_mode=pl.Buffered(1)),
        pl.BlockSpec((None, d_model, 2 * d_ff), const, pipeline_mode=pl.Buffered(1)),
        pl.BlockSpec((None, d_ff, d_model), const, pipeline_mode=pl.Buffered(1)),
    ]
    args = [mix_a, mix_b, x, mod, gain, w_out, w_up, w_down]
    if final:
        in_specs.append(pl.BlockSpec((1, d_model), lambda i: (0, 0)))
        args.append(final_gain)
    return pl.pallas_call(
        functools.partial(_out_ffn_kernel, d_ff=d_ff, final=final),
        grid=(n_tok // tile_m,),
        in_specs=in_specs,
        out_specs=pl.BlockSpec((tile_m, d_model), lambda i: (i, 0)),
        out_shape=jax.ShapeDtypeStruct((n_tok, d_model), F32),
        scratch_shapes=[pltpu.VMEM((tile_m, d_ff), BF16)],
        compiler_params=pltpu.CompilerParams(vmem_limit_bytes=VMEM_LIMIT),
        name="out_ffn",
    )(*args)


def kernel(x_prompt, x_sample, state_hgrn, state_ret, c, c_ctx, ada_w, ada_b, norm_mix, norm_ffn, w_in,
           hgrn_lb_logits, hgrn_norm, ret_decay_logit, ret_norm, w_out, w_up, w_down, norm_final):
    batch, seq, d_model = x_prompt.shape
    dec_batch, dec_seq, _ = x_sample.shape
    depth = ada_w.shape[0]
    tile_m = DENSE_TILE_M
    assert w_in.shape[2] == D_IN and state_hgrn.shape[3] == N_HEADS and state_ret.shape[3] == N_HEADS
    assert dec_batch < COND_ROWS
    for rows in (tile_m, FFN_TILE_M):
        assert dec_seq % rows == 0 and (batch * seq) % rows == 0
    assert dec_seq % GRID_W == 0
    for n_l in (seq, dec_seq):
        assert n_l % min(HGRN_BLOCK, n_l) == 0 and min(HGRN_BLOCK, n_l) % HGRN_CHUNK == 0
        assert n_l % min(RET_BLOCK, n_l) == 0 and n_l % RET_CHUNK == 0

    cond = jnp.concatenate(
        [c.astype(F32), c_ctx.astype(F32)[None, :], jnp.zeros((COND_ROWS - dec_batch - 1, d_model), F32)], axis=0
    )
    mod = _modulation(cond, ada_w, ada_b)

    w_in16, w_out16, w_up16, w_down16 = (w.astype(BF16) for w in (w_in, w_out, w_up, w_down))
    gain_mix = norm_mix.reshape(depth, 1, d_model)
    gain_ffn = norm_ffn.reshape(depth, 1, d_model)
    gain_hgrn = hgrn_norm.reshape(depth, 1, -1)
    gain_ret = ret_norm.reshape(depth, 1, -1)
    gain_final = norm_final.reshape(1, d_model)
    rope_tables = _rope_tables(dec_seq)

    tiles_per_seq = dec_seq // tile_m
    groups = [
        [x_prompt.astype(F32).reshape(batch * seq, d_model), batch, seq, lambda rows: (lambda i: dec_batch),
         None, None, None, True],
        [x_sample.astype(F32).reshape(dec_batch * dec_seq, d_model), dec_batch, dec_seq,
         lambda rows: (lambda i: i // (dec_seq // rows)), state_hgrn, state_ret, rope_tables, False],
    ]
    new_h = new_r = None
    for layer in range(depth):
        final_gain = gain_final if layer == depth - 1 else None
        for grp in groups:
            x, n_b, n_l, cond_of, s0_h, s0_r, tables, emit = grp
            p32, p16 = _in_proj(
                x, mod, gain_mix, w_in16, hgrn_lb_logits, tables, layer, tile_m, cond_of(tile_m), tiles_per_seq
            )
            p32 = p32.reshape(N_P32, n_b, n_l, HEAD_DIM)
            p16 = p16.reshape(N_P16, n_b, n_l, HEAD_DIM)
            n_seqs = max(1, min(n_b, SCAN_ROWS_PER_STEP // n_l, SCAN_SEQS_PER_STEP))
            mix_a, st_a = _hgrn(p32, p16, gain_hgrn, s0_h, layer, emit, new_h, n_seqs)
            mix_b, st_b = _ret(p32, p16, ret_decay_logit, gain_ret, s0_r, layer, emit, new_r, n_seqs)
            grp[0] = _out_ffn(
                mix_a.reshape(N_HEADS, n_b * n_l, HEAD_DIM), mix_b.reshape(N_HEADS, n_b * n_l, HEAD_DIM), x, mod,
                gain_ffn, w_out16, w_up16, w_down16, final_gain, layer, FFN_TILE_M, cond_of(FFN_TILE_M),
            )
            if emit:
                new_h, new_r = st_a, st_b
    y_prompt = groups[0][0].reshape(batch, seq, d_model).astype(x_prompt.dtype)
    y_sample = groups[1][0].reshape(dec_batch, dec_seq, d_model).astype(x_sample.dtype)
    new_state_hgrn = new_h.astype(state_hgrn.dtype)
    new_state_ret = new_r.astype(state_ret.dtype)
    return (y_prompt, y_sample, new_state_hgrn, new_state_ret)
```

```python
import functools

import jax
import jax.numpy as jnp
import numpy as np
from jax import lax
from jax.experimental import pallas as pl
from jax.experimental.pallas import tpu as pltpu

F32 = jnp.float32
BF16 = jnp.bfloat16

N_HEADS = 4
HEAD_DIM = 128
N_SECTIONS = 9
SECTION = N_HEADS * HEAD_DIM
D_IN = N_SECTIONS * SECTION
GRID_W = 64
ROPE_BASE = 10000.0
EPS = 1e-6
GATE_FLOOR = 1e-12

P32_QA, P32_KF, P32_KB, P32_GATE_A, P32_GATE_R = range(5)
P16_GF_HI, P16_GF_LO, P16_GB_HI, P16_GB_LO, P16_VA, P16_QR, P16_KR, P16_VR = range(8)
N_P32 = 5 * N_HEADS
N_P16 = 8 * N_HEADS

N_MOD = 6
COND_ROWS = 16

HGRN_CHUNK = 64
HGRN_BLOCK = 256
RET_CHUNK = 128
RET_BLOCK = 256
SCAN_ROWS_PER_STEP = 2048
BLOCKS_PER_TRIP = 8
RET_CHUNKS_PER_TRIP = 16
STATE_STEPS_INLINE = 32
STATE_STEPS_PER_TRIP = 2
SAFE_EXP_RANGE = 80.0
SAFE_Q_MAX = 1e3
FFN_CHUNK = 256
MOD_TILE_N = 1536
DENSE_TILE_M = 512
FFN_TILE_M = 1024
V7X_VMEM_BYTES = 64 * 1024 * 1024
VMEM_LIMIT = V7X_VMEM_BYTES - 8 * 1024 * 1024


def _silu(x):
    h = 0.5 * x
    return h + h * jnp.tanh(h)


def _dot(a, b):
    return jnp.dot(a, b, preferred_element_type=F32)


def _dot_nt(a, b):
    return lax.dot_general(a, b, (((1,), (1,)), ((), ())), preferred_element_type=F32)


def _split_bf16(x):
    hi = x.astype(BF16)
    return hi, (x - hi.astype(F32)).astype(BF16)


def _modnorm(x, gain, scale, shift):
    ms = jnp.mean(x * x, axis=-1, keepdims=True)
    return x * lax.rsqrt(ms + EPS) * gain * (1.0 + scale) + shift


def _mod_kernel(cond_ref, w_ref, b_ref, out_ref):
    s_hi, s_lo = _split_bf16(_silu(cond_ref[...]))
    w_hi, w_lo = _split_bf16(w_ref[...])
    s_both = jnp.concatenate([s_hi, s_lo], axis=0)
    p = _dot(s_both, w_hi) + _dot(s_both, w_lo)
    out_ref[...] = p[:COND_ROWS] + p[COND_ROWS:] + b_ref[...]


def _modulation(cond, ada_w, ada_b):
    depth, d_model, n_out = ada_w.shape
    out = pl.pallas_call(
        _mod_kernel,
        grid=(depth, n_out // MOD_TILE_N),
        in_specs=[
            pl.BlockSpec((COND_ROWS, d_model), lambda l, j: (0, 0)),
            pl.BlockSpec((None, d_model, MOD_TILE_N), lambda l, j: (l, 0, j)),
            pl.BlockSpec((None, 1, MOD_TILE_N), lambda l, j: (l, 0, j)),
        ],
        out_specs=pl.BlockSpec((None, COND_ROWS, MOD_TILE_N), lambda l, j: (l, 0, j)),
        out_shape=jax.ShapeDtypeStruct((depth, COND_ROWS, n_out), F32),
        compiler_params=pltpu.CompilerParams(vmem_limit_bytes=VMEM_LIMIT),
        name="adaln_mod",
    )(cond, ada_w, ada_b.reshape(depth, 1, n_out))
    return out.reshape(depth, COND_ROWS, N_MOD, d_model)


def _rope(x, cos, sin_signed):
    lane = lax.broadcasted_iota(jnp.int32, x.shape, 1)
    quarter = HEAD_DIM // 4
    partner = jnp.where(lane % (2 * quarter) < quarter, pltpu.roll(x, HEAD_DIM - quarter, 1), pltpu.roll(x, quarter, 1))
    return x * cos + partner * sin_signed


def _hgrn_lower_bounds(logits, layer):
    e = jnp.exp(logits - jnp.max(logits, axis=0, keepdims=True))
    p = e / jnp.sum(e, axis=0, keepdims=True)
    cum = p[0]
    for i in range(1, layer + 1):
        cum = cum + p[i]
    return jnp.clip(cum - p[0], 0.0, 1.0)


def _in_proj_kernel(*refs, layer, use_rope):
    x_ref, mod_ref, gain_ref, w_ref, lbl_ref = refs[:5]
    refs = refs[5:]
    if use_rope:
        cos_ref, sin_ref, refs = refs[0], refs[1], refs[2:]
    p32_ref, p16_ref = refs

    mod = mod_ref[...]
    h = _modnorm(x_ref[...], gain_ref[...], mod[1:2], mod[0:1]).astype(BF16)
    lb = _hgrn_lower_bounds(lbl_ref[...], layer)

    def proj(j):
        return _dot(h, w_ref[:, j * SECTION : (j + 1) * SECTION])

    def put(ref, section, val):
        for hd in range(N_HEADS):
            ref[section * N_HEADS + hd] = val[:, hd * HEAD_DIM : (hd + 1) * HEAD_DIM].astype(ref.dtype)

    put(p32_ref, P32_QA, _silu(proj(0)))
    for d, (sec_hi, sec_lo, sec_k) in enumerate(
        ((P16_GF_HI, P16_GF_LO, P32_KF), (P16_GB_HI, P16_GB_LO, P32_KB))
    ):
        lb_d = lb[d : d + 1]
        c0, c1 = 0.5 + 0.5 * lb_d, 0.5 - 0.5 * lb_d
        ct = c1 * jnp.tanh(0.5 * proj(1 + d))
        g_hi, g_lo = _split_bf16(jnp.log(jnp.maximum(c0 + ct, GATE_FLOOR)))
        put(p16_ref, sec_hi, g_hi)
        put(p16_ref, sec_lo, g_lo)
        put(p32_ref, sec_k, c1 - ct)
    put(p16_ref, P16_VA, proj(3))
    put(p32_ref, P32_GATE_A, _silu(proj(4)))
    q = proj(5)
    k = proj(6) * (HEAD_DIM ** -0.5)
    if use_rope:
        cos, sin = cos_ref[...], sin_ref[...]
        heads = [slice(hd * HEAD_DIM, (hd + 1) * HEAD_DIM) for hd in range(N_HEADS)]
        q = jnp.concatenate([_rope(q[:, sl], cos, sin) for sl in heads], axis=1)
        k = jnp.concatenate([_rope(k[:, sl], cos, sin) for sl in heads], axis=1)
    put(p16_ref, P16_QR, q)
    put(p16_ref, P16_KR, k)
    put(p16_ref, P16_VR, proj(7))
    put(p32_ref, P32_GATE_R, _silu(proj(8)))


def _in_proj(x, mod, gain, w, lb_logits, rope_tables, layer, tile_m, cond_index, tiles_per_seq):
    n_tok, d_model = x.shape
    depth = lb_logits.shape[0]
    use_rope = rope_tables is not None
    in_specs = [
        pl.BlockSpec((tile_m, d_model), lambda i: (i, 0)),
        pl.BlockSpec((None, None, N_MOD, d_model), lambda i: (layer, cond_index(i), 0, 0)),
        pl.BlockSpec((None, 1, d_model), lambda i: (layer, 0, 0)),
        pl.BlockSpec((None, d_model, D_IN), lambda i: (layer, 0, 0), pipeline_mode=pl.Buffered(1)),
        pl.BlockSpec((depth, 2, SECTION), lambda i: (0, 0, 0)),
    ]
    args = [x, mod, gain, w, lb_logits]
    if use_rope:
        in_specs += [pl.BlockSpec((tile_m, HEAD_DIM), lambda i: (i % tiles_per_seq, 0))] * 2
        args += list(rope_tables)
    return pl.pallas_call(
        functools.partial(_in_proj_kernel, layer=layer, use_rope=use_rope),
        grid=(n_tok // tile_m,),
        in_specs=in_specs,
        out_specs=[
            pl.BlockSpec((N_P32, tile_m, HEAD_DIM), lambda i: (0, i, 0)),
            pl.BlockSpec((N_P16, tile_m, HEAD_DIM), lambda i: (0, i, 0)),
        ],
        out_shape=[
            jax.ShapeDtypeStruct((N_P32, n_tok, HEAD_DIM), F32),
            jax.ShapeDtypeStruct((N_P16, n_tok, HEAD_DIM), BF16),
        ],
        compiler_params=pltpu.CompilerParams(vmem_limit_bytes=VMEM_LIMIT),
        name="in_proj",
    )(*args)


def _rope_tables(seq_len):
    quarter = HEAD_DIM // 4
    inv = np.float32(ROPE_BASE) ** (-np.arange(quarter, dtype=np.float32) / np.float32(quarter))
    pos = np.arange(seq_len, dtype=np.int32)
    ang_r = (pos // GRID_W).astype(np.float32)[:, None] * inv
    ang_c = (pos % GRID_W).astype(np.float32)[:, None] * inv
    cos = np.concatenate([np.cos(ang_r), np.cos(ang_r), np.cos(ang_c), np.cos(ang_c)], axis=1)
    sin = np.concatenate([-np.sin(ang_r), np.sin(ang_r), -np.sin(ang_c), np.sin(ang_c)], axis=1)
    return jnp.asarray(cos, F32), jnp.asarray(sin, F32)


def _rows(start, size):
    return pl.ds(start if isinstance(start, int) else pl.multiple_of(start, size), size)


def _skewed_stages(n_steps, produce, consume, steps_per_trip):
    steps_per_trip = min(steps_per_trip, n_steps)
    assert steps_per_trip % 2 == 0 and n_steps % steps_per_trip == 0
    produce(0, 0)
    if steps_per_trip == n_steps:
        for j in range(n_steps):
            if j + 1 < n_steps:
                produce(j + 1, (j + 1) % 2)
            consume(j, j % 2)
        return

    def body(trip, carry):
        first = steps_per_trip * trip
        for j in range(steps_per_trip):
            produce(jnp.minimum(first + j + 1, n_steps - 1), (j + 1) % 2)
            consume(first + j, j % 2)
        return carry

    lax.fori_loop(0, n_steps // steps_per_trip, body, 0)


def _state_increment(v16, k_cat):
    return _dot(v16.astype(F32).T.astype(BF16), k_cat)


def _run_states(n_chunks, first, dec_of, u_scr, st_scr, s_f, s_b):
    def step(i, carry):
        s_f, s_b = carry
        cf, cb = first + i, first + n_chunks - 1 - i
        st_scr[cf, :, :HEAD_DIM] = s_f.astype(BF16)
        st_scr[cb, :, HEAD_DIM:] = s_b.astype(BF16)
        s_f = dec_of(cf)[:, :HEAD_DIM] * s_f + u_scr[cf, :, :HEAD_DIM]
        s_b = dec_of(cb)[:, HEAD_DIM:] * s_b + u_scr[cb, :, HEAD_DIM:]
        return s_f, s_b

    if n_chunks <= STATE_STEPS_INLINE:
        carry = (s_f, s_b)
        for i in range(n_chunks):
            carry = step(i, carry)
        return carry
    return lax.fori_loop(0, n_chunks, step, (s_f, s_b), unroll=STATE_STEPS_PER_TRIP)


def _scan_states(n_chunks, n_seqs, dec_of, u_scr, st_scr, s0_ref, earlier_ref, st_out_ref):
    per_seq = n_chunks // n_seqs
    n_earlier = 0
    if earlier_ref is not None:
        n_earlier = earlier_ref.shape[1]
        st_out_ref[:, :n_earlier] = earlier_ref[...]
    for s in range(n_seqs):
        if s0_ref is None:
            s_f = s_b = jnp.zeros((HEAD_DIM, HEAD_DIM), F32)
        else:
            s_f, s_b = s0_ref[s, 0].T, s0_ref[s, 1].T
        s_f, s_b = _run_states(per_seq, s * per_seq, dec_of, u_scr, st_scr, s_f, s_b)
        if st_out_ref is not None:
            st_out_ref[s, n_earlier, 0] = s_f.T
            st_out_ref[s, n_earlier, 1] = s_b.T


def _slab_spec(seq_len, section):
    return pl.BlockSpec((None, None, seq_len, HEAD_DIM), lambda b, h: (section * N_HEADS + h, b, 0, 0))


def _state_in_spec(layer, n_seqs):
    return pl.BlockSpec((n_seqs, None, 2, None, HEAD_DIM, HEAD_DIM), lambda b, h: (b, layer, 0, h, 0, 0))


def _stacked_state_spec(n_seqs, n_layers):
    return pl.BlockSpec((n_seqs, n_layers, 2, None, HEAD_DIM, HEAD_DIM), lambda b, h: (b, 0, 0, h, 0, 0))


def _state_outputs(earlier, n_seqs, batch, in_specs, args, out_specs, out_shape):
    n_layers = 1
    if earlier is not None:
        n_layers += earlier.shape[1]
        in_specs.append(_stacked_state_spec(n_seqs, n_layers - 1))
        args.append(earlier)
    out_specs.append(_stacked_state_spec(n_seqs, n_layers))
    out_shape.append(jax.ShapeDtypeStruct((batch * n_seqs, n_layers, 2, N_HEADS, HEAD_DIM, HEAD_DIM), F32))


def _fold_sequences(slabs, n_seqs):
    n, batch, seq_len, d = slabs.shape
    assert batch % n_seqs == 0
    return slabs.reshape(n, batch // n_seqs, n_seqs * seq_len, d)


def _mix_out(batch, seq_len):
    spec = pl.BlockSpec((None, None, seq_len, HEAD_DIM), lambda b, h: (h, b, 0, 0))
    return spec, jax.ShapeDtypeStruct((N_HEADS, batch, seq_len, HEAD_DIM), BF16)


def _cumsum_rows(g, reverse):
    n = g.shape[0]
    pos = lax.broadcasted_iota(jnp.int32, g.shape, 0)
    sh = 1
    while sh < n:
        if reverse:
            g = g + jnp.where(pos < n - sh, pltpu.roll(g, n - sh, 0), 0.0)
        else:
            g = g + jnp.where(pos >= sh, pltpu.roll(g, sh, 0), 0.0)
        sh *= 2
    return g


def _causal_mask(n, lower):
    t_idx = lax.broadcasted_iota(jnp.int32, (n, n), 0)
    s_idx = lax.broadcasted_iota(jnp.int32, (n, n), 1)
    return t_idx >= s_idx if lower else t_idx <= s_idx


def _hgrn_direct_scores(q, k, b, k_scr, b_scr):
    n = q.shape[0]
    k_scr[...] = k
    b_scr[...] = b
    col = lax.broadcasted_iota(jnp.int32, (n, n), 1)

    def body(s, acc):
        z = q * k_scr[pl.ds(s, 1), :] * jnp.exp(jnp.minimum(b - b_scr[pl.ds(s, 1), :], 0.0))
        return jnp.where(col == s, jnp.sum(z, axis=1, keepdims=True), acc)

    return lax.fori_loop(0, n, body, jnp.zeros((n, n), F32))


def _hgrn_kernel(*refs, seq_len, n_seqs, has_s0, has_earlier, emit_state):
    qa_ref, kf_ref, kb_ref, gate_ref, gf_hi_ref, gf_lo_ref, gb_hi_ref, gb_lo_ref, va_ref, gn_ref = refs[:10]
    refs = refs[10:]
    s0_ref = earlier_ref = None
    if has_s0:
        s0_ref, refs = refs[0], refs[1:]
    if has_earlier:
        earlier_ref, refs = refs[0], refs[1:]
    mix_ref, refs = refs[0], refs[1:]
    st_out_ref = None
    if emit_state:
        st_out_ref, refs = refs[0], refs[1:]
    tri_scr, cum_scr, a16_scr, qe_scr, o_scr, u_scr, dec_scr, st_scr, safe_smem, k_scr, b_scr = refs

    c_len = HGRN_CHUNK
    blk_len = min(HGRN_BLOCK, seq_len)
    n_chunks = seq_len // c_len
    n_blocks = seq_len // blk_len
    chunks_per_blk = blk_len // c_len
    directions = ((kf_ref, gf_hi_ref, gf_lo_ref, False), (kb_ref, gb_hi_ref, gb_lo_ref, True))
    causal = (_causal_mask(c_len, True), _causal_mask(c_len, False))

    @pl.when(jnp.logical_and(pl.program_id(0) == 0, pl.program_id(1) == 0))
    def _():
        t_idx = lax.broadcasted_iota(jnp.int32, (blk_len, blk_len), 0)
        s_idx = lax.broadcasted_iota(jnp.int32, (blk_len, blk_len), 1)
        same_chunk = (t_idx // c_len) == (s_idx // c_len)
        tri_scr[0] = jnp.where(jnp.logical_and(same_chunk, s_idx <= t_idx), 1.0, 0.0).astype(BF16)
        tri_scr[1] = jnp.where(jnp.logical_and(same_chunk, s_idx >= t_idx), 1.0, 0.0).astype(BF16)

    safe_smem[n_chunks] = 0

    def cumulate(blk, slot):
        rows = _rows(blk * blk_len, blk_len)
        for d, (_, g_hi_ref, g_lo_ref, _) in enumerate(directions):
            cum = _dot(tri_scr[d], jnp.concatenate([g_hi_ref[rows, :], g_lo_ref[rows, :]], axis=1))
            cum_scr[slot, d] = cum[:, :HEAD_DIM] + cum[:, HEAD_DIM:]

    def prepare(blk, slot):
        r0 = blk * blk_len
        rows = _rows(r0, blk_len)
        q = qa_ref[rows, :]
        v16 = va_ref[rows, :]
        q_ok = jnp.max(jnp.abs(q)) <= SAFE_Q_MAX
        gates = [(k_ref[rows, :], cum_scr[slot, d]) for d, (k_ref, _, _, _) in enumerate(directions)]
        for j in range(chunks_per_blk):
            sl = slice(j * c_len, (j + 1) * c_len)
            rows_j = _rows(r0 + j * c_len, c_len)
            c = blk * chunks_per_blk + j
            q_e, k_e, dec, span, scores = [], [], [], None, None
            for d, (k, b) in enumerate(gates):
                k_j, b_j = k[sl], b[sl]
                tot = b_j[0:1] if directions[d][3] else b_j[c_len - 1 : c_len]
                dist = b_j - b_j[c_len // 2 - 1 : c_len // 2]
                a = _dot_nt((q[sl] * jnp.exp(dist)).astype(BF16), (k_j * jnp.exp(-dist)).astype(BF16))
                a = jnp.where(causal[d], a, 0.0)
                scores = a if scores is None else scores + a
                q_e.append(q[sl] * jnp.exp(b_j))
                k_e.append(k_j * jnp.exp(tot - b_j))
                dec.append(jnp.exp(tot))
                ends = jnp.maximum(jnp.abs(dist[0:1]), jnp.abs(dist[c_len - 1 : c_len]))
                span = ends if span is None else jnp.maximum(span, ends)
            a16_scr[rows_j, :] = scores.astype(BF16)
            qe_scr[rows_j, :] = jnp.concatenate(q_e, axis=1).astype(BF16)
            u_scr[c] = _state_increment(v16[sl], jnp.concatenate(k_e, axis=1).astype(BF16))
            dec_scr[c] = jnp.concatenate(dec, axis=1)
            safe = jnp.logical_and(q_ok, jnp.max(span) <= SAFE_EXP_RANGE).astype(jnp.int32)
            safe_smem[c] = safe
            safe_smem[n_chunks] = safe_smem[n_chunks] + (1 - safe)

    _skewed_stages(n_blocks, cumulate, prepare, steps_per_trip=BLOCKS_PER_TRIP)

    _scan_states(n_chunks, n_seqs, lambda c: dec_scr[c], u_scr, st_scr, s0_ref, earlier_ref, st_out_ref)

    def redo_scores(c, carry):
        @pl.when(safe_smem[c] == 0)
        def _():
            rows = _rows(c * c_len, c_len)
            q = qa_ref[rows, :]
            scores = jnp.zeros((c_len, c_len), F32)
            for d, (k_ref, g_hi_ref, g_lo_ref, rev) in enumerate(directions):
                b = _cumsum_rows(g_hi_ref[rows, :].astype(F32) + g_lo_ref[rows, :].astype(F32), rev)
                scores = scores + jnp.where(causal[d], _hgrn_direct_scores(q, k_ref[rows, :], b, k_scr, b_scr), 0.0)
            a16_scr[rows, :] = scores.astype(BF16)

        return carry

    @pl.when(safe_smem[n_chunks] > 0)
    def _():
        lax.fori_loop(0, n_chunks, redo_scores, 0)

    def combine(blk, slot):
        for j in range(chunks_per_blk):
            c = blk * chunks_per_blk + j
            rows = _rows(c * c_len, c_len)
            o = _dot(a16_scr[rows, :], va_ref[rows, :]) + _dot_nt(qe_scr[rows, :], st_scr[c])
            o_scr[slot, j * c_len : (j + 1) * c_len, :] = o

    gn = gn_ref[...]

    def head_norm_gate(blk, slot):
        rows = _rows(blk * blk_len, blk_len)
        o = o_scr[slot]
        o = o * lax.rsqrt(jnp.mean(o * o, axis=-1, keepdims=True) + EPS) * gn
        mix_ref[rows, :] = (o * gate_ref[rows, :]).astype(BF16)

    _skewed_stages(n_blocks, combine, head_norm_gate, steps_per_trip=BLOCKS_PER_TRIP)


def _hgrn(p32, p16, norm_gain, state, layer, emit_state, earlier, n_seqs):
    p32, p16 = _fold_sequences(p32, n_seqs), _fold_sequences(p16, n_seqs)
    _, batch, seq_len, _ = p32.shape
    n_chunks = seq_len // HGRN_CHUNK
    blk_len = min(HGRN_BLOCK, seq_len)
    has_s0 = state is not None
    f32_slabs = (P32_QA, P32_KF, P32_KB, P32_GATE_A)
    bf16_slabs = (P16_GF_HI, P16_GF_LO, P16_GB_HI, P16_GB_LO, P16_VA)
    in_specs = [_slab_spec(seq_len, s) for s in f32_slabs + bf16_slabs]
    in_specs.append(pl.BlockSpec((None, 1, HEAD_DIM), lambda b, h: (layer, 0, h)))
    args = [p32] * len(f32_slabs) + [p16] * len(bf16_slabs) + [norm_gain]
    if has_s0:
        in_specs.append(_state_in_spec(layer, n_seqs))
        args.append(state)
    mix_spec, mix_shape = _mix_out(batch, seq_len)
    out_specs, out_shape = [mix_spec], [mix_shape]
    if emit_state:
        _state_outputs(earlier, n_seqs, batch, in_specs, args, out_specs, out_shape)
    outs = pl.pallas_call(
        functools.partial(
            _hgrn_kernel, seq_len=seq_len, n_seqs=n_seqs, has_s0=has_s0,
            has_earlier=emit_state and earlier is not None, emit_state=emit_state,
        ),
        grid=(batch, N_HEADS),
        in_specs=in_specs,
        out_specs=out_specs,
        out_shape=out_shape,
        scratch_shapes=[
            pltpu.VMEM((2, blk_len, blk_len), BF16),
            pltpu.VMEM((2, 2, blk_len, HEAD_DIM), F32),
            pltpu.VMEM((seq_len, HGRN_CHUNK), BF16),
            pltpu.VMEM((seq_len, 2 * HEAD_DIM), BF16),
            pltpu.VMEM((2, blk_len, HEAD_DIM), F32),
            pltpu.VMEM((n_chunks, HEAD_DIM, 2 * HEAD_DIM), F32),
            pltpu.VMEM((n_chunks, 1, 2 * HEAD_DIM), F32),
            pltpu.VMEM((n_chunks, HEAD_DIM, 2 * HEAD_DIM), BF16),
            pltpu.SMEM((n_chunks + 1,), jnp.int32),
            pltpu.VMEM((HGRN_CHUNK, HEAD_DIM), F32),
            pltpu.VMEM((HGRN_CHUNK, HEAD_DIM), F32),
        ],
        compiler_params=pltpu.CompilerParams(vmem_limit_bytes=VMEM_LIMIT),
        name="hgrn_scan",
    )(*args)
    return (outs[0], outs[1]) if emit_state else (outs[0], None)


def _log_sigmoid(x):
    return jnp.minimum(x, 0.0) - jnp.log1p(jnp.exp(-jnp.abs(x)))


def _ret_kernel(*refs, layer, seq_len, n_seqs, has_s0, has_earlier, emit_state):
    logit_ref, q_ref, k_ref, v_ref, gate_ref, gn_ref = refs[:6]
    refs = refs[6:]
    s0_ref = earlier_ref = None
    if has_s0:
        s0_ref, refs = refs[0], refs[1:]
    if has_earlier:
        earlier_ref, refs = refs[0], refs[1:]
    mix_ref, refs = refs[0], refs[1:]
    st_out_ref = None
    if emit_state:
        st_out_ref, refs = refs[0], refs[1:]
    a16_scr, o_scr, u_scr, st_scr = refs

    c_len = RET_CHUNK
    n_chunks = seq_len // c_len
    head = pl.program_id(1)
    lg_f = _log_sigmoid(jnp.full((1, HEAD_DIM), logit_ref[layer, 0, head], F32))
    lg_b = _log_sigmoid(jnp.full((1, HEAD_DIM), logit_ref[layer, 1, head], F32))

    t = lax.broadcasted_iota(jnp.int32, (c_len, HEAD_DIM), 0).astype(F32)
    q_dec_f = jnp.exp((t + 1.0) * lg_f)
    q_dec_b = jnp.exp((c_len - t) * lg_b)
    k_dec = jnp.concatenate([jnp.exp((c_len - 1.0 - t) * lg_f), jnp.exp(t * lg_b)], axis=1)
    chunk_dec = jnp.concatenate([jnp.exp(c_len * lg_f), jnp.exp(c_len * lg_b)], axis=1)
    t_idx = lax.broadcasted_iota(jnp.int32, (c_len, c_len), 0)
    s_idx = lax.broadcasted_iota(jnp.int32, (c_len, c_len), 1)
    dist = (t_idx - s_idx).astype(F32)
    decay = jnp.where(t_idx >= s_idx, jnp.exp(jnp.maximum(dist, 0.0) * lg_f), 0.0)
    decay = decay + jnp.where(t_idx <= s_idx, jnp.exp(jnp.maximum(-dist, 0.0) * lg_b), 0.0)

    def prepare(c, carry):
        rows = _rows(c * c_len, c_len)
        k16 = k_ref[rows, :]
        a16_scr[rows, :] = (_dot_nt(q_ref[rows, :], k16) * decay).astype(BF16)
        k = k16.astype(F32)
        u_scr[c] = _state_increment(v_ref[rows, :], (jnp.concatenate([k, k], axis=1) * k_dec).astype(BF16))
        return carry

    lax.fori_loop(0, n_chunks, prepare, 0, unroll=min(RET_CHUNKS_PER_TRIP, n_chunks))

    _scan_states(n_chunks, n_seqs, lambda c: chunk_dec, u_scr, st_scr, s0_ref, earlier_ref, st_out_ref)

    blk_len = min(RET_BLOCK, seq_len)
    chunks_per_blk = blk_len // c_len

    def combine(blk, slot):
        for j in range(chunks_per_blk):
            c = blk * chunks_per_blk + j
            rows = _rows(c * c_len, c_len)
            q16 = q_ref[rows, :]
            o = _dot(a16_scr[rows, :], v_ref[rows, :])
            o = o + q_dec_f * _dot_nt(q16, st_scr[c, :, :HEAD_DIM]) + q_dec_b * _dot_nt(q16, st_scr[c, :, HEAD_DIM:])
            o_scr[slot, j * c_len : (j + 1) * c_len, :] = o

    gn = gn_ref[...]

    def head_norm_gate(blk, slot):
        rows = _rows(blk * blk_len, blk_len)
        o = o_scr[slot]
        o = o - jnp.mean(o, axis=-1, keepdims=True)
        o = o * lax.rsqrt(jnp.mean(o * o, axis=-1, keepdims=True) + EPS) * gn
        mix_ref[rows, :] = (o * gate_ref[rows, :]).astype(BF16)

    _skewed_stages(seq_len // blk_len, combine, head_norm_gate, steps_per_trip=BLOCKS_PER_TRIP)


def _ret(p32, p16, decay_logit, norm_gain, state, layer, emit_state, earlier, n_seqs):
    p32, p16 = _fold_sequences(p32, n_seqs), _fold_sequences(p16, n_seqs)
    _, batch, seq_len, _ = p32.shape
    n_chunks = seq_len // RET_CHUNK
    has_s0 = state is not None
    in_specs = [pl.BlockSpec(memory_space=pltpu.SMEM)]
    in_specs += [_slab_spec(seq_len, s) for s in (P16_QR, P16_KR, P16_VR, P32_GATE_R)]
    in_specs.append(pl.BlockSpec((None, 1, HEAD_DIM), lambda b, h: (layer, 0, h)))
    args = [decay_logit, p16, p16, p16, p32, norm_gain]
    if has_s0:
        in_specs.append(_state_in_spec(layer, n_seqs))
        args.append(state)
    mix_spec, mix_shape = _mix_out(batch, seq_len)
    out_specs, out_shape = [mix_spec], [mix_shape]
    if emit_state:
        _state_outputs(earlier, n_seqs, batch, in_specs, args, out_specs, out_shape)
    outs = pl.pallas_call(
        functools.partial(
            _ret_kernel, layer=layer, seq_len=seq_len, n_seqs=n_seqs, has_s0=has_s0,
            has_earlier=emit_state and earlier is not None, emit_state=emit_state,
        ),
        grid=(batch, N_HEADS),
        in_specs=in_specs,
        out_specs=out_specs,
        out_shape=out_shape,
        scratch_shapes=[
            pltpu.VMEM((seq_len, RET_CHUNK), BF16),
            pltpu.VMEM((2, min(RET_BLOCK, seq_len), HEAD_DIM), F32),
            pltpu.VMEM((n_chunks, HEAD_DIM, 2 * HEAD_DIM), F32),
            pltpu.VMEM((n_chunks, HEAD_DIM, 2 * HEAD_DIM), BF16),
        ],
        compiler_params=pltpu.CompilerParams(vmem_limit_bytes=VMEM_LIMIT),
        name="ret_scan",
    )(*args)
    return (outs[0], outs[1]) if emit_state else (outs[0], None)


def _out_ffn_kernel(*refs, d_ff, final):
    ma_ref, mb_ref, x_ref, mod_ref, gain_ref, wo_ref, wu_ref, wd_ref = refs[:8]
    refs = refs[8:]
    if final:
        gfin_ref, refs = refs[0], refs[1:]
    out_ref, act_scr = refs

    mod = mod_ref[...]
    mix = jnp.concatenate([ma_ref[hd] for hd in range(N_HEADS)] + [mb_ref[hd] for hd in range(N_HEADS)], axis=1)
    x1 = x_ref[...] + mod[2:3] * _dot(mix, wo_ref[...])
    h = _modnorm(x1, gain_ref[...], mod[4:5], mod[3:4]).astype(BF16)
    for c in range(d_ff // FFN_CHUNK):
        lo = c * FFN_CHUNK
        gate = _dot(h, wu_ref[:, lo : lo + FFN_CHUNK])
        up = _dot(h, wu_ref[:, d_ff + lo : d_ff + lo + FFN_CHUNK])
        act_scr[:, lo : lo + FFN_CHUNK] = (_silu(gate) * up).astype(BF16)
    x2 = x1 + mod[5:6] * _dot(act_scr[...], wd_ref[...])
    if final:
        x2 = x2 * lax.rsqrt(jnp.mean(x2 * x2, axis=-1, keepdims=True) + EPS) * gfin_ref[...]
    out_ref[...] = x2


def _out_ffn(mix_a, mix_b, x, mod, gain, w_out, w_up, w_down, final_gain, layer, tile_m, cond_index):
    n_tok, d_model = x.shape
    d_ff = w_down.shape[1]
    final = final_gain is not None
    const = lambda i: (layer, 0, 0)
    in_specs = [
        pl.BlockSpec((N_HEADS, tile_m, HEAD_DIM), lambda i: (0, i, 0)),
        pl.BlockSpec((N_HEADS, tile_m, HEAD_DIM), lambda i: (0, i, 0)),
        pl.BlockSpec((tile_m, d_model), lambda i: (i, 0)),
        pl.BlockSpec((None, None, N_MOD, d_model), lambda i: (layer, cond_index(i), 0, 0)),
        pl.BlockSpec((None, 1, d_model), const),
        pl.BlockSpec((None, 2 * N_HEADS * HEAD_DIM, d_model), const, pipeline_mode=pl.Buffered(1)),
        pl.BlockSpec((None, d_model, 2 * d_ff), const, pipeline_mode=pl.Buffered(1)),
        pl.BlockSpec((None, d_ff, d_model), const, pipeline_mode=pl.Buffered(1)),
    ]
    args = [mix_a, mix_b, x, mod, gain, w_out, w_up, w_down]
    if final:
        in_specs.append(pl.BlockSpec((1, d_model), lambda i: (0, 0)))
        args.append(final_gain)
    return pl.pallas_call(
        functools.partial(_out_ffn_kernel, d_ff=d_ff, final=final),
        grid=(n_tok // tile_m,),
        in_specs=in_specs,
        out_specs=pl.BlockSpec((tile_m, d_model), lambda i: (i, 0)),
        out_shape=jax.ShapeDtypeStruct((n_tok, d_model), F32),
        scratch_shapes=[pltpu.VMEM((tile_m, d_ff), BF16)],
        compiler_params=pltpu.CompilerParams(vmem_limit_bytes=VMEM_LIMIT),
        name="out_ffn",
    )(*args)


def kernel(x_prompt, x_sample, state_hgrn, state_ret, c, c_ctx, ada_w, ada_b, norm_mix, norm_ffn, w_in,
           hgrn_lb_logits, hgrn_norm, ret_decay_logit, ret_norm, w_out, w_up, w_down, norm_final):
    batch, seq, d_model = x_prompt.shape
    dec_batch, dec_seq, _ = x_sample.shape
    depth = ada_w.shape[0]
    tile_m = DENSE_TILE_M
    assert w_in.shape[2] == D_IN and state_hgrn.shape[3] == N_HEADS and state_ret.shape[3] == N_HEADS
    assert dec_batch < COND_ROWS
    for rows in (tile_m, FFN_TILE_M):
        assert dec_seq % rows == 0 and (batch * seq) % rows == 0
    assert dec_seq % GRID_W == 0
    for n_l in (seq, dec_seq):
        assert n_l % min(HGRN_BLOCK, n_l) == 0 and min(HGRN_BLOCK, n_l) % HGRN_CHUNK == 0
        assert n_l % min(RET_BLOCK, n_l) == 0 and n_l % RET_CHUNK == 0

    cond = jnp.concatenate(
        [c.astype(F32), c_ctx.astype(F32)[None, :], jnp.zeros((COND_ROWS - dec_batch - 1, d_model), F32)], axis=0
    )
    mod = _modulation(cond, ada_w, ada_b)

    w_in16, w_out16, w_up16, w_down16 = (w.astype(BF16) for w in (w_in, w_out, w_up, w_down))
    gain_mix = norm_mix.reshape(depth, 1, d_model)
    gain_ffn = norm_ffn.reshape(depth, 1, d_model)
    gain_hgrn = hgrn_norm.reshape(depth, 1, -1)
    gain_ret = ret_norm.reshape(depth, 1, -1)
    gain_final = norm_final.reshape(1, d_model)
    rope_tables = _rope_tables(dec_seq)

    tiles_per_seq = dec_seq // tile_m
    groups = [
        [x_prompt.astype(F32).reshape(batch * seq, d_model), batch, seq, lambda rows: (lambda i: dec_batch),
         None, None, None, True],
        [x_sample.astype(F32).reshape(dec_batch * dec_seq, d_model), dec_batch, dec_seq,
         lambda rows: (lambda i: i // (dec_seq // rows)), state_hgrn, state_ret, rope_tables, False],
    ]
    new_h = new_r = None
    for layer in range(depth):
        final_gain = gain_final if layer == depth - 1 else None
        for grp in groups:
            x, n_b, n_l, cond_of, s0_h, s0_r, tables, emit = grp
            p32, p16 = _in_proj(
                x, mod, gain_mix, w_in16, hgrn_lb_logits, tables, layer, tile_m, cond_of(tile_m), tiles_per_seq
            )
            p32 = p32.reshape(N_P32, n_b, n_l, HEAD_DIM)
            p16 = p16.reshape(N_P16, n_b, n_l, HEAD_DIM)
            n_seqs = max(1, min(n_b, SCAN_ROWS_PER_STEP // n_l))
            mix_a, st_a = _hgrn(p32, p16, gain_hgrn, s0_h, layer, emit, new_h, n_seqs)
            mix_b, st_b = _ret(p32, p16, ret_decay_logit, gain_ret, s0_r, layer, emit, new_r, n_seqs)
            grp[0] = _out_ffn(
                mix_a.reshape(N_HEADS, n_b * n_l, HEAD_DIM), mix_b.reshape(N_HEADS, n_b * n_l, HEAD_DIM), x, mod,
                gain_ffn, w_out16, w_up16, w_down16, final_gain, layer, FFN_TILE_M, cond_of(FFN_TILE_M),
            )
            if emit:
                new_h, new_r = st_a, st_b
    y_prompt = groups[0][0].reshape(batch, seq, d_model).astype(x_prompt.dtype)
    y_sample = groups[1][0].reshape(dec_batch, dec_seq, d_model).astype(x_sample.dtype)
    new_state_hgrn = new_h.astype(state_hgrn.dtype)
    new_state_ret = new_r.astype(state_ret.dtype)
    return (y_prompt, y_sample, new_state_hgrn, new_state_ret)
```
